```python
import jax
import jax.numpy as jnp
from jax import lax
import numpy as np

D_MODEL = 1024
BATCH = 16
SEQ = 2048
DEPTH = 2

N_META = 16
CHUNK = 32
FRONT_PAD = (CHUNK - N_META % CHUNK) % CHUNK
F_TINY = 1e-30

GLA_HEADS = 4
GLA_DK = 64
GLA_DV = 128
GLA_K = GLA_HEADS * GLA_DK
GLA_V = GLA_HEADS * GLA_DV
GLA_GATE_RANK = 16
GLA_GATE_NORM = 16.0

RW_HEADS = 8
RW_HD = 64
RW_DIM = RW_HEADS * RW_HD
RW_W_RANK = 64
RW_A_RANK = 64
RW_V_RANK = 32
RW_G_RANK = 128
RW_GN_EPS = 64e-5

HG_HEADS = 4
HG_DK = 128
HG_DV = 128
HG_K = HG_HEADS * HG_DK
HG_V = HG_HEADS * HG_DV

D_FF = 2816
CONV_W = 3
NORM_EPS = 1e-6

GLA_LAYOUT = (("gla_q", GLA_K), ("gla_k", GLA_K), ("gla_v", GLA_V), ("gla_gk", GLA_GATE_RANK), ("gla_g", GLA_V))
HG_LAYOUT = (("hg_q", HG_K), ("hg_f", HG_K), ("hg_i", HG_V), ("hg_g", HG_V))
GATE_LAYOUT = (("gate_gla", D_MODEL), ("gate_rw", D_MODEL), ("gate_hg", D_MODEL))
NON_RW_LAYOUT = GLA_LAYOUT + HG_LAYOUT + GATE_LAYOUT
RW_LAYOUT = (("rw_r", RW_DIM), ("rw_w", RW_W_RANK), ("rw_k", RW_DIM), ("rw_v", RW_DIM), ("rw_a", RW_A_RANK), ("rw_g", RW_G_RANK))
RW_VRES_LAYOUT = (("rw_vr", RW_V_RANK),)
NON_RW = sum(w for _, w in NON_RW_LAYOUT)
RW_SHIFT = sum(w for _, w in RW_LAYOUT)
W_IN = NON_RW + RW_SHIFT

kernel_name = "hybrid_gla_rwkv7_hgrn2_block"


def _rmsnorm(x, w):
    x32 = x.astype(jnp.float32)
    y = x32 * lax.rsqrt(jnp.mean(x32 * x32, axis=-1, keepdims=True) + NORM_EPS)
    return (y * w.astype(jnp.float32)).astype(x.dtype)


def _head_rmsnorm(o, w):
    return o * lax.rsqrt(jnp.mean(o * o, axis=-1, keepdims=True) + NORM_EPS) * w


def _split(p, layout):
    out, off = {}, 0
    for name, width in layout:
        out[name] = p[..., off:off + width]
        off += width
    return out


def _token_shift(p, mu):
    prev = jnp.pad(p[:, :-1], ((0, 0), (1, 0), (0, 0)))
    return p + (prev - p) * mu.astype(p.dtype)


def _chunk_gated_linear_attention(q, k, v, g):
    bsz, t, h, dk = q.shape
    dv = v.shape[-1]
    pad = ((0, 0), (FRONT_PAD, 0), (0, 0), (0, 0))
    q, k, v, g = [jnp.pad(a, pad) for a in (q, k, v, g)]
    n = (t + FRONT_PAD) // CHUNK

    def blk(a):
        return a.reshape(bsz, n, CHUNK, h, a.shape[-1]).transpose(0, 3, 1, 2, 4)

    q, k, v, g = blk(q), blk(k), blk(v), blk(g)
    b = jnp.cumsum(g, axis=3)
    b_ref = b[:, :, :, CHUNK // 2 - 1:CHUNK // 2]
    b_last = b[:, :, :, -1:]
    scores = jnp.einsum('bhnid,bhnjd->bhnij', q * jnp.exp(b - b_ref), k * jnp.exp(b_ref - b))
    causal = jnp.tril(jnp.ones((CHUNK, CHUNK), dtype=bool))
    scores = jnp.where(causal, scores, 0.0)
    o_intra = jnp.einsum('bhnij,bhnjv->bhniv', scores, v)
    inc = jnp.einsum('bhnjd,bhnjv->bhndv', k * jnp.exp(b_last - b), v)
    dec = jnp.exp(b_last[:, :, :, 0])

    def step(s, xs):
        d_n, u_n = xs
        return d_n[..., None] * s + u_n, s

    s0 = jnp.zeros((bsz, h, dk, dv), jnp.float32)
    _, s_start = lax.scan(step, s0, (jnp.moveaxis(dec, 2, 0), jnp.moveaxis(inc, 2, 0)))
    s_start = jnp.moveaxis(s_start, 0, 2)
    o_inter = jnp.einsum('bhnid,bhndv->bhniv', q * jnp.exp(b), s_start)
    o = (o_intra + o_inter).transpose(0, 2, 3, 1, 4).reshape(bsz, n * CHUNK, h, dv)
    return o[:, FRONT_PAD:]


def _gla_branch(f, gk_up, gk_bias, norm_w):
    f32 = jnp.float32
    bsz, t, _ = f["gla_q"].shape
    q = f["gla_q"].astype(f32).reshape(bsz, t, GLA_HEADS, GLA_DK) * (GLA_DK ** -0.5)
    k = f["gla_k"].astype(f32).reshape(bsz, t, GLA_HEADS, GLA_DK)
    v = f["gla_v"].astype(f32).reshape(bsz, t, GLA_HEADS, GLA_DV)
    gk = jax.nn.log_sigmoid((f["gla_gk"] @ gk_up + gk_bias).astype(f32)) / GLA_GATE_NORM
    gk = gk.reshape(bsz, t, GLA_HEADS, GLA_DK)
    o = _chunk_gated_linear_attention(q, k, v, gk)
    g = f["gla_g"].astype(f32).reshape(bsz, t, GLA_HEADS, GLA_DV)
    o = _head_rmsnorm(o, norm_w.astype(f32)) * jax.nn.silu(g)
    return o.reshape(bsz, t, GLA_V)


def _hgrn2_branch(f, lb, norm_w):
    f32 = jnp.float32
    bsz, t, _ = f["hg_f"].shape
    z = f["hg_f"].astype(f32).reshape(bsz, t, HG_HEADS, HG_DK)
    lb = lb.reshape(HG_HEADS, HG_DK)
    forget = lb + (1.0 - lb) * jax.nn.sigmoid(z)
    log_f = jnp.log(jnp.maximum(forget, F_TINY))
    k = (1.0 - lb) * jax.nn.sigmoid(-z)
    q = f["hg_q"].astype(f32).reshape(bsz, t, HG_HEADS, HG_DK)
    i = f["hg_i"].astype(f32).reshape(bsz, t, HG_HEADS, HG_DV)
    o = _chunk_gated_linear_attention(q, k, i, log_f)
    g = f["hg_g"].astype(f32).reshape(bsz, t, HG_HEADS, HG_DV)
    o = _head_rmsnorm(o, norm_w.astype(f32)) * jax.nn.silu(g)
    return o.reshape(bsz, t, HG_V)


def _rwkv7_scan(r, w, k, v, kk, a):
    bsz, _, h, n = r.shape

    def step(s, xs):
        r_t, w_t, k_t, v_t, kk_t, a_t = xs
        sa = jnp.einsum('bhvk,bhk->bhv', s, -kk_t)
        s = (s * w_t[:, :, None, :] + sa[..., None] * (kk_t * a_t)[:, :, None, :]
             + v_t[..., None] * k_t[:, :, None, :])
        return s, jnp.einsum('bhvk,bhk->bhv', s, r_t)

    s0 = jnp.zeros((bsz, h, n, n), jnp.float32)
    xs = tuple(jnp.moveaxis(a_, 1, 0) for a_ in (r, w, k, v, kk, a))
    _, y = lax.scan(step, s0, xs)
    return jnp.moveaxis(y, 0, 1)


def _rwkv7_branch(s, w0, w2, a0, a2, g2, k_k, k_a, r_k, ln_w, ln_b, v_first, v0, v2):
    f32 = jnp.float32
    bsz, t, _ = s["rw_r"].shape
    r = s["rw_r"].astype(f32)
    k = s["rw_k"].astype(f32)
    v = s["rw_v"].astype(f32)
    w_log = -jax.nn.softplus(-(w0 + jnp.tanh(s["rw_w"]) @ w2).astype(f32)) - 0.5
    decay = jnp.exp(-jnp.exp(w_log))
    a = jax.nn.sigmoid((a0 + s["rw_a"] @ a2).astype(f32))
    g = (jax.nn.sigmoid(s["rw_g"]) @ g2).astype(f32)
    if v_first is None:
        v_first = v
    else:
        v = v + (v_first - v) * jax.nn.sigmoid((v0 + s["rw_vr"] @ v2).astype(f32))
    hd = lambda x_: x_.reshape(bsz, t, RW_HEADS, RW_HD)
    kk = hd(k * k_k.astype(f32))
    kk = kk / jnp.maximum(jnp.sqrt(jnp.sum(kk * kk, axis=-1, keepdims=True)), 1e-12)
    k = k * (1.0 + (a - 1.0) * k_a.astype(f32))
    rh, kh, vh = hd(r), hd(k), hd(v)
    y = _rwkv7_scan(rh, hd(decay), kh, vh, kk, hd(a))
    mean = jnp.mean(y, axis=-1, keepdims=True)
    var = jnp.mean(jnp.square(y - mean), axis=-1, keepdims=True)
    y = ((y - mean) * lax.rsqrt(var + RW_GN_EPS)).reshape(bsz, t, RW_DIM)
    y = y * ln_w.astype(f32) + ln_b.astype(f32)
    bonus = jnp.sum(rh * kh * r_k.astype(f32), axis=-1, keepdims=True) * vh
    y = (y + bonus.reshape(bsz, t, RW_DIM)) * g
    return y, v_first


def _conv_ffn(h, w_up, conv_w, conv_b, w_down):
    t = h.shape[1]
    u = h @ w_up
    up = jnp.pad(u, ((0, 0), (CONV_W - 1, 0), (0, 0)))
    c = conv_b + up[:, 0:t] * conv_w[0]
    for j in range(1, CONV_W):
        c = c + up[:, j:j + t] * conv_w[j]
    gate, val = c[..., :D_FF], c[..., D_FF:]
    return (jax.nn.silu(gate) * val) @ w_down


def setup_inputs(seed: int = 0) -> dict:
    key = jax.random.key(seed)
    ks = iter(jax.random.split(key, 40))
    nrm = lambda shape, scale: jax.random.normal(next(ks), shape, jnp.float32) * scale
    gain = lambda shape: 1.0 + nrm(shape, 0.02)
    uni = lambda shape, lo, hi: jax.random.uniform(next(ks), shape, jnp.float32, lo, hi)
    L = DEPTH
    return {
        "x": nrm((BATCH, SEQ, D_MODEL), 1.0),
        "meta": nrm((N_META, D_MODEL), 1.0),
        "mix_norm": gain((L, D_MODEL)),
        "w_in": nrm((L, D_MODEL, W_IN), D_MODEL ** -0.5),
        "w_in_vres": nrm((L - 1, D_MODEL, RW_V_RANK), D_MODEL ** -0.5),
        "rw_mu": uni((L, RW_SHIFT), 0.0, 1.0),
        "rw_mu_vres": uni((L - 1, RW_V_RANK), 0.0, 1.0),
        "gla_gk_up": nrm((L, GLA_GATE_RANK, GLA_K), GLA_GATE_RANK ** -0.5),
        "gla_gk_bias": nrm((L, GLA_K), 0.1),
        "gla_norm": gain((L, GLA_DV)),
        "rw_w0": uni((L, RW_DIM), -6.0, -1.0),
        "rw_w2": nrm((L, RW_W_RANK, RW_DIM), 0.1),
        "rw_a0": nrm((L, RW_DIM), 0.1),
        "rw_a2": nrm((L, RW_A_RANK, RW_DIM), 0.1),
        "rw_v0": nrm((L - 1, RW_DIM), 0.1),
        "rw_v2": nrm((L - 1, RW_V_RANK, RW_DIM), 0.1),
        "rw_g2": nrm((L, RW_G_RANK, RW_DIM), RW_G_RANK ** -0.5),
        "rw_kk": 0.85 + nrm((L, RW_DIM), 0.05),
        "rw_ka": 1.0 + nrm((L, RW_DIM), 0.05),
        "rw_rk": nrm((L, RW_HEADS, RW_HD), 0.1),
        "rw_ln_w": gain((L, RW_DIM)),
        "rw_ln_b": nrm((L, RW_DIM), 0.02),
        "hg_lb_logits": nrm((L, HG_K), 0.1),
        "hg_norm": gain((L, HG_DV)),
        "w_out_gla": nrm((L, GLA_V, D_MODEL), GLA_V ** -0.5),
        "w_out_rw": nrm((L, RW_DIM, D_MODEL), RW_DIM ** -0.5),
        "w_out_hg": nrm((L, HG_V, D_MODEL), HG_V ** -0.5),
        "w_out": nrm((L, D_MODEL, D_MODEL), D_MODEL ** -0.5),
        "ffn_norm": gain((L, D_MODEL)),
        "w_up": nrm((L, D_MODEL, 2 * D_FF), D_MODEL ** -0.5),
        "conv_w": nrm((L, CONV_W, 2 * D_FF), CONV_W ** -0.5),
        "conv_b": nrm((L, 2 * D_FF), 0.02),
        "w_down": nrm((L, D_FF, D_MODEL), D_FF ** -0.5),
        "final_norm": gain((D_MODEL,)),
    }


def reference(x, meta, mix_norm, w_in, w_in_vres, rw_mu, rw_mu_vres, gla_gk_up, gla_gk_bias, gla_norm,
              rw_w0, rw_w2, rw_a0, rw_a2, rw_v0, rw_v2, rw_g2, rw_kk, rw_ka, rw_rk, rw_ln_w, rw_ln_b,
              hg_lb_logits, hg_norm, w_out_gla, w_out_rw, w_out_hg, w_out, ffn_norm, w_up, conv_w, conv_b,
              w_down, final_norm):
    bsz = x.shape[0]
    dt = x.dtype
    z = jnp.concatenate([jnp.broadcast_to(meta.astype(dt)[None], (bsz, N_META, D_MODEL)), x], axis=1)
    lb_p = jax.nn.softmax(hg_lb_logits.astype(jnp.float32), axis=0)
    lb_all = jnp.cumsum(lb_p, axis=0) - lb_p[0:1]
    v_first = None
    for i in range(DEPTH):
        h = _rmsnorm(z, mix_norm[i])
        if i == 0:
            w_cat, mu, rw_layout, v0_i, v2_i = w_in[0], rw_mu[0], RW_LAYOUT, None, None
        else:
            w_cat = jnp.concatenate([w_in[i], w_in_vres[i - 1]], axis=1)
            mu = jnp.concatenate([rw_mu[i], rw_mu_vres[i - 1]], axis=0)
            rw_layout, v0_i, v2_i = RW_LAYOUT + RW_VRES_LAYOUT, rw_v0[i - 1], rw_v2[i - 1]
        p = h @ w_cat
        f = _split(p[..., :NON_RW], NON_RW_LAYOUT)
        s = _split(_token_shift(p[..., NON_RW:], mu), rw_layout)
        y_gla = _gla_branch(f, gla_gk_up[i], gla_gk_bias[i], gla_norm[i]).astype(dt)
        y_rw, v_first = _rwkv7_branch(s, rw_w0[i], rw_w2[i], rw_a0[i], rw_a2[i], rw_g2[i], rw_kk[i], rw_ka[i],
                                      rw_rk[i], rw_ln_w[i], rw_ln_b[i], v_first, v0_i, v2_i)
        y_hg = _hgrn2_branch(f, lb_all[i], hg_norm[i]).astype(dt)
        merged = (jax.nn.sigmoid(f["gate_gla"]) * (y_gla @ w_out_gla[i])
                  + jax.nn.sigmoid(f["gate_rw"]) * (y_rw.astype(dt) @ w_out_rw[i])
                  + jax.nn.sigmoid(f["gate_hg"]) * (y_hg @ w_out_hg[i]))
        z = z + merged @ w_out[i]
        z = z + _conv_ffn(_rmsnorm(z, ffn_norm[i]), w_up[i], conv_w[i], conv_b[i], w_down[i])
    return _rmsnorm(z, final_norm)[:, N_META:]
```

```python
import functools

import jax
import jax.numpy as jnp
from jax import lax
from jax.experimental import pallas as pl
from jax.experimental.pallas import tpu as pltpu

F32 = jnp.float32
BF16 = jnp.bfloat16

D_MODEL = 1024
N_META = 16
F_TINY = 1e-30
NORM_EPS = 1e-6

GLA_HEADS, GLA_DK, GLA_DV = 4, 64, 128
GLA_K, GLA_V = GLA_HEADS * GLA_DK, GLA_HEADS * GLA_DV
GLA_GATE_RANK = 16
GLA_GATE_NORM = 16.0

RW_HEADS, RW_HD = 8, 64
RW_DIM = RW_HEADS * RW_HD
RW_W_RANK, RW_A_RANK, RW_V_RANK, RW_G_RANK = 64, 64, 32, 128
RW_GN_EPS = 64e-5
RW_GROUP_HEADS = 4
RW_GROUP = RW_GROUP_HEADS * RW_HD

HG_HEADS, HG_DK, HG_DV = 4, 128, 128
HG_K, HG_V = HG_HEADS * HG_DK, HG_HEADS * HG_DV

D_FF = 2816
CONV_W = 3

_REF_LAYOUT = (
    ("gla_q", GLA_K), ("gla_k", GLA_K), ("gla_v", GLA_V), ("gla_gk", GLA_GATE_RANK), ("gla_g", GLA_V),
    ("hg_q", HG_K), ("hg_f", HG_K), ("hg_i", HG_V), ("hg_g", HG_V),
    ("gate_gla", D_MODEL), ("gate_rw", D_MODEL), ("gate_hg", D_MODEL),
    ("rw_r", RW_DIM), ("rw_w", RW_W_RANK), ("rw_k", RW_DIM), ("rw_v", RW_DIM), ("rw_a", RW_A_RANK),
    ("rw_g", RW_G_RANK),
)
_REF_OFF = {}
_o = 0
for _n, _w in _REF_LAYOUT:
    _REF_OFF[_n] = (_o, _w)
    _o += _w
W_IN = _o

LANE = 128
SUBLANE = 8
_K_LAYOUT = (
    ("gate_gla", 1024), ("gate_rw", 1024), ("gate_hg", 1024),
    ("gla_v", 512), ("gla_g", 512), ("hg_q", 512), ("hg_f", 512), ("hg_i", 512), ("hg_g", 512),
    ("rw_r", 512), ("rw_k", 512), ("rw_v", 512),
    ("gla_q", 256), ("gla_k", 256),
    ("rw_g", 128), ("rw_wa", 128), ("misc", 128),
)
_K_OFF = {}
_o = 0
for _n, _w in _K_LAYOUT:
    assert _o % _w == 0
    _K_OFF[_n] = (_o, _w)
    _o += _w
IN_TN = 512
WP = -(-_o // IN_TN) * IN_TN
MISC_GK = 0
MISC_VR = GLA_GATE_RANK

CHUNK = 64
SUB = 32
VMEM_LIMIT = 56 * 1024 * 1024


def _mm(a, b):
    return jnp.dot(a.astype(BF16), b.astype(BF16), preferred_element_type=F32)


def _mm_nt(a, b):
    return lax.dot_general(a.astype(BF16), b.astype(BF16), (((1,), (1,)), ((), ())),
                           preferred_element_type=F32)


def _mm_tn(a, b):
    return lax.dot_general(a.astype(BF16), b.astype(BF16), (((0,), (0,)), ((), ())),
                           preferred_element_type=F32)


def _split3(x):
    x1 = x.astype(BF16)
    r = x - x1.astype(F32)
    x2 = r.astype(BF16)
    x3 = (r - x2.astype(F32)).astype(BF16)
    return x1, x2, x3


def _mm_exact_rhs(a_f32, b_bf16):
    a1, a2, a3 = _split3(a_f32)
    d = lambda t: jnp.dot(t, b_bf16, preferred_element_type=F32)
    return d(a1) + d(a2) + d(a3)


def _cumsum_rows(tril_bf16, g):
    g1, g2, g3 = _split3(g)
    d = lambda t: jnp.dot(tril_bf16, t, preferred_element_type=F32)
    return d(g1) + d(g2) + d(g3)


def _sigmoid(x):
    return 1.0 / (1.0 + jnp.exp(-x))


def _softplus(x):
    return jnp.maximum(x, 0.0) + jnp.log(1.0 + jnp.exp(-jnp.abs(x)))


def _rms(x, w):
    return x * lax.rsqrt(jnp.mean(x * x, axis=-1, keepdims=True) + NORM_EPS) * w


def _stack_heads(x, heads, width):
    lane = lax.broadcasted_iota(jnp.int32, x.shape, 1)
    return jnp.concatenate(
        [jnp.where((lane >= h * width) & (lane < (h + 1) * width), x, 0.0) for h in range(heads)], axis=0)


def _inproj_kernel(z_ref, nw_ref, w_ref, o_ref, h_ref):
    @pl.when(pl.program_id(1) == 0)
    def _():
        h_ref[...] = _rms(z_ref[...], nw_ref[...]).astype(BF16)

    o_ref[...] = jnp.dot(h_ref[...], w_ref[...], preferred_element_type=F32)


def _inproj(z2d, norm_w, w_bf16, tm):
    n = z2d.shape[0]
    return pl.pallas_call(
        _inproj_kernel,
        grid=(n // tm, WP // IN_TN),
        in_specs=[
            pl.BlockSpec((tm, D_MODEL), lambda i, j: (i, 0)),
            pl.BlockSpec((1, D_MODEL), lambda i, j: (0, 0)),
            pl.BlockSpec((D_MODEL, IN_TN), lambda i, j: (0, j)),
        ],
        out_specs=pl.BlockSpec((tm, IN_TN), lambda i, j: (i, j)),
        out_shape=jax.ShapeDtypeStruct((n, WP), F32),
        scratch_shapes=[pltpu.VMEM((tm, D_MODEL), BF16)],
        compiler_params=pltpu.CompilerParams(
            dimension_semantics=("arbitrary", "arbitrary"), vmem_limit_bytes=VMEM_LIMIT),
        name="inproj",
    )(z2d, norm_w, w_bf16)


def _gla_core(q, k, v, g, tril, s_ref, heads, dk, first):
    c_rows, hk = q.shape
    assert c_rows == CHUNK and CHUNK == 2 * SUB
    dv = v.shape[1] // heads

    @pl.when(first)
    def _():
        s_ref[...] = jnp.zeros_like(s_ref)

    c = _cumsum_rows(tril, g)
    row = lax.broadcasted_iota(jnp.int32, (c_rows, hk), 0)
    blk1 = row >= SUB
    r0 = c[SUB // 2 - 1:SUB // 2, :]
    r1 = c[SUB + SUB // 2 - 1:SUB + SUB // 2, :]
    c_last = c[c_rows - 1:c_rows, :]
    kt = k * jnp.exp(jnp.where(blk1, r1, r0) - c)
    qa = q * jnp.exp(c - r0)
    qb = jnp.where(blk1, q * jnp.exp(jnp.where(blk1, c - r1, 0.0)), 0.0)
    kt0 = jnp.where(blk1, 0.0, kt)
    kt1 = jnp.where(blk1, kt, 0.0)
    st = functools.partial(_stack_heads, heads=heads, width=dk)
    qs = jnp.concatenate([st(qa), st(qb)], axis=1)
    ks = jnp.concatenate([st(kt0), st(kt1)], axis=1)
    a = _mm_nt(qs, ks)
    n_st = heads * c_rows
    ri = lax.broadcasted_iota(jnp.int32, (n_st, n_st), 0)
    ci = lax.broadcasted_iota(jnp.int32, (n_st, n_st), 1)
    a = jnp.where(ci <= ri, a, 0.0)
    vs = jnp.concatenate([v[:, h * dv:(h + 1) * dv] for h in range(heads)], axis=0)
    s_t = s_ref[...]
    o = _mm(a, vs) + _mm_nt(st(q * jnp.exp(c)), s_t)
    kd = k * jnp.exp(c_last - c)
    s_ref[...] = s_t * jnp.exp(c_last) + _mm_tn(vs, st(kd))
    return o


def _gla_finish(o_st, norm_w, gate, heads):
    c_rows = o_st.shape[0] // heads
    o_n = o_st * lax.rsqrt(jnp.mean(o_st * o_st, axis=-1, keepdims=True) + NORM_EPS) * norm_w
    o = jnp.concatenate([o_n[h * c_rows:(h + 1) * c_rows] for h in range(heads)], axis=1)
    return o * (gate * _sigmoid(gate))


def _gla_kernel(q_ref, k_ref, v_ref, misc_ref, og_ref, gkup_ref, gkb_ref, nw_ref, tril_ref, y_ref, s_ref):
    q = q_ref[...] * (GLA_DK ** -0.5)
    x = _mm(misc_ref[...], gkup_ref[...]) + gkb_ref[...]
    g = -_softplus(-x) * (1.0 / GLA_GATE_NORM)
    o = _gla_core(q, k_ref[...], v_ref[...], g, tril_ref[...], s_ref, GLA_HEADS, GLA_DK,
                  pl.program_id(1) == 0)
    y_ref[...] = _gla_finish(o, nw_ref[...], og_ref[...], GLA_HEADS).astype(y_ref.dtype)


def _hgrn_kernel(q_ref, f_ref, i_ref, og_ref, lb_ref, nw_ref, tril_ref, y_ref, s_ref):
    z = f_ref[...]
    lb = lb_ref[...]
    forget = lb + (1.0 - lb) * _sigmoid(z)
    g = jnp.log(jnp.maximum(forget, F_TINY))
    k = (1.0 - lb) * _sigmoid(-z)
    o = _gla_core(q_ref[...], k, i_ref[...], g, tril_ref[...], s_ref, HG_HEADS, HG_DK,
                  pl.program_id(1) == 0)
    y_ref[...] = _gla_finish(o, nw_ref[...], og_ref[...], HG_HEADS).astype(y_ref.dtype)


def _pcol(name, tp):
    off, width = _K_OFF[name]
    nt = tp // CHUNK
    cb = off // width
    return pl.BlockSpec((CHUNK, width), lambda b, t: (b * nt + t, cb))


def _const_spec(shape):
    nd = len(shape)
    return pl.BlockSpec(shape, lambda b, t: (0,) * nd)


def _seq_params():
    return pltpu.CompilerParams(dimension_semantics=("arbitrary", "arbitrary"), vmem_limit_bytes=VMEM_LIMIT)


def _gla_mixer(p, gk_up_pad, gk_bias, norm_w, tril, bsz, tp):
    n = p.shape[0]
    return pl.pallas_call(
        _gla_kernel,
        grid=(bsz, tp // CHUNK),
        in_specs=[_pcol("gla_q", tp), _pcol("gla_k", tp), _pcol("gla_v", tp), _pcol("misc", tp),
                  _pcol("gla_g", tp), _const_spec(gk_up_pad.shape), _const_spec(gk_bias.shape),
                  _const_spec(norm_w.shape), _const_spec(tril.shape)],
        out_specs=pl.BlockSpec((CHUNK, GLA_V), lambda b, t: (b * (tp // CHUNK) + t, 0)),
        out_shape=jax.ShapeDtypeStruct((n, GLA_V), BF16),
        scratch_shapes=[pltpu.VMEM((GLA_DV, GLA_K), F32)],
        compiler_params=_seq_params(),
        name="gla_mixer",
    )(p, p, p, p, p, gk_up_pad, gk_bias, norm_w, tril)


def _hgrn_mixer(p, lb, norm_w, tril, bsz, tp):
    n = p.shape[0]
    return pl.pallas_call(
        _hgrn_kernel,
        grid=(bsz, tp // CHUNK),
        in_specs=[_pcol("hg_q", tp), _pcol("hg_f", tp), _pcol("hg_i", tp), _pcol("hg_g", tp),
                  _const_spec(lb.shape), _const_spec(norm_w.shape), _const_spec(tril.shape)],
        out_specs=pl.BlockSpec((CHUNK, HG_V), lambda b, t: (b * (tp // CHUNK) + t, 0)),
        out_shape=jax.ShapeDtypeStruct((n, HG_V), BF16),
        scratch_shapes=[pltpu.VMEM((HG_DV, HG_K), F32)],
        compiler_params=_seq_params(),
        name="hgrn_mixer",
    )(p, p, p, p, lb, norm_w, tril)


def _unit_lower_inverse(n_strict, blk16, blk32):
    size = n_strict.shape[0]
    ri = lax.broadcasted_iota(jnp.int32, (size, size), 0)
    ci = lax.broadcasted_iota(jnp.int32, (size, size), 1)
    eye = jnp.where(ri == ci, 1.0, 0.0)
    n16 = jnp.where(blk16, n_strict, 0.0)
    n32 = jnp.where(blk32 & jnp.logical_not(blk16), n_strict, 0.0)
    n64 = jnp.where(blk32, 0.0, n_strict)
    t = eye + n16
    m = n16
    for _ in range(3):
        m = _mm(m, m)
        t = t + _mm(m, t)
    t = t + _mm(_mm(t, n32), t)
    t = t + _mm(_mm(t, n64), t)
    return t


def _rwkv_kernel(has_vres, *refs):
    if has_vres:
        (r_ref, k_ref, v_ref, g_ref, wa_ref, misc_ref, vf_ref,
         mu_r_ref, mu_k_ref, mu_v_ref, mu_g_ref, mu_wa_ref, mu_misc_ref,
         w0_ref, w2_ref, a0_ref, a2_ref, g2_ref, v0_ref, v2_ref,
         kk_ref, ka_ref, rk_ref, lnw_ref, lnb_ref, seg_ref, tril_ref,
         y_ref, s_ref, carry_ref) = refs
    else:
        (r_ref, k_ref, v_ref, g_ref, wa_ref, misc_ref,
         mu_r_ref, mu_k_ref, mu_v_ref, mu_g_ref, mu_wa_ref, mu_misc_ref,
         w0_ref, w2_ref, a0_ref, a2_ref, g2_ref,
         kk_ref, ka_ref, rk_ref, lnw_ref, lnb_ref, seg_ref, tril_ref,
         y_ref, vf_out_ref, s_ref, carry_ref) = refs

    first = pl.program_id(1) == 0

    @pl.when(first)
    def _():
        s_ref[...] = jnp.zeros_like(s_ref)
        carry_ref[...] = jnp.zeros_like(carry_ref)

    srcs = (r_ref, k_ref, v_ref, g_ref, wa_ref, misc_ref)
    mus = (mu_r_ref, mu_k_ref, mu_v_ref, mu_g_ref, mu_wa_ref, mu_misc_ref)
    shifted = []
    off = 0
    for src, mu in zip(srcs, mus):
        p = src[...]
        width = p.shape[1]
        prev_row = carry_ref[0:1, off:off + width]
        rolled = pltpu.roll(p, 1, 0)
        row = lax.broadcasted_iota(jnp.int32, p.shape, 0)
        prev = jnp.where(row == 0, prev_row, rolled)
        shifted.append(p + (prev - p) * mu[...])
        carry_ref[0:1, off:off + width] = p[CHUNK - 1:CHUNK, :]
        off += width
    r, k, v, s_g, s_wa, s_misc = shifted

    seg = seg_ref[...]
    w_log = -_softplus(-(w0_ref[...] + _mm(jnp.tanh(s_wa), w2_ref[...]))) - 0.5
    lw = -jnp.exp(w_log)
    a = _sigmoid(a0_ref[...] + _mm(s_wa, a2_ref[...]))
    g = _mm(_sigmoid(s_g), g2_ref[...])
    if has_vres:
        v = v + (vf_ref[...] - v) * _sigmoid(v0_ref[...] + _mm(s_misc, v2_ref[...]))
    else:
        vf_out_ref[...] = v
    kk = k * kk_ref[...]
    kk = kk / jnp.maximum(jnp.sqrt(_mm_exact_rhs(kk * kk, seg)), 1e-12)
    k = k * (1.0 + (a - 1.0) * ka_ref[...])
    alpha = -kk
    beta = kk * a

    c = _cumsum_rows(tril_ref[...], lw)
    ce = c - lw
    mid = c[CHUNK // 2 - 1:CHUNK // 2, :]
    c_last = c[CHUNK - 1:CHUNK, :]
    e_in = jnp.exp(c - mid)
    e_out = jnp.exp(mid - c)
    at = alpha * jnp.exp(ce - mid)
    rt = r * e_in
    bh = beta * e_out
    kh = k * e_out
    ae = alpha * jnp.exp(ce)
    rs = r * jnp.exp(c)
    e_end = jnp.exp(c_last - c)
    be = beta * e_end
    ke = k * e_end
    dec = jnp.exp(c_last)

    gs = RW_GROUP
    n_st = RW_GROUP_HEADS * CHUNK
    ri = lax.broadcasted_iota(jnp.int32, (n_st, n_st), 0)
    ci = lax.broadcasted_iota(jnp.int32, (n_st, n_st), 1)
    strict = ci < ri
    blk16 = (ri >> 4) == (ci >> 4)
    blk32 = (ri >> 5) == (ci >> 5)
    tcol = lax.broadcasted_iota(jnp.int32, (CHUNK, n_st), 1) & (CHUNK - 1)
    trow = lax.broadcasted_iota(jnp.int32, (CHUNK, n_st), 0)
    incl = tcol <= trow
    st = functools.partial(_stack_heads, heads=RW_GROUP_HEADS, width=RW_HD)

    ys = []
    for gi in range(RW_HEADS // RW_GROUP_HEADS):
        sl = slice(gi * gs, (gi + 1) * gs)
        s_g0 = s_ref[gi]
        bm = st(bh[:, sl])
        km = st(kh[:, sl])
        am = st(at[:, sl])
        v_bd = st(v[:, sl])
        n_ab = jnp.where(strict, _mm_nt(am, bm), 0.0)
        n_ak = jnp.where(strict, _mm_nt(am, km), 0.0)
        a_rb = jnp.where(incl, _mm_nt(rt[:, sl], bm), 0.0)
        a_rk = jnp.where(incl, _mm_nt(rt[:, sl], km), 0.0)
        x = _mm_nt(st(ae[:, sl]), s_g0) + _mm(n_ak, v_bd)
        u = _mm(_unit_lower_inverse(n_ab, blk16, blk32), x)
        ys.append(_mm_nt(rs[:, sl], s_g0) + _mm(a_rb, u) + _mm(a_rk, v_bd))
        s_ref[gi] = s_g0 * dec[:, sl] + _mm_tn(u, st(be[:, sl])) + _mm_tn(v_bd, st(ke[:, sl]))
    y = jnp.concatenate(ys, axis=1)

    inv_hd = 1.0 / RW_HD
    mean = _mm_exact_rhs(y, seg) * inv_hd
    yc = y - mean
    var = _mm_exact_rhs(yc * yc, seg) * inv_hd
    y = yc * lax.rsqrt(var + RW_GN_EPS) * lnw_ref[...] + lnb_ref[...]
    bonus = _mm_exact_rhs(r * k * rk_ref[...], seg) * v
    y_ref[...] = ((y + bonus) * g).astype(y_ref.dtype)


def _rwkv_mixer(p, v_first, mus, prm, seg, tril, bsz, tp):
    n = p.shape[0]
    nt = tp // CHUNK
    has_vres = v_first is not None
    row_spec = pl.BlockSpec((CHUNK, RW_DIM), lambda b, t: (b * nt + t, 0))
    in_specs = [_pcol("rw_r", tp), _pcol("rw_k", tp), _pcol("rw_v", tp), _pcol("rw_g", tp),
                _pcol("rw_wa", tp), _pcol("misc", tp)]
    args = [p, p, p, p, p, p]
    if has_vres:
        in_specs.append(row_spec)
        args.append(v_first)
    consts = list(mus) + [prm["w0"], prm["w2"], prm["a0"], prm["a2"], prm["g2"]]
    if has_vres:
        consts += [prm["v0"], prm["v2"]]
    consts += [prm["kk"], prm["ka"], prm["rk"], prm["lnw"], prm["lnb"], seg, tril]
    in_specs += [_const_spec(c.shape) for c in consts]
    args += consts
    out_shape = [jax.ShapeDtypeStruct((n, RW_DIM), BF16)]
    out_specs = [row_spec]
    if not has_vres:
        out_shape.append(jax.ShapeDtypeStruct((n, RW_DIM), F32))
        out_specs.append(row_spec)
    carry_w = 3 * RW_DIM + 3 * LANE
    res = pl.pallas_call(
        functools.partial(_rwkv_kernel, has_vres),
        grid=(bsz, nt),
        in_specs=in_specs,
        out_specs=out_specs,
        out_shape=out_shape,
        scratch_shapes=[pltpu.VMEM((RW_HEADS // RW_GROUP_HEADS, RW_GROUP, RW_GROUP), F32),
                        pltpu.VMEM((SUBLANE, carry_w), F32)],
        compiler_params=_seq_params(),
        name="rwkv_mixer",
    )(*args)
    return (res[0], v_first) if has_vres else (res[0], res[1])


def _merge_kernel(z_ref, gg_ref, gr_ref, gh_ref, yg_ref, yr_ref, yh_ref, wa_ref, wb_ref, wc_ref, wo_ref, o_ref):
    d = lambda y, w: jnp.dot(y[...], w[...], preferred_element_type=F32)
    m = (_sigmoid(gg_ref[...]) * d(yg_ref, wa_ref) + _sigmoid(gr_ref[...]) * d(yr_ref, wb_ref)
         + _sigmoid(gh_ref[...]) * d(yh_ref, wc_ref))
    o_ref[...] = z_ref[...] + _mm(m, wo_ref[...])


def _merge(z2d, p, y_gla, y_rw, y_hg, wa, wb, wc, wo, tm):
    n = z2d.shape[0]
    row = lambda w: pl.BlockSpec((tm, w), lambda i: (i, 0))
    gate = lambda name: pl.BlockSpec((tm, D_MODEL), lambda i, cb=_K_OFF[name][0] // D_MODEL: (i, cb))
    full = lambda a: pl.BlockSpec(a.shape, lambda i: (0, 0))
    return pl.pallas_call(
        _merge_kernel,
        grid=(n // tm,),
        in_specs=[row(D_MODEL), gate("gate_gla"), gate("gate_rw"), gate("gate_hg"),
                  row(GLA_V), row(RW_DIM), row(HG_V), full(wa), full(wb), full(wc), full(wo)],
        out_specs=row(D_MODEL),
        out_shape=jax.ShapeDtypeStruct((n, D_MODEL), F32),
        compiler_params=pltpu.CompilerParams(dimension_semantics=("arbitrary",), vmem_limit_bytes=VMEM_LIMIT),
        name="merge",
    )(z2d, p, p, p, y_gla, y_rw, y_hg, wa, wb, wc, wo)


FFN_FC = 256
FFN_HALO = SUBLANE


def _ffn_kernel(final, z_ref, halo_ref, nw_ref, wug_ref, wuv_ref, cwg_ref, cwv_ref, cbg_ref, cbv_ref, wd_ref,
                fnw_ref, o_ref, acc_ref):
    tt = z_ref.shape[0]
    zt = z_ref[...]
    nw = nw_ref[...]
    h = jnp.concatenate([_rms(halo_ref[...], nw), _rms(zt, nw)], axis=0).astype(BF16)
    keep = (lax.broadcasted_iota(jnp.int32, (tt + FFN_HALO, 1), 0) >= FFN_HALO) | (pl.program_id(1) > 0)
    acc_ref[...] = zt

    def conv(u, cw, cb):
        u = jnp.where(keep, u, 0.0)
        u1 = pltpu.roll(u, 1, 0)
        u2 = pltpu.roll(u, 2, 0)
        c = cb + u2 * cw[0:1, :] + u1 * cw[1:2, :] + u * cw[2:3, :]
        return c[FFN_HALO:, :]

    def body(j, carry):
        cg = conv(jnp.dot(h, wug_ref[j], preferred_element_type=F32), cwg_ref[j], cbg_ref[j])
        cv = conv(jnp.dot(h, wuv_ref[j], preferred_element_type=F32), cwv_ref[j], cbv_ref[j])
        act = (cg * _sigmoid(cg) * cv).astype(BF16)
        acc_ref[...] += jnp.dot(act, wd_ref[j], preferred_element_type=F32)
        return carry

    lax.fori_loop(0, wug_ref.shape[0], body, 0)
    out = acc_ref[...]
    if final:
        out = _rms(out, fnw_ref[...])
    o_ref[...] = out


def _ffn(z3d, norm_w, wug, wuv, cwg, cwv, cbg, cbv, wd, final_w, tt, final):
    bsz, tp, _ = z3d.shape
    hb = tt // FFN_HALO
    once = lambda a: pl.BlockSpec(a.shape, lambda b, t: (0,) * a.ndim, pipeline_mode=pl.Buffered(1))
    return pl.pallas_call(
        functools.partial(_ffn_kernel, final),
        grid=(bsz, tp // tt),
        in_specs=[pl.BlockSpec((None, tt, D_MODEL), lambda b, t: (b, t, 0)),
                  pl.BlockSpec((None, FFN_HALO, D_MODEL), lambda b, t: (b, jnp.maximum(t * hb - 1, 0), 0)),
                  once(norm_w), once(wug), once(wuv), once(cwg), once(cwv), once(cbg), once(cbv), once(wd),
                  once(final_w)],
        out_specs=pl.BlockSpec((None, tt, D_MODEL), lambda b, t: (b, t, 0)),
        out_shape=jax.ShapeDtypeStruct((bsz, tp, D_MODEL), F32),
        scratch_shapes=[pltpu.VMEM((tt, D_MODEL), F32)],
        compiler_params=_seq_params(),
        name="conv_ffn",
    )(z3d, z3d, norm_w, wug, wuv, cwg, cwv, cbg, cbv, wd, final_w)


def _largest_tile(n, cands):
    for c in cands:
        if n % c == 0:
            return c
    raise ValueError(f"no tile for {n}")


def _ref_cols(w, name):
    off, width = _REF_OFF[name]
    return w[..., off:off + width]


def _pack_columns(w, vres):
    lead = w.shape[:-1]
    zeros = lambda n: jnp.zeros(lead + (n,), w.dtype)
    vr = vres if vres is not None else zeros(RW_V_RANK)
    parts = []
    for name, width in _K_LAYOUT:
        if name == "rw_wa":
            parts.append(jnp.concatenate([_ref_cols(w, "rw_w"), _ref_cols(w, "rw_a")], axis=-1))
        elif name == "misc":
            parts.append(jnp.concatenate(
                [_ref_cols(w, "gla_gk"), vr, zeros(width - GLA_GATE_RANK - RW_V_RANK)], axis=-1))
        else:
            parts.append(_ref_cols(w, name))
    used = sum(wd for _, wd in _K_LAYOUT)
    parts.append(zeros(WP - used))
    return jnp.concatenate(parts, axis=-1)


def _pad_rows(w, lo, total):
    return jnp.pad(w, ((lo, total - lo - w.shape[0]), (0, 0)))


def kernel(x, meta, mix_norm, w_in, w_in_vres, rw_mu, rw_mu_vres, gla_gk_up, gla_gk_bias, gla_norm, rw_w0,
           rw_w2, rw_a0, rw_a2, rw_v0, rw_v2, rw_g2, rw_kk, rw_ka, rw_rk, rw_ln_w, rw_ln_b, hg_lb_logits,
           hg_norm, w_out_gla, w_out_rw, w_out_hg, w_out, ffn_norm, w_up, conv_w, conv_b, w_down, final_norm):
    bsz, seq, _ = x.shape
    depth = w_in.shape[0]
    t_real = seq + N_META
    tp = -(-t_real // CHUNK) * CHUNK
    n = bsz * tp
    tm = _largest_tile(n, (1024, 512, 256, 128, 64))
    tm_merge = _largest_tile(n, (512, 256, 128, 64))
    tt =_largest_tile(tp, (704, 512, 384, 256, 192, 128, 64))

    z = jnp.concatenate([jnp.broadcast_to(meta.astype(x.dtype)[None], (bsz, N_META, D_MODEL)), x,
                         jnp.zeros((bsz, tp - t_real, D_MODEL), x.dtype)], axis=1)

    row = lambda a: a.reshape(1, -1).astype(F32)
    tril = jnp.tril(jnp.ones((CHUNK, CHUNK), F32)).astype(BF16)
    head_of = jnp.arange(RW_DIM) // RW_HD
    seg = (head_of[:, None] == head_of[None, :]).astype(BF16)
    lb_p = jax.nn.softmax(hg_lb_logits.astype(F32), axis=0)
    lb_all = jnp.cumsum(lb_p, axis=0) - lb_p[0:1]

    nj = D_FF // FFN_FC
    v_first = None
    for i in range(depth):
        vres_w = w_in_vres[i - 1] if i > 0 else None
        vres_mu = rw_mu_vres[i - 1] if i > 0 else None
        w_cat = _pack_columns(w_in[i], vres_w).astype(BF16)
        mu_full = jnp.concatenate([jnp.zeros((W_IN - rw_mu.shape[1],), F32), rw_mu[i].astype(F32)])
        mu_cat = _pack_columns(mu_full, vres_mu)
        mus = [row(mu_cat[_K_OFF[nm][0]:_K_OFF[nm][0] + _K_OFF[nm][1]])
               for nm in ("rw_r", "rw_k", "rw_v", "rw_g", "rw_wa", "misc")]

        z2d = z.reshape(n, D_MODEL)
        p = _inproj(z2d, row(mix_norm[i]), w_cat, tm)

        gk_up_pad = _pad_rows(gla_gk_up[i], MISC_GK, LANE).astype(BF16)
        y_gla = _gla_mixer(p, gk_up_pad, row(gla_gk_bias[i]), row(gla_norm[i]), tril, bsz, tp)
        y_hg = _hgrn_mixer(p, row(lb_all[i]), row(hg_norm[i]), tril, bsz, tp)

        prm = {
            "w0": row(rw_w0[i]), "w2": _pad_rows(rw_w2[i], 0, LANE).astype(BF16),
            "a0": row(rw_a0[i]), "a2": _pad_rows(rw_a2[i], RW_W_RANK, LANE).astype(BF16),
            "g2": rw_g2[i].astype(BF16),
            "kk": row(rw_kk[i]), "ka": row(rw_ka[i]), "rk": row(rw_rk[i]),
            "lnw": row(rw_ln_w[i]), "lnb": row(rw_ln_b[i]),
        }
        if i > 0:
            prm["v0"] = row(rw_v0[i - 1])
            prm["v2"] = _pad_rows(rw_v2[i - 1], MISC_VR, LANE).astype(BF16)
        y_rw, v_first = _rwkv_mixer(p, v_first, mus, prm, seg, tril, bsz, tp)

        z2d = _merge(z2d, p, y_gla, y_rw, y_hg, w_out_gla[i].astype(BF16), w_out_rw[i].astype(BF16),
                     w_out_hg[i].astype(BF16), w_out[i].astype(BF16), tm_merge)

        chunked = lambda a: a.reshape(a.shape[0], nj, FFN_FC).transpose(1, 0, 2)
        wug = chunked(w_up[i][:, :D_FF]).astype(BF16)
        wuv = chunked(w_up[i][:, D_FF:]).astype(BF16)
        cwg = jnp.pad(chunked(conv_w[i][:, :D_FF]), ((0, 0), (0, SUBLANE - CONV_W), (0, 0))).astype(F32)
        cwv = jnp.pad(chunked(conv_w[i][:, D_FF:]), ((0, 0), (0, SUBLANE - CONV_W), (0, 0))).astype(F32)
        cbg = chunked(conv_b[i][None, :D_FF]).astype(F32)
        cbv = chunked(conv_b[i][None, D_FF:]).astype(F32)
        wd = w_down[i].reshape(nj, FFN_FC, D_MODEL).astype(BF16)
        z = _ffn(z2d.reshape(bsz, tp, D_MODEL), row(ffn_norm[i]), wug, wuv, cwg, cwv, cbg, cbv, wd,
                 row(final_norm), tt, final=(i == depth - 1))
    return z[:, N_META:t_real]
```

```python
import functools

import jax
import jax.numpy as jnp
from jax import lax
from jax.experimental import pallas as pl
from jax.experimental.pallas import tpu as pltpu

F32 = jnp.float32
BF16 = jnp.bfloat16

D_MODEL = 1024
N_META = 16
F_TINY = 1e-30
NORM_EPS = 1e-6

GLA_HEADS, GLA_DK, GLA_DV = 4, 64, 128
GLA_K, GLA_V = GLA_HEADS * GLA_DK, GLA_HEADS * GLA_DV
GLA_GATE_RANK = 16
GLA_GATE_NORM = 16.0

RW_HEADS, RW_HD = 8, 64
RW_DIM = RW_HEADS * RW_HD
RW_W_RANK, RW_A_RANK, RW_V_RANK, RW_G_RANK = 64, 64, 32, 128
RW_GN_EPS = 64e-5
RW_GROUP_HEADS = 4
RW_GROUP = RW_GROUP_HEADS * RW_HD

HG_HEADS, HG_DK, HG_DV = 4, 128, 128
HG_K, HG_V = HG_HEADS * HG_DK, HG_HEADS * HG_DV

D_FF = 2816
CONV_W = 3

_REF_LAYOUT = (
    ("gla_q", GLA_K), ("gla_k", GLA_K), ("gla_v", GLA_V), ("gla_gk", GLA_GATE_RANK), ("gla_g", GLA_V),
    ("hg_q", HG_K), ("hg_f", HG_K), ("hg_i", HG_V), ("hg_g", HG_V),
    ("gate_gla", D_MODEL), ("gate_rw", D_MODEL), ("gate_hg", D_MODEL),
    ("rw_r", RW_DIM), ("rw_w", RW_W_RANK), ("rw_k", RW_DIM), ("rw_v", RW_DIM), ("rw_a", RW_A_RANK),
    ("rw_g", RW_G_RANK),
)
_REF_OFF = {}
_o = 0
for _n, _w in _REF_LAYOUT:
    _REF_OFF[_n] = (_o, _w)
    _o += _w
W_IN = _o

LANE = 128
SUBLANE = 8
_K_LAYOUT = (
    ("gate_gla", 1024), ("gate_rw", 1024), ("gate_hg", 1024),
    ("gla_v", 512), ("gla_g", 512), ("hg_q", 512), ("hg_f", 512), ("hg_i", 512), ("hg_g", 512),
    ("rw_r", 512), ("rw_k", 512), ("rw_v", 512),
    ("gla_q", 256), ("gla_k", 256),
    ("rw_g", 128), ("rw_wa", 128), ("misc", 128),
)
_K_OFF = {}
_o = 0
for _n, _w in _K_LAYOUT:
    assert _o % _w == 0
    _K_OFF[_n] = (_o, _w)
    _o += _w
IN_TN = 512
WP = -(-_o // IN_TN) * IN_TN
MISC_GK = 0
MISC_VR = GLA_GATE_RANK

CHUNK = 64
SUB = 32
VMEM_LIMIT = 56 * 1024 * 1024


def _mm(a, b):
    return jnp.dot(a.astype(BF16), b.astype(BF16), preferred_element_type=F32)


def _mm_nt(a, b):
    return lax.dot_general(a.astype(BF16), b.astype(BF16), (((1,), (1,)), ((), ())),
                           preferred_element_type=F32)


def _mm_tn(a, b):
    return lax.dot_general(a.astype(BF16), b.astype(BF16), (((0,), (0,)), ((), ())),
                           preferred_element_type=F32)


def _split2(x):
    hi = x.astype(BF16)
    lo = (x - hi.astype(F32)).astype(BF16)
    return hi, lo


def _mm_exact_rhs(a_f32, b_bf16):
    m = a_f32.shape[0]
    out = jnp.dot(jnp.concatenate(_split2(a_f32), axis=0), b_bf16, preferred_element_type=F32)
    return out[:m] + out[m:]


def _cumsum_rows(tril_bf16, g):
    w = g.shape[1]
    out = jnp.dot(tril_bf16, jnp.concatenate(_split2(g), axis=1), preferred_element_type=F32)
    return out[:, :w] + out[:, w:]


def _sigmoid(x):
    return 1.0 / (1.0 + jnp.exp(-x))


def _softplus(x):
    return jnp.maximum(x, 0.0) + jnp.log(1.0 + jnp.exp(-jnp.abs(x)))


def _rms(x, w):
    return x * lax.rsqrt(jnp.mean(x * x, axis=-1, keepdims=True) + NORM_EPS) * w


def _stack_heads(x, heads, width):
    lane = lax.broadcasted_iota(jnp.int32, x.shape, 1)
    return jnp.concatenate(
        [jnp.where((lane >= h * width) & (lane < (h + 1) * width), x, 0.0) for h in range(heads)], axis=0)


def _inproj_kernel(z_ref, nw_ref, w_ref, o_ref, h_ref):
    @pl.when(pl.program_id(1) == 0)
    def _():
        h_ref[...] = _rms(z_ref[...], nw_ref[...]).astype(BF16)

    o_ref[...] = jnp.dot(h_ref[...], w_ref[...], preferred_element_type=F32)


def _inproj(z2d, norm_w, w_bf16, tm):
    n = z2d.shape[0]
    return pl.pallas_call(
        _inproj_kernel,
        grid=(n // tm, WP // IN_TN),
        in_specs=[
            pl.BlockSpec((tm, D_MODEL), lambda i, j: (i, 0)),
            pl.BlockSpec((1, D_MODEL), lambda i, j: (0, 0)),
            pl.BlockSpec((D_MODEL, IN_TN), lambda i, j: (0, j)),
        ],
        out_specs=pl.BlockSpec((tm, IN_TN), lambda i, j: (i, j)),
        out_shape=jax.ShapeDtypeStruct((n, WP), F32),
        scratch_shapes=[pltpu.VMEM((tm, D_MODEL), BF16)],
        compiler_params=pltpu.CompilerParams(
            dimension_semantics=("arbitrary", "arbitrary"), vmem_limit_bytes=VMEM_LIMIT),
        name="inproj",
    )(z2d, norm_w, w_bf16)


def _gla_tile(q, k, v, g, tril_bd, s_ref, norm_w, heads, dk, first):
    tt, hk = q.shape
    assert tt % CHUNK == 0 and CHUNK == 2 * SUB
    dv = v.shape[1] // heads

    @pl.when(first)
    def _():
        s_ref[...] = jnp.zeros_like(s_ref)

    c_all = _cumsum_rows(tril_bd, g)
    blk1 = lax.broadcasted_iota(jnp.int32, (CHUNK, hk), 0) >= SUB
    st = functools.partial(_stack_heads, heads=heads, width=dk)
    n_st = heads * CHUNK
    causal = (lax.broadcasted_iota(jnp.int32, (n_st, n_st), 1)
              <= lax.broadcasted_iota(jnp.int32, (n_st, n_st), 0))
    chunks = []
    for j in range(tt // CHUNK):
        rows = slice(j * CHUNK, (j + 1) * CHUNK)
        c, qj, kj, vj = c_all[rows], q[rows], k[rows], v[rows]
        r0 = c[SUB // 2 - 1:SUB // 2, :]
        r1 = c[SUB + SUB // 2 - 1:SUB + SUB // 2, :]
        c_last = c[CHUNK - 1:CHUNK, :]
        kt = kj * jnp.exp(jnp.where(blk1, r1, r0) - c)
        qa = qj * jnp.exp(c - r0)
        qb = jnp.where(blk1, qj * jnp.exp(jnp.where(blk1, c - r1, 0.0)), 0.0)
        kt0 = jnp.where(blk1, 0.0, kt)
        kt1 = jnp.where(blk1, kt, 0.0)
        qs = jnp.concatenate([st(qa), st(qb)], axis=1).astype(BF16)
        ks = jnp.concatenate([st(kt0), st(kt1)], axis=1).astype(BF16)
        vs = jnp.concatenate([vj[:, h * dv:(h + 1) * dv] for h in range(heads)], axis=0).astype(BF16)
        chunks.append(dict(qs=qs, ks=ks, vs=vs, qe=st(qj * jnp.exp(c)).astype(BF16),
                           kd=st(kj * jnp.exp(c_last - c)).astype(BF16), dec=jnp.exp(c_last)))
    for ch in chunks:
        ch["a"] = jnp.where(causal, _mm_nt(ch["qs"], ch["ks"]), 0.0)
    for ch in chunks:
        ch["oa"] = _mm(ch["a"], ch["vs"])
        ch["kv"] = _mm_tn(ch["vs"], ch["kd"])
    s_t = s_ref[...]
    outs = []
    for ch in chunks:
        o = ch["oa"] + _mm_nt(ch["qe"], s_t)
        s_t = s_t * ch["dec"] + ch["kv"]
        o = o * lax.rsqrt(jnp.mean(o * o, axis=-1, keepdims=True) + NORM_EPS) * norm_w
        outs.append(jnp.concatenate([o[h * CHUNK:(h + 1) * CHUNK] for h in range(heads)], axis=1))
    s_ref[...] = s_t
    return jnp.concatenate(outs, axis=0)


def _silu(x):
    return x * _sigmoid(x)


def _gla_kernel(q_ref, k_ref, v_ref, misc_ref, og_ref, gkup_ref, gkb_ref, nw_ref, tril_ref, y_ref, s_ref):
    q = q_ref[...] * (GLA_DK ** -0.5)
    x = _mm(misc_ref[...], gkup_ref[...]) + gkb_ref[...]
    g = -_softplus(-x) * (1.0 / GLA_GATE_NORM)
    o = _gla_tile(q, k_ref[...], v_ref[...], g, tril_ref[...], s_ref, nw_ref[...], GLA_HEADS, GLA_DK,
                  pl.program_id(1) == 0)
    y_ref[...] = (o * _silu(og_ref[...])).astype(y_ref.dtype)


def _hgrn_kernel(q_ref, f_ref, i_ref, og_ref, lb_ref, nw_ref, tril_ref, y_ref, s_ref):
    z = f_ref[...]
    lb = lb_ref[...]
    forget = lb + (1.0 - lb) * _sigmoid(z)
    g = jnp.log(jnp.maximum(forget, F_TINY))
    k = (1.0 - lb) * _sigmoid(-z)
    o = _gla_tile(q_ref[...], k, i_ref[...], g, tril_ref[...], s_ref, nw_ref[...], HG_HEADS, HG_DK,
                  pl.program_id(1) == 0)
    y_ref[...] = (o * _silu(og_ref[...])).astype(y_ref.dtype)


def _pcol(name, tp, tt):
    off, width = _K_OFF[name]
    nt = tp // tt
    cb = off // width
    return pl.BlockSpec((tt, width), lambda b, t: (b * nt + t, cb))


def _const_spec(shape):
    nd = len(shape)
    return pl.BlockSpec(shape, lambda b, t: (0,) * nd)


def _seq_params():
    return pltpu.CompilerParams(dimension_semantics=("arbitrary", "arbitrary"), vmem_limit_bytes=VMEM_LIMIT)


def _gla_mixer(p, gk_up_pad, gk_bias, norm_w, tril, bsz, tp, tt):
    n = p.shape[0]
    col = functools.partial(_pcol, tp=tp, tt=tt)
    return pl.pallas_call(
        _gla_kernel,
        grid=(bsz, tp // tt),
        in_specs=[col("gla_q"), col("gla_k"), col("gla_v"), col("misc"), col("gla_g"),
                  _const_spec(gk_up_pad.shape), _const_spec(gk_bias.shape),
                  _const_spec(norm_w.shape), _const_spec(tril.shape)],
        out_specs=pl.BlockSpec((tt, GLA_V), lambda b, t: (b * (tp // tt) + t, 0)),
        out_shape=jax.ShapeDtypeStruct((n, GLA_V), BF16),
        scratch_shapes=[pltpu.VMEM((GLA_DV, GLA_K), F32)],
        compiler_params=_seq_params(),
        name="gla_mixer",
    )(p, p, p, p, p, gk_up_pad, gk_bias, norm_w, tril)


def _hgrn_mixer(p, lb, norm_w, tril, bsz, tp, tt):
    n = p.shape[0]
    col = functools.partial(_pcol, tp=tp, tt=tt)
    return pl.pallas_call(
        _hgrn_kernel,
        grid=(bsz, tp // tt),
        in_specs=[col("hg_q"), col("hg_f"), col("hg_i"), col("hg_g"),
                  _const_spec(lb.shape), _const_spec(norm_w.shape), _const_spec(tril.shape)],
        out_specs=pl.BlockSpec((tt, HG_V), lambda b, t: (b * (tp // tt) + t, 0)),
        out_shape=jax.ShapeDtypeStruct((n, HG_V), BF16),
        scratch_shapes=[pltpu.VMEM((HG_DV, HG_K), F32)],
        compiler_params=_seq_params(),
        name="hgrn_mixer",
    )(p, p, p, p, lb, norm_w, tril)


def _unit_lower_inverses(n_list, blk16, blk32):
    size = n_list[0].shape[0]
    ri = lax.broadcasted_iota(jnp.int32, (size, size), 0)
    ci = lax.broadcasted_iota(jnp.int32, (size, size), 1)
    eye = jnp.where(ri == ci, 1.0, 0.0)
    n16 = [jnp.where(blk16, n, 0.0) for n in n_list]
    n32 = [jnp.where(blk32 & jnp.logical_not(blk16), n, 0.0).astype(BF16) for n in n_list]
    n64 = [jnp.where(blk32, 0.0, n).astype(BF16) for n in n_list]
    t = [eye + n for n in n16]
    m = [n.astype(BF16) for n in n16]
    for it in range(3):
        m = [_mm(x, x).astype(BF16) for x in m]
        t = [ti + _mm(mi, ti) for mi, ti in zip(m, t)]
    for off_diag in (n32, n64):
        tb = [ti.astype(BF16) for ti in t]
        w = [_mm(ti, ni).astype(BF16) for ti, ni in zip(tb, off_diag)]
        t = [ti + _mm(wi, tbi) for ti, wi, tbi in zip(t, w, tb)]
    return t


def _rwkv_kernel(has_vres, *refs):
    if has_vres:
        (r_ref, k_ref, v_ref, g_ref, wa_ref, misc_ref, vf_ref,
         mu_r_ref, mu_k_ref, mu_v_ref, mu_g_ref, mu_wa_ref, mu_misc_ref,
         w0_ref, w2_ref, a0_ref, a2_ref, g2_ref, v0_ref, v2_ref,
         kk_ref, ka_ref, rk_ref, lnw_ref, lnb_ref, seg_ref, tril_ref,
         y_ref, s_ref, carry_ref) = refs
    else:
        (r_ref, k_ref, v_ref, g_ref, wa_ref, misc_ref,
         mu_r_ref, mu_k_ref, mu_v_ref, mu_g_ref, mu_wa_ref, mu_misc_ref,
         w0_ref, w2_ref, a0_ref, a2_ref, g2_ref,
         kk_ref, ka_ref, rk_ref, lnw_ref, lnb_ref, seg_ref, tril_ref,
         y_ref, vf_out_ref, s_ref, carry_ref) = refs

    first = pl.program_id(1) == 0
    tt = r_ref.shape[0]

    @pl.when(first)
    def _():
        s_ref[...] = jnp.zeros_like(s_ref)
        carry_ref[...] = jnp.zeros_like(carry_ref)

    srcs = (r_ref, k_ref, v_ref, g_ref, wa_ref, misc_ref)
    mus = (mu_r_ref, mu_k_ref, mu_v_ref, mu_g_ref, mu_wa_ref, mu_misc_ref)
    shifted = []
    off = 0
    for src, mu in zip(srcs, mus):
        p = src[...]
        width = p.shape[1]
        prev_row = carry_ref[0:1, off:off + width]
        rolled = pltpu.roll(p, 1, 0)
        row = lax.broadcasted_iota(jnp.int32, p.shape, 0)
        prev = jnp.where(row == 0, prev_row, rolled)
        shifted.append(p + (prev - p) * mu[...])
        carry_ref[0:1, off:off + width] = p[tt - 1:tt, :]
        off += width
    r, k, v, s_g, s_wa, s_misc = shifted

    seg = seg_ref[...]
    w_log = -_softplus(-(w0_ref[...] + _mm(jnp.tanh(s_wa), w2_ref[...]))) - 0.5
    lw = -jnp.exp(w_log)
    a = _sigmoid(a0_ref[...] + _mm(s_wa, a2_ref[...]))
    g = _mm(_sigmoid(s_g), g2_ref[...])
    if has_vres:
        v = v + (vf_ref[...] - v) * _sigmoid(v0_ref[...] + _mm(s_misc, v2_ref[...]))
    else:
        vf_out_ref[...] = v
    kk = k * kk_ref[...]
    k = k * (1.0 + (a - 1.0) * ka_ref[...])
    sums = _mm_exact_rhs(jnp.concatenate([kk * kk, r * k * rk_ref[...]], axis=0), seg)
    kk = kk / jnp.maximum(jnp.sqrt(sums[:tt]), 1e-12)
    bonus = sums[tt:] * v
    alpha = -kk
    beta = kk * a
    c_all = _cumsum_rows(tril_ref[...], lw)
    ce_all = c_all - lw

    gs = RW_GROUP
    n_groups = RW_HEADS // RW_GROUP_HEADS
    n_st = RW_GROUP_HEADS * CHUNK
    ri = lax.broadcasted_iota(jnp.int32, (n_st, n_st), 0)
    ci = lax.broadcasted_iota(jnp.int32, (n_st, n_st), 1)
    strict = ci < ri
    blk16 = (ri >> 4) == (ci >> 4)
    blk32 = (ri >> 5) == (ci >> 5)
    tcol = lax.broadcasted_iota(jnp.int32, (CHUNK, n_st), 1) & (CHUNK - 1)
    trow = lax.broadcasted_iota(jnp.int32, (CHUNK, n_st), 0)
    incl = tcol <= trow
    st = functools.partial(_stack_heads, heads=RW_GROUP_HEADS, width=RW_HD)

    blocks = []
    for j in range(tt // CHUNK):
        rows = slice(j * CHUNK, (j + 1) * CHUNK)
        c, ce = c_all[rows], ce_all[rows]
        rj, kj, vj, al, be_ = r[rows], k[rows], v[rows], alpha[rows], beta[rows]
        mid = c[CHUNK // 2 - 1:CHUNK // 2, :]
        c_last = c[CHUNK - 1:CHUNK, :]
        e_out = jnp.exp(mid - c)
        e_end = jnp.exp(c_last - c)
        at = al * jnp.exp(ce - mid)
        rt = rj * jnp.exp(c - mid)
        bh = be_ * e_out
        kh = kj * e_out
        ae = al * jnp.exp(ce)
        rs = rj * jnp.exp(c)
        be = be_ * e_end
        ke = kj * e_end
        dec = jnp.exp(c_last)
        for gi in range(n_groups):
            sl = slice(gi * gs, (gi + 1) * gs)
            stb = lambda a_: st(a_[:, sl]).astype(BF16)
            blocks.append(dict(j=j, gi=gi, bm=stb(bh), km=stb(kh), am=stb(at), v_bd=stb(vj), aem=stb(ae),
                               bem=stb(be), kem=stb(ke), rt=rt[:, sl].astype(BF16),
                               rs=rs[:, sl].astype(BF16), dec=dec[:, sl]))
    for b in blocks:
        b["n_ab"] = jnp.where(strict, _mm_nt(b["am"], b["bm"]), 0.0)
    for b in blocks:
        b["n_ak"] = jnp.where(strict, _mm_nt(b["am"], b["km"]), 0.0).astype(BF16)
    for b in blocks:
        b["a_rb"] = jnp.where(incl, _mm_nt(b["rt"], b["bm"]), 0.0).astype(BF16)
        b["a_rk"] = jnp.where(incl, _mm_nt(b["rt"], b["km"]), 0.0).astype(BF16)
    for b, t_inv in zip(blocks, _unit_lower_inverses([b["n_ab"] for b in blocks], blk16, blk32)):
        b["t_inv"] = t_inv.astype(BF16)
    for b in blocks:
        b["w1"] = _mm(b["n_ak"], b["v_bd"])
        b["y0"] = _mm(b["a_rk"], b["v_bd"])
        b["vk"] = _mm_tn(b["v_bd"], b["kem"])

    state = [s_ref[gi] for gi in range(n_groups)]
    y_rows = []
    for j in range(tt // CHUNK):
        cur = [b for b in blocks if b["j"] == j]
        sb = [state[b["gi"]].astype(BF16) for b in cur]
        xs = [_mm_nt(b["aem"], s) + b["w1"] for b, s in zip(cur, sb)]
        us = [_mm(b["t_inv"], x).astype(BF16) for b, x in zip(cur, xs)]
        for b, u in zip(cur, us):
            state[b["gi"]] = state[b["gi"]] * b["dec"] + _mm_tn(u, b["bem"]) + b["vk"]
        ys = [_mm_nt(b["rs"], s) + _mm(b["a_rb"], u) + b["y0"] for b, s, u in zip(cur, sb, us)]
        y_rows.append(jnp.concatenate(ys, axis=1))
    for gi in range(n_groups):
        s_ref[gi] = state[gi]
    y = jnp.concatenate(y_rows, axis=0)

    inv_hd = 1.0 / RW_HD
    mean = _mm_exact_rhs(y, seg) * inv_hd
    yc = y - mean
    var = _mm_exact_rhs(yc * yc, seg) * inv_hd
    y = yc * lax.rsqrt(var + RW_GN_EPS) * lnw_ref[...] + lnb_ref[...]
    y_ref[...] = ((y + bonus) * g).astype(y_ref.dtype)


def _rwkv_mixer(p, v_first, mus, prm, seg, tril, bsz, tp, tt):
    n = p.shape[0]
    nt = tp // tt
    has_vres = v_first is not None
    row_spec = pl.BlockSpec((tt, RW_DIM), lambda b, t: (b * nt + t, 0))
    col = functools.partial(_pcol, tp=tp, tt=tt)
    in_specs = [col("rw_r"), col("rw_k"), col("rw_v"), col("rw_g"), col("rw_wa"), col("misc")]
    args = [p, p, p, p, p, p]
    if has_vres:
        in_specs.append(row_spec)
        args.append(v_first)
    consts = list(mus) + [prm["w0"], prm["w2"], prm["a0"], prm["a2"], prm["g2"]]
    if has_vres:
        consts += [prm["v0"], prm["v2"]]
    consts += [prm["kk"], prm["ka"], prm["rk"], prm["lnw"], prm["lnb"], seg, tril]
    in_specs += [_const_spec(c.shape) for c in consts]
    args += consts
    out_shape = [jax.ShapeDtypeStruct((n, RW_DIM), BF16)]
    out_specs = [row_spec]
    if not has_vres:
        out_shape.append(jax.ShapeDtypeStruct((n, RW_DIM), F32))
        out_specs.append(row_spec)
    carry_w = 3 * RW_DIM + 3 * LANE
    res = pl.pallas_call(
        functools.partial(_rwkv_kernel, has_vres),
        grid=(bsz, nt),
        in_specs=in_specs,
        out_specs=out_specs,
        out_shape=out_shape,
        scratch_shapes=[pltpu.VMEM((RW_HEADS // RW_GROUP_HEADS, RW_GROUP, RW_GROUP), F32),
                        pltpu.VMEM((SUBLANE, carry_w), F32)],
        compiler_params=_seq_params(),
        name="rwkv_mixer",
    )(*args)
    return (res[0], v_first) if has_vres else (res[0], res[1])


def _merge_kernel(z_ref, gg_ref, gr_ref, gh_ref, yg_ref, yr_ref, yh_ref, wa_ref, wb_ref, wc_ref, wo_ref, o_ref):
    d = lambda y, w: jnp.dot(y[...], w[...], preferred_element_type=F32)
    m = (_sigmoid(gg_ref[...]) * d(yg_ref, wa_ref) + _sigmoid(gr_ref[...]) * d(yr_ref, wb_ref)
         + _sigmoid(gh_ref[...]) * d(yh_ref, wc_ref))
    o_ref[...] = z_ref[...] + _mm(m, wo_ref[...])


def _merge(z2d, p, y_gla, y_rw, y_hg, wa, wb, wc, wo, tm):
    n = z2d.shape[0]
    row = lambda w: pl.BlockSpec((tm, w), lambda i: (i, 0))
    gate = lambda name: pl.BlockSpec((tm, D_MODEL), lambda i, cb=_K_OFF[name][0] // D_MODEL: (i, cb))
    full = lambda a: pl.BlockSpec(a.shape, lambda i: (0, 0))
    return pl.pallas_call(
        _merge_kernel,
        grid=(n // tm,),
        in_specs=[row(D_MODEL), gate("gate_gla"), gate("gate_rw"), gate("gate_hg"),
                  row(GLA_V), row(RW_DIM), row(HG_V), full(wa), full(wb), full(wc), full(wo)],
        out_specs=row(D_MODEL),
        out_shape=jax.ShapeDtypeStruct((n, D_MODEL), F32),
        compiler_params=pltpu.CompilerParams(dimension_semantics=("arbitrary",), vmem_limit_bytes=VMEM_LIMIT),
        name="merge",
    )(z2d, p, p, p, y_gla, y_rw, y_hg, wa, wb, wc, wo)


FFN_FC = 256
FFN_HALO = SUBLANE


def _ffn_kernel(final, z_ref, halo_ref, nw_ref, wug_ref, wuv_ref, cwg_ref, cwv_ref, cbg_ref, cbv_ref, wd_ref,
                fnw_ref, o_ref, acc_ref):
    tt = z_ref.shape[0]
    zt = z_ref[...]
    nw = nw_ref[...]
    h = jnp.concatenate([_rms(halo_ref[...], nw), _rms(zt, nw)], axis=0).astype(BF16)
    keep = (lax.broadcasted_iota(jnp.int32, (tt + FFN_HALO, 1), 0) >= FFN_HALO) | (pl.program_id(1) > 0)
    acc_ref[...] = zt

    def conv(u, cw, cb):
        u = jnp.where(keep, u, 0.0)
        u1 = pltpu.roll(u, 1, 0)
        u2 = pltpu.roll(u, 2, 0)
        c = cb + u2 * cw[0:1, :] + u1 * cw[1:2, :] + u * cw[2:3, :]
        return c[FFN_HALO:, :]

    def body(j, carry):
        cg = conv(jnp.dot(h, wug_ref[j], preferred_element_type=F32), cwg_ref[j], cbg_ref[j])
        cv = conv(jnp.dot(h, wuv_ref[j], preferred_element_type=F32), cwv_ref[j], cbv_ref[j])
        act = (cg * _sigmoid(cg) * cv).astype(BF16)
        acc_ref[...] += jnp.dot(act, wd_ref[j], preferred_element_type=F32)
        return carry

    lax.fori_loop(0, wug_ref.shape[0], body, 0)
    out = acc_ref[...]
    if final:
        out = _rms(out, fnw_ref[...])
    o_ref[...] = out


def _ffn(z3d, norm_w, wug, wuv, cwg, cwv, cbg, cbv, wd, final_w, tt, final):
    bsz, tp, _ = z3d.shape
    hb = tt // FFN_HALO
    once = lambda a: pl.BlockSpec(a.shape, lambda b, t: (0,) * a.ndim, pipeline_mode=pl.Buffered(1))
    return pl.pallas_call(
        functools.partial(_ffn_kernel, final),
        grid=(bsz, tp // tt),
        in_specs=[pl.BlockSpec((None, tt, D_MODEL), lambda b, t: (b, t, 0)),
                  pl.BlockSpec((None, FFN_HALO, D_MODEL), lambda b, t: (b, jnp.maximum(t * hb - 1, 0), 0)),
                  once(norm_w), once(wug), once(wuv), once(cwg), once(cwv), once(cbg), once(cbv), once(wd),
                  once(final_w)],
        out_specs=pl.BlockSpec((None, tt, D_MODEL), lambda b, t: (b, t, 0)),
        out_shape=jax.ShapeDtypeStruct((bsz, tp, D_MODEL), F32),
        scratch_shapes=[pltpu.VMEM((tt, D_MODEL), F32)],
        compiler_params=_seq_params(),
        name="conv_ffn",
    )(z3d, z3d, norm_w, wug, wuv, cwg, cwv, cbg, cbv, wd, final_w)


def _largest_tile(n, cands):
    for c in cands:
        if n % c == 0:
            return c
    raise ValueError(f"no tile for {n}")


def _ref_cols(w, name):
    off, width = _REF_OFF[name]
    return w[..., off:off + width]


def _pack_columns(w, vres):
    lead = w.shape[:-1]
    zeros = lambda n: jnp.zeros(lead + (n,), w.dtype)
    vr = vres if vres is not None else zeros(RW_V_RANK)
    parts = []
    for name, width in _K_LAYOUT:
        if name == "rw_wa":
            parts.append(jnp.concatenate([_ref_cols(w, "rw_w"), _ref_cols(w, "rw_a")], axis=-1))
        elif name == "misc":
            parts.append(jnp.concatenate(
                [_ref_cols(w, "gla_gk"), vr, zeros(width - GLA_GATE_RANK - RW_V_RANK)], axis=-1))
        else:
            parts.append(_ref_cols(w, name))
    used = sum(wd for _, wd in _K_LAYOUT)
    parts.append(zeros(WP - used))
    return jnp.concatenate(parts, axis=-1)


def _pad_rows(w, lo, total):
    return jnp.pad(w, ((lo, total - lo - w.shape[0]), (0, 0)))


def kernel(x, meta, mix_norm, w_in, w_in_vres, rw_mu, rw_mu_vres, gla_gk_up, gla_gk_bias, gla_norm, rw_w0,
           rw_w2, rw_a0, rw_a2, rw_v0, rw_v2, rw_g2, rw_kk, rw_ka, rw_rk, rw_ln_w, rw_ln_b, hg_lb_logits,
           hg_norm, w_out_gla, w_out_rw, w_out_hg, w_out, ffn_norm, w_up, conv_w, conv_b, w_down, final_norm):
    bsz, seq, _ = x.shape
    depth = w_in.shape[0]
    t_real = seq + N_META
    tp = -(-t_real // CHUNK) * CHUNK
    n = bsz * tp
    tm = _largest_tile(n, (1024, 512, 256, 128, 64))
    tm_merge = _largest_tile(n, (512, 256, 128, 64))
    tt = _largest_tile(tp, (704, 512, 384, 256, 192, 128, 64))
    tt_mix = _largest_tile(tp, (3 * CHUNK, 2 * CHUNK, CHUNK))

    z = jnp.concatenate([jnp.broadcast_to(meta.astype(x.dtype)[None], (bsz, N_META, D_MODEL)), x,
                         jnp.zeros((bsz, tp - t_real, D_MODEL), x.dtype)], axis=1)

    row = lambda a: a.reshape(1, -1).astype(F32)
    t_idx = jnp.arange(tt_mix)
    tril = ((t_idx[:, None] // CHUNK == t_idx[None, :] // CHUNK)
            & (t_idx[None, :] <= t_idx[:, None])).astype(BF16)
    head_of = jnp.arange(RW_DIM) // RW_HD
    seg = (head_of[:, None] == head_of[None, :]).astype(BF16)
    lb_p = jax.nn.softmax(hg_lb_logits.astype(F32), axis=0)
    lb_all = jnp.cumsum(lb_p, axis=0) - lb_p[0:1]

    nj = D_FF // FFN_FC
    v_first = None
    for i in range(depth):
        vres_w = w_in_vres[i - 1] if i > 0 else None
        vres_mu = rw_mu_vres[i - 1] if i > 0 else None
        w_cat = _pack_columns(w_in[i], vres_w).astype(BF16)
        mu_full = jnp.concatenate([jnp.zeros((W_IN - rw_mu.shape[1],), F32), rw_mu[i].astype(F32)])
        mu_cat = _pack_columns(mu_full, vres_mu)
        mus = [row(mu_cat[_K_OFF[nm][0]:_K_OFF[nm][0] + _K_OFF[nm][1]])
               for nm in ("rw_r", "rw_k", "rw_v", "rw_g", "rw_wa", "misc")]

        z2d = z.reshape(n, D_MODEL)
        p = _inproj(z2d, row(mix_norm[i]), w_cat, tm)

        gk_up_pad = _pad_rows(gla_gk_up[i], MISC_GK, LANE).astype(BF16)
        y_gla = _gla_mixer(p, gk_up_pad, row(gla_gk_bias[i]), row(gla_norm[i]), tril, bsz, tp, tt_mix)
        y_hg = _hgrn_mixer(p, row(lb_all[i]), row(hg_norm[i]), tril, bsz, tp, tt_mix)

        prm = {
            "w0": row(rw_w0[i]), "w2": _pad_rows(rw_w2[i], 0, LANE).astype(BF16),
            "a0": row(rw_a0[i]), "a2": _pad_rows(rw_a2[i], RW_W_RANK, LANE).astype(BF16),
            "g2": rw_g2[i].astype(BF16),
            "kk": row(rw_kk[i]), "ka": row(rw_ka[i]), "rk": row(rw_rk[i]),
            "lnw": row(rw_ln_w[i]), "lnb": row(rw_ln_b[i]),
        }
        if i > 0:
            prm["v0"] = row(rw_v0[i - 1])
            prm["v2"] = _pad_rows(rw_v2[i - 1], MISC_VR, LANE).astype(BF16)
        y_rw, v_first = _rwkv_mixer(p, v_first, mus, prm, seg, tril, bsz, tp, tt_mix)

        z2d = _merge(z2d, p, y_gla, y_rw, y_hg, w_out_gla[i].astype(BF16), w_out_rw[i].astype(BF16),
                     w_out_hg[i].astype(BF16), w_out[i].astype(BF16), tm_merge)

        chunked = lambda a: a.reshape(a.shape[0], nj, FFN_FC).transpose(1, 0, 2)
        wug = chunked(w_up[i][:, :D_FF]).astype(BF16)
        wuv = chunked(w_up[i][:, D_FF:]).astype(BF16)
        cwg = jnp.pad(chunked(conv_w[i][:, :D_FF]), ((0, 0), (0, SUBLANE - CONV_W), (0, 0))).astype(F32)
        cwv = jnp.pad(chunked(conv_w[i][:, D_FF:]), ((0, 0), (0, SUBLANE - CONV_W), (0, 0))).astype(F32)
        cbg = chunked(conv_b[i][None, :D_FF]).astype(F32)
        cbv = chunked(conv_b[i][None, D_FF:]).astype(F32)
        wd = w_down[i].reshape(nj, FFN_FC, D_MODEL).astype(BF16)
        z = _ffn(z2d.reshape(bsz, tp, D_MODEL), row(ffn_norm[i]), wug, wuv, cwg, cwv, cbg, cbv, wd,
                 row(final_norm), tt, final=(i == depth - 1))
    return z[:, N_META:t_real]
```

```python
import functools

import jax
import jax.numpy as jnp
from jax import lax
from jax.experimental import pallas as pl
from jax.experimental.pallas import tpu as pltpu

F32 = jnp.float32
BF16 = jnp.bfloat16

D_MODEL = 1024
N_META = 16
F_TINY = 1e-30
NORM_EPS = 1e-6

GLA_HEADS, GLA_DK, GLA_DV = 4, 64, 128
GLA_K, GLA_V = GLA_HEADS * GLA_DK, GLA_HEADS * GLA_DV
GLA_GATE_RANK = 16
GLA_GATE_NORM = 16.0

RW_HEADS, RW_HD = 8, 64
RW_DIM = RW_HEADS * RW_HD
RW_W_RANK, RW_A_RANK, RW_V_RANK, RW_G_RANK = 64, 64, 32, 128
RW_GN_EPS = 64e-5
RW_GROUP_HEADS = 4
RW_GROUP = RW_GROUP_HEADS * RW_HD

HG_HEADS, HG_DK, HG_DV = 4, 128, 128
HG_K, HG_V = HG_HEADS * HG_DK, HG_HEADS * HG_DV

D_FF = 2816
CONV_W = 3

_REF_LAYOUT = (
    ("gla_q", GLA_K), ("gla_k", GLA_K), ("gla_v", GLA_V), ("gla_gk", GLA_GATE_RANK), ("gla_g", GLA_V),
    ("hg_q", HG_K), ("hg_f", HG_K), ("hg_i", HG_V), ("hg_g", HG_V),
    ("gate_gla", D_MODEL), ("gate_rw", D_MODEL), ("gate_hg", D_MODEL),
    ("rw_r", RW_DIM), ("rw_w", RW_W_RANK), ("rw_k", RW_DIM), ("rw_v", RW_DIM), ("rw_a", RW_A_RANK),
    ("rw_g", RW_G_RANK),
)
_REF_OFF = {}
_o = 0
for _n, _w in _REF_LAYOUT:
    _REF_OFF[_n] = (_o, _w)
    _o += _w
W_IN = _o

LANE = 128
SUBLANE = 8
_K_LAYOUT = (
    ("gate_gla", 1024), ("gate_rw", 1024), ("gate_hg", 1024),
    ("gla_v", 512), ("gla_g", 512), ("hg_q", 512), ("hg_f", 512), ("hg_i", 512), ("hg_g", 512),
    ("rw_r", 512), ("rw_k", 512), ("rw_v", 512),
    ("gla_q", 256), ("gla_k", 256),
    ("rw_g", 128), ("rw_wa", 128), ("misc", 128),
)
_K_OFF = {}
_o = 0
for _n, _w in _K_LAYOUT:
    assert _o % _w == 0
    _K_OFF[_n] = (_o, _w)
    _o += _w
IN_TN = 512
WP = -(-_o // IN_TN) * IN_TN
MISC_GK = 0
MISC_VR = GLA_GATE_RANK

CHUNK = 64
SUB = 32
VMEM_LIMIT = 56 * 1024 * 1024


def _mm(a, b):
    return jnp.dot(a.astype(BF16), b.astype(BF16), preferred_element_type=F32)


def _mm_nt(a, b):
    return lax.dot_general(a.astype(BF16), b.astype(BF16), (((1,), (1,)), ((), ())),
                           preferred_element_type=F32)


def _mm_tn(a, b):
    return lax.dot_general(a.astype(BF16), b.astype(BF16), (((0,), (0,)), ((), ())),
                           preferred_element_type=F32)


def _split2(x):
    hi = x.astype(BF16)
    lo = (x - hi.astype(F32)).astype(BF16)
    return hi, lo


def _mm_exact_rhs(a_f32, b_bf16):
    m = a_f32.shape[0]
    out = jnp.dot(jnp.concatenate(_split2(a_f32), axis=0), b_bf16, preferred_element_type=F32)
    return out[:m] + out[m:]


def _cumsum_rows(tril_bf16, g):
    w = g.shape[1]
    out = jnp.dot(tril_bf16, jnp.concatenate(_split2(g), axis=1), preferred_element_type=F32)
    return out[:, :w] + out[:, w:]


def _sigmoid(x):
    return 1.0 / (1.0 + jnp.exp(-x))


def _softplus(x):
    return jnp.maximum(x, 0.0) + jnp.log(1.0 + jnp.exp(-jnp.abs(x)))


def _rms(x, w):
    return x * lax.rsqrt(jnp.mean(x * x, axis=-1, keepdims=True) + NORM_EPS) * w


def _stack_heads(x, heads, width):
    lane = lax.broadcasted_iota(jnp.int32, x.shape, 1)
    return jnp.concatenate(
        [jnp.where((lane >= h * width) & (lane < (h + 1) * width), x, 0.0) for h in range(heads)], axis=0)


def _inproj_kernel(z_ref, nw_ref, w_ref, o_ref, h_ref):
    @pl.when(pl.program_id(1) == 0)
    def _():
        h_ref[...] = _rms(z_ref[...], nw_ref[...]).astype(BF16)

    o_ref[...] = jnp.dot(h_ref[...], w_ref[...], preferred_element_type=F32).astype(o_ref.dtype)


def _inproj(z2d, norm_w, w_bf16, tm):
    n = z2d.shape[0]
    return pl.pallas_call(
        _inproj_kernel,
        grid=(n // tm, WP // IN_TN),
        in_specs=[
            pl.BlockSpec((tm, D_MODEL), lambda i, j: (i, 0)),
            pl.BlockSpec((1, D_MODEL), lambda i, j: (0, 0)),
            pl.BlockSpec((D_MODEL, IN_TN), lambda i, j: (0, j)),
        ],
        out_specs=pl.BlockSpec((tm, IN_TN), lambda i, j: (i, j)),
        out_shape=jax.ShapeDtypeStruct((n, WP), BF16),
        scratch_shapes=[pltpu.VMEM((tm, D_MODEL), BF16)],
        compiler_params=pltpu.CompilerParams(
            dimension_semantics=("arbitrary", "arbitrary"), vmem_limit_bytes=VMEM_LIMIT),
        name="inproj",
    )(z2d, norm_w, w_bf16)


def _gla_tile(q, k, v, g, tril_bd, s_ref, norm_w, heads, dk, first):
    tt, hk = q.shape
    assert tt % CHUNK == 0 and CHUNK == 2 * SUB
    dv = v.shape[1] // heads

    @pl.when(first)
    def _():
        s_ref[...] = jnp.zeros_like(s_ref)

    c_all = _cumsum_rows(tril_bd, g)
    blk1 = lax.broadcasted_iota(jnp.int32, (CHUNK, hk), 0) >= SUB
    st = functools.partial(_stack_heads, heads=heads, width=dk)
    n_st = heads * CHUNK
    causal = (lax.broadcasted_iota(jnp.int32, (n_st, n_st), 1)
              <= lax.broadcasted_iota(jnp.int32, (n_st, n_st), 0))
    chunks = []
    for j in range(tt // CHUNK):
        rows = slice(j * CHUNK, (j + 1) * CHUNK)
        c, qj, kj, vj = c_all[rows], q[rows], k[rows], v[rows]
        r0 = c[SUB // 2 - 1:SUB // 2, :]
        r1 = c[SUB + SUB // 2 - 1:SUB + SUB // 2, :]
        c_last = c[CHUNK - 1:CHUNK, :]
        kt = kj * jnp.exp(jnp.where(blk1, r1, r0) - c)
        qa = qj * jnp.exp(c - r0)
        qb = jnp.where(blk1, qj * jnp.exp(jnp.where(blk1, c - r1, 0.0)), 0.0)
        kt0 = jnp.where(blk1, 0.0, kt)
        kt1 = jnp.where(blk1, kt, 0.0)
        qs = jnp.concatenate([st(qa), st(qb)], axis=1).astype(BF16)
        ks = jnp.concatenate([st(kt0), st(kt1)], axis=1).astype(BF16)
        vs = jnp.concatenate([vj[:, h * dv:(h + 1) * dv] for h in range(heads)], axis=0).astype(BF16)
        chunks.append(dict(qs=qs, ks=ks, vs=vs, qe=st(qj * jnp.exp(c)).astype(BF16),
                           kd=st(kj * jnp.exp(c_last - c)).astype(BF16), dec=jnp.exp(c_last)))
    for ch in chunks:
        ch["a"] = jnp.where(causal, _mm_nt(ch["qs"], ch["ks"]), 0.0)
    for ch in chunks:
        ch["oa"] = _mm(ch["a"], ch["vs"])
        ch["kv"] = _mm_tn(ch["vs"], ch["kd"])
    s_t = s_ref[...]
    outs = []
    for ch in chunks:
        o = ch["oa"] + _mm_nt(ch["qe"], s_t)
        s_t = s_t * ch["dec"] + ch["kv"]
        o = o * lax.rsqrt(jnp.mean(o * o, axis=-1, keepdims=True) + NORM_EPS) * norm_w
        outs.append(jnp.concatenate([o[h * CHUNK:(h + 1) * CHUNK] for h in range(heads)], axis=1))
    s_ref[...] = s_t
    return jnp.concatenate(outs, axis=0)


def _silu(x):
    return x * _sigmoid(x)


def _gla_kernel(q_ref, k_ref, v_ref, misc_ref, og_ref, gkup_ref, gkb_ref, nw_ref, tril_ref, y_ref, s_ref):
    q = q_ref[...].astype(F32) * (GLA_DK ** -0.5)
    x = _mm(misc_ref[...], gkup_ref[...]) + gkb_ref[...]
    g = -_softplus(-x) * (1.0 / GLA_GATE_NORM)
    o = _gla_tile(q, k_ref[...].astype(F32), v_ref[...].astype(F32), g, tril_ref[...], s_ref, nw_ref[...],
                  GLA_HEADS, GLA_DK, pl.program_id(1) == 0)
    y_ref[...] = (o * _silu(og_ref[...].astype(F32))).astype(y_ref.dtype)


def _hgrn_kernel(q_ref, f_ref, i_ref, og_ref, lb_ref, nw_ref, tril_ref, y_ref, s_ref):
    z = f_ref[...].astype(F32)
    lb = lb_ref[...]
    forget = lb + (1.0 - lb) * _sigmoid(z)
    g = jnp.log(jnp.maximum(forget, F_TINY))
    k = (1.0 - lb) * _sigmoid(-z)
    o = _gla_tile(q_ref[...].astype(F32), k, i_ref[...].astype(F32), g, tril_ref[...], s_ref, nw_ref[...],
                  HG_HEADS, HG_DK, pl.program_id(1) == 0)
    y_ref[...] = (o * _silu(og_ref[...].astype(F32))).astype(y_ref.dtype)


def _pcol(name, tp, tt):
    off, width = _K_OFF[name]
    nt = tp // tt
    cb = off // width
    return pl.BlockSpec((tt, width), lambda b, t: (b * nt + t, cb))


def _const_spec(shape):
    nd = len(shape)
    return pl.BlockSpec(shape, lambda b, t: (0,) * nd)


def _seq_params():
    return pltpu.CompilerParams(dimension_semantics=("arbitrary", "arbitrary"), vmem_limit_bytes=VMEM_LIMIT)


def _gla_mixer(p, gk_up_pad, gk_bias, norm_w, tril, bsz, tp, tt):
    n = p.shape[0]
    col = functools.partial(_pcol, tp=tp, tt=tt)
    return pl.pallas_call(
        _gla_kernel,
        grid=(bsz, tp // tt),
        in_specs=[col("gla_q"), col("gla_k"), col("gla_v"), col("misc"), col("gla_g"),
                  _const_spec(gk_up_pad.shape), _const_spec(gk_bias.shape),
                  _const_spec(norm_w.shape), _const_spec(tril.shape)],
        out_specs=pl.BlockSpec((tt, GLA_V), lambda b, t: (b * (tp // tt) + t, 0)),
        out_shape=jax.ShapeDtypeStruct((n, GLA_V), BF16),
        scratch_shapes=[pltpu.VMEM((GLA_DV, GLA_K), F32)],
        compiler_params=_seq_params(),
        name="gla_mixer",
    )(p, p, p, p, p, gk_up_pad, gk_bias, norm_w, tril)


def _hgrn_mixer(p, lb, norm_w, tril, bsz, tp, tt):
    n = p.shape[0]
    col = functools.partial(_pcol, tp=tp, tt=tt)
    return pl.pallas_call(
        _hgrn_kernel,
        grid=(bsz, tp // tt),
        in_specs=[col("hg_q"), col("hg_f"), col("hg_i"), col("hg_g"),
                  _const_spec(lb.shape), _const_spec(norm_w.shape), _const_spec(tril.shape)],
        out_specs=pl.BlockSpec((tt, HG_V), lambda b, t: (b * (tp // tt) + t, 0)),
        out_shape=jax.ShapeDtypeStruct((n, HG_V), BF16),
        scratch_shapes=[pltpu.VMEM((HG_DV, HG_K), F32)],
        compiler_params=_seq_params(),
        name="hgrn_mixer",
    )(p, p, p, p, lb, norm_w, tril)


def _unit_lower_inverses(n_list, blk16, blk32):
    size = n_list[0].shape[0]
    ri = lax.broadcasted_iota(jnp.int32, (size, size), 0)
    ci = lax.broadcasted_iota(jnp.int32, (size, size), 1)
    eye = jnp.where(ri == ci, 1.0, 0.0)
    n16 = [jnp.where(blk16, n, 0.0) for n in n_list]
    n32 = [jnp.where(blk32 & jnp.logical_not(blk16), n, 0.0).astype(BF16) for n in n_list]
    n64 = [jnp.where(blk32, 0.0, n).astype(BF16) for n in n_list]
    t = [eye + n for n in n16]
    m = [n.astype(BF16) for n in n16]
    for it in range(3):
        m = [_mm(x, x).astype(BF16) for x in m]
        t = [ti + _mm(mi, ti) for mi, ti in zip(m, t)]
    for off_diag in (n32, n64):
        tb = [ti.astype(BF16) for ti in t]
        w = [_mm(ti, ni).astype(BF16) for ti, ni in zip(tb, off_diag)]
        t = [ti + _mm(wi, tbi) for ti, wi, tbi in zip(t, w, tb)]
    return t


def _rwkv_kernel(has_vres, *refs):
    if has_vres:
        (r_ref, k_ref, v_ref, g_ref, wa_ref, misc_ref, vf_ref,
         mu_r_ref, mu_k_ref, mu_v_ref, mu_g_ref, mu_wa_ref, mu_misc_ref,
         w0_ref, w2_ref, a0_ref, a2_ref, g2_ref, v0_ref, v2_ref,
         kk_ref, ka_ref, rk_ref, lnw_ref, lnb_ref, seg_ref, tril_ref,
         y_ref, s_ref, carry_ref) = refs
    else:
        (r_ref, k_ref, v_ref, g_ref, wa_ref, misc_ref,
         mu_r_ref, mu_k_ref, mu_v_ref, mu_g_ref, mu_wa_ref, mu_misc_ref,
         w0_ref, w2_ref, a0_ref, a2_ref, g2_ref,
         kk_ref, ka_ref, rk_ref, lnw_ref, lnb_ref, seg_ref, tril_ref,
         y_ref, vf_out_ref, s_ref, carry_ref) = refs

    first = pl.program_id(1) == 0
    tt = r_ref.shape[0]

    @pl.when(first)
    def _():
        s_ref[...] = jnp.zeros_like(s_ref)
        carry_ref[...] = jnp.zeros_like(carry_ref)

    srcs = (r_ref, k_ref, v_ref, g_ref, wa_ref, misc_ref)
    mus = (mu_r_ref, mu_k_ref, mu_v_ref, mu_g_ref, mu_wa_ref, mu_misc_ref)
    shifted = []
    off = 0
    for src, mu in zip(srcs, mus):
        p = src[...].astype(F32)
        width = p.shape[1]
        prev_row = carry_ref[0:1, off:off + width]
        rolled = pltpu.roll(p, 1, 0)
        row = lax.broadcasted_iota(jnp.int32, p.shape, 0)
        prev = jnp.where(row == 0, prev_row, rolled)
        shifted.append(p + (prev - p) * mu[...])
        carry_ref[0:1, off:off + width] = p[tt - 1:tt, :]
        off += width
    r, k, v, s_g, s_wa, s_misc = shifted

    seg = seg_ref[...]
    w_log = -_softplus(-(w0_ref[...] + _mm(jnp.tanh(s_wa), w2_ref[...]))) - 0.5
    lw = -jnp.exp(w_log)
    a = _sigmoid(a0_ref[...] + _mm(s_wa, a2_ref[...]))
    g = _mm(_sigmoid(s_g), g2_ref[...])
    if has_vres:
        v = v + (vf_ref[...] - v) * _sigmoid(v0_ref[...] + _mm(s_misc, v2_ref[...]))
    else:
        vf_out_ref[...] = v
    kk = k * kk_ref[...]
    k = k * (1.0 + (a - 1.0) * ka_ref[...])
    sums = _mm_exact_rhs(jnp.concatenate([kk * kk, r * k * rk_ref[...]], axis=0), seg)
    kk = kk / jnp.maximum(jnp.sqrt(sums[:tt]), 1e-12)
    bonus = sums[tt:] * v
    alpha = -kk
    beta = kk * a
    c_all = _cumsum_rows(tril_ref[...], lw)
    ce_all = c_all - lw

    gs = RW_GROUP
    n_groups = RW_HEADS // RW_GROUP_HEADS
    n_st = RW_GROUP_HEADS * CHUNK
    ri = lax.broadcasted_iota(jnp.int32, (n_st, n_st), 0)
    ci = lax.broadcasted_iota(jnp.int32, (n_st, n_st), 1)
    strict = ci < ri
    blk16 = (ri >> 4) == (ci >> 4)
    blk32 = (ri >> 5) == (ci >> 5)
    tcol = lax.broadcasted_iota(jnp.int32, (CHUNK, n_st), 1) & (CHUNK - 1)
    trow = lax.broadcasted_iota(jnp.int32, (CHUNK, n_st), 0)
    incl = tcol <= trow
    st = functools.partial(_stack_heads, heads=RW_GROUP_HEADS, width=RW_HD)

    blocks = []
    for j in range(tt // CHUNK):
        rows = slice(j * CHUNK, (j + 1) * CHUNK)
        c, ce = c_all[rows], ce_all[rows]
        rj, kj, vj, al, be_ = r[rows], k[rows], v[rows], alpha[rows], beta[rows]
        mid = c[CHUNK // 2 - 1:CHUNK // 2, :]
        c_last = c[CHUNK - 1:CHUNK, :]
        e_out = jnp.exp(mid - c)
        e_end = jnp.exp(c_last - c)
        at = al * jnp.exp(ce - mid)
        rt = rj * jnp.exp(c - mid)
        bh = be_ * e_out
        kh = kj * e_out
        ae = al * jnp.exp(ce)
        rs = rj * jnp.exp(c)
        be = be_ * e_end
        ke = kj * e_end
        dec = jnp.exp(c_last)
        for gi in range(n_groups):
            sl = slice(gi * gs, (gi + 1) * gs)
            stb = lambda a_: st(a_[:, sl]).astype(BF16)
            blocks.append(dict(j=j, gi=gi, bm=stb(bh), km=stb(kh), am=stb(at), v_bd=stb(vj), aem=stb(ae),
                               bem=stb(be), kem=stb(ke), rt=rt[:, sl].astype(BF16),
                               rs=rs[:, sl].astype(BF16), dec=dec[:, sl]))
    for b in blocks:
        b["n_ab"] = jnp.where(strict, _mm_nt(b["am"], b["bm"]), 0.0)
    for b in blocks:
        b["n_ak"] = jnp.where(strict, _mm_nt(b["am"], b["km"]), 0.0).astype(BF16)
    for b in blocks:
        b["a_rb"] = jnp.where(incl, _mm_nt(b["rt"], b["bm"]), 0.0).astype(BF16)
        b["a_rk"] = jnp.where(incl, _mm_nt(b["rt"], b["km"]), 0.0).astype(BF16)
    for b, t_inv in zip(blocks, _unit_lower_inverses([b["n_ab"] for b in blocks], blk16, blk32)):
        b["t_inv"] = t_inv.astype(BF16)
    for b in blocks:
        b["w1"] = _mm(b["n_ak"], b["v_bd"])
        b["y0"] = _mm(b["a_rk"], b["v_bd"])
        b["vk"] = _mm_tn(b["v_bd"], b["kem"])

    state = [s_ref[gi] for gi in range(n_groups)]
    y_rows = []
    for j in range(tt // CHUNK):
        cur = [b for b in blocks if b["j"] == j]
        sb = [state[b["gi"]].astype(BF16) for b in cur]
        xs = [_mm_nt(b["aem"], s) + b["w1"] for b, s in zip(cur, sb)]
        us = [_mm(b["t_inv"], x).astype(BF16) for b, x in zip(cur, xs)]
        for b, u in zip(cur, us):
            state[b["gi"]] = state[b["gi"]] * b["dec"] + _mm_tn(u, b["bem"]) + b["vk"]
        ys = [_mm_nt(b["rs"], s) + _mm(b["a_rb"], u) + b["y0"] for b, s, u in zip(cur, sb, us)]
        y_rows.append(jnp.concatenate(ys, axis=1))
    for gi in range(n_groups):
        s_ref[gi] = state[gi]
    y = jnp.concatenate(y_rows, axis=0)

    inv_hd = 1.0 / RW_HD
    mean = _mm_exact_rhs(y, seg) * inv_hd
    yc = y - mean
    var = _mm_exact_rhs(yc * yc, seg) * inv_hd
    y = yc * lax.rsqrt(var + RW_GN_EPS) * lnw_ref[...] + lnb_ref[...]
    y_ref[...] = ((y + bonus) * g).astype(y_ref.dtype)


def _rwkv_mixer(p, v_first, mus, prm, seg, tril, bsz, tp, tt):
    n = p.shape[0]
    nt = tp // tt
    has_vres = v_first is not None
    row_spec = pl.BlockSpec((tt, RW_DIM), lambda b, t: (b * nt + t, 0))
    col = functools.partial(_pcol, tp=tp, tt=tt)
    in_specs = [col("rw_r"), col("rw_k"), col("rw_v"), col("rw_g"), col("rw_wa"), col("misc")]
    args = [p, p, p, p, p, p]
    if has_vres:
        in_specs.append(row_spec)
        args.append(v_first)
    consts = list(mus) + [prm["w0"], prm["w2"], prm["a0"], prm["a2"], prm["g2"]]
    if has_vres:
        consts += [prm["v0"], prm["v2"]]
    consts += [prm["kk"], prm["ka"], prm["rk"], prm["lnw"], prm["lnb"], seg, tril]
    in_specs += [_const_spec(c.shape) for c in consts]
    args += consts
    out_shape = [jax.ShapeDtypeStruct((n, RW_DIM), BF16)]
    out_specs = [row_spec]
    if not has_vres:
        out_shape.append(jax.ShapeDtypeStruct((n, RW_DIM), F32))
        out_specs.append(row_spec)
    carry_w = 3 * RW_DIM + 3 * LANE
    res = pl.pallas_call(
        functools.partial(_rwkv_kernel, has_vres),
        grid=(bsz, nt),
        in_specs=in_specs,
        out_specs=out_specs,
        out_shape=out_shape,
        scratch_shapes=[pltpu.VMEM((RW_HEADS // RW_GROUP_HEADS, RW_GROUP, RW_GROUP), F32),
                        pltpu.VMEM((SUBLANE, carry_w), F32)],
        compiler_params=_seq_params(),
        name="rwkv_mixer",
    )(*args)
    return (res[0], v_first) if has_vres else (res[0], res[1])


def _merge_kernel(z_ref, gg_ref, gr_ref, gh_ref, yg_ref, yr_ref, yh_ref, wa_ref, wb_ref, wc_ref, wo_ref, o_ref):
    d = lambda y, w: jnp.dot(y[...], w[...], preferred_element_type=F32)
    sg = lambda g: _sigmoid(g[...].astype(F32))
    m = sg(gg_ref) * d(yg_ref, wa_ref) + sg(gr_ref) * d(yr_ref, wb_ref) + sg(gh_ref) * d(yh_ref, wc_ref)
    o_ref[...] = z_ref[...] + _mm(m, wo_ref[...])


def _merge(z2d, p, y_gla, y_rw, y_hg, wa, wb, wc, wo, tm):
    n = z2d.shape[0]
    row = lambda w: pl.BlockSpec((tm, w), lambda i: (i, 0))
    gate = lambda name: pl.BlockSpec((tm, D_MODEL), lambda i, cb=_K_OFF[name][0] // D_MODEL: (i, cb))
    full = lambda a: pl.BlockSpec(a.shape, lambda i: (0, 0))
    return pl.pallas_call(
        _merge_kernel,
        grid=(n // tm,),
        in_specs=[row(D_MODEL), gate("gate_gla"), gate("gate_rw"), gate("gate_hg"),
                  row(GLA_V), row(RW_DIM), row(HG_V), full(wa), full(wb), full(wc), full(wo)],
        out_specs=row(D_MODEL),
        out_shape=jax.ShapeDtypeStruct((n, D_MODEL), F32),
        compiler_params=pltpu.CompilerParams(dimension_semantics=("arbitrary",), vmem_limit_bytes=VMEM_LIMIT),
        name="merge",
    )(z2d, p, p, p, y_gla, y_rw, y_hg, wa, wb, wc, wo)


FFN_FC = 256
FFN_HALO = SUBLANE


def _ffn_kernel(final, z_ref, halo_ref, nw_ref, wug_ref, wuv_ref, cwg_ref, cwv_ref, cbg_ref, cbv_ref, wd_ref,
                fnw_ref, o_ref, act_ref):
    tt = z_ref.shape[0]
    zt = z_ref[...]
    nw = nw_ref[...]
    h = jnp.concatenate([_rms(halo_ref[...], nw), _rms(zt, nw)], axis=0).astype(BF16)
    has_prev = pl.program_id(1) > 0

    def up(j):
        return (jnp.dot(h, wug_ref[j], preferred_element_type=F32),
                jnp.dot(h, wuv_ref[j], preferred_element_type=F32))

    def conv(u, cw, cb):
        u = jnp.concatenate([jnp.where(has_prev, u[:FFN_HALO], 0.0), u[FFN_HALO:]], axis=0)
        inner = u * cw[1:2, :] + pltpu.roll(u * cw[0:1, :], 1, 0)
        c = cb + u * cw[2:3, :] + pltpu.roll(inner, 1, 0)
        return c[FFN_HALO:, :]

    n_chunks = wug_ref.shape[0]
    fc = wug_ref.shape[2]
    split = (n_chunks + 1) // 2
    out = zt
    u_next = up(0)
    for j in range(n_chunks):
        ug, uv = u_next
        if j + 1 < n_chunks:
            u_next = up(j + 1)
        cg = conv(ug, cwg_ref[j], cbg_ref[j])
        cv = conv(uv, cwv_ref[j], cbv_ref[j])
        act_ref[:, j * fc:(j + 1) * fc] = (cg * cv / (1.0 + jnp.exp(-cg))).astype(BF16)
        if j + 1 in (split, n_chunks):
            lo = 0 if j + 1 == split else split
            out = out + jnp.dot(act_ref[:, lo * fc:(j + 1) * fc], wd_ref[lo * fc:(j + 1) * fc, :],
                                preferred_element_type=F32)
    if final:
        out = _rms(out, fnw_ref[...])
    o_ref[...] = out


def _ffn(z3d, norm_w, wug, wuv, cwg, cwv, cbg, cbv, wd, final_w, tt, final):
    bsz, tp, _ = z3d.shape
    hb = tt // FFN_HALO
    once = lambda a: pl.BlockSpec(a.shape, lambda b, t: (0,) * a.ndim, pipeline_mode=pl.Buffered(1))
    return pl.pallas_call(
        functools.partial(_ffn_kernel, final),
        grid=(bsz, tp // tt),
        in_specs=[pl.BlockSpec((None, tt, D_MODEL), lambda b, t: (b, t, 0)),
                  pl.BlockSpec((None, FFN_HALO, D_MODEL), lambda b, t: (b, jnp.maximum(t * hb - 1, 0), 0)),
                  once(norm_w), once(wug), once(wuv), once(cwg), once(cwv), once(cbg), once(cbv), once(wd),
                  once(final_w)],
        out_specs=pl.BlockSpec((None, tt, D_MODEL), lambda b, t: (b, t, 0)),
        out_shape=jax.ShapeDtypeStruct((bsz, tp, D_MODEL), F32),
        scratch_shapes=[pltpu.VMEM((tt, D_FF), BF16)],
        compiler_params=_seq_params(),
        name="conv_ffn",
    )(z3d, z3d, norm_w, wug, wuv, cwg, cwv, cbg, cbv, wd, final_w)


def _largest_tile(n, cands):
    for c in cands:
        if n % c == 0:
            return c
    raise ValueError(f"no tile for {n}")


def _ref_cols(w, name):
    off, width = _REF_OFF[name]
    return w[..., off:off + width]


def _pack_columns(w, vres):
    lead = w.shape[:-1]
    zeros = lambda n: jnp.zeros(lead + (n,), w.dtype)
    vr = vres if vres is not None else zeros(RW_V_RANK)
    parts = []
    for name, width in _K_LAYOUT:
        if name == "rw_wa":
            parts.append(jnp.concatenate([_ref_cols(w, "rw_w"), _ref_cols(w, "rw_a")], axis=-1))
        elif name == "misc":
            parts.append(jnp.concatenate(
                [_ref_cols(w, "gla_gk"), vr, zeros(width - GLA_GATE_RANK - RW_V_RANK)], axis=-1))
        else:
            parts.append(_ref_cols(w, name))
    used = sum(wd for _, wd in _K_LAYOUT)
    parts.append(zeros(WP - used))
    return jnp.concatenate(parts, axis=-1)


def _pad_rows(w, lo, total):
    return jnp.pad(w, ((lo, total - lo - w.shape[0]), (0, 0)))


def kernel(x, meta, mix_norm, w_in, w_in_vres, rw_mu, rw_mu_vres, gla_gk_up, gla_gk_bias, gla_norm, rw_w0,
           rw_w2, rw_a0, rw_a2, rw_v0, rw_v2, rw_g2, rw_kk, rw_ka, rw_rk, rw_ln_w, rw_ln_b, hg_lb_logits,
           hg_norm, w_out_gla, w_out_rw, w_out_hg, w_out, ffn_norm, w_up, conv_w, conv_b, w_down, final_norm):
    bsz, seq, _ = x.shape
    depth = w_in.shape[0]
    t_real = seq + N_META
    tp = -(-t_real // CHUNK) * CHUNK
    n = bsz * tp
    tm = _largest_tile(n, (1024, 512, 256, 128, 64))
    tm_merge = _largest_tile(n, (512, 256, 128, 64))
    tt = _largest_tile(tp, (704, 512, 384, 256, 192, 128, 64))
    tt_mix = _largest_tile(tp, (3 * CHUNK, 2 * CHUNK, CHUNK))

    z = jnp.concatenate([jnp.broadcast_to(meta.astype(x.dtype)[None], (bsz, N_META, D_MODEL)), x,
                         jnp.zeros((bsz, tp - t_real, D_MODEL), x.dtype)], axis=1)

    row = lambda a: a.reshape(1, -1).astype(F32)
    t_idx = jnp.arange(tt_mix)
    tril = ((t_idx[:, None] // CHUNK == t_idx[None, :] // CHUNK)
            & (t_idx[None, :] <= t_idx[:, None])).astype(BF16)
    head_of = jnp.arange(RW_DIM) // RW_HD
    seg = (head_of[:, None] == head_of[None, :]).astype(BF16)
    lb_p = jax.nn.softmax(hg_lb_logits.astype(F32), axis=0)
    lb_all = jnp.cumsum(lb_p, axis=0) - lb_p[0:1]

    nj = D_FF // FFN_FC
    v_first = None
    for i in range(depth):
        vres_w = w_in_vres[i - 1] if i > 0 else None
        vres_mu = rw_mu_vres[i - 1] if i > 0 else None
        w_cat = _pack_columns(w_in[i], vres_w).astype(BF16)
        mu_full = jnp.concatenate([jnp.zeros((W_IN - rw_mu.shape[1],), F32), rw_mu[i].astype(F32)])
        mu_cat = _pack_columns(mu_full, vres_mu)
        mus = [row(mu_cat[_K_OFF[nm][0]:_K_OFF[nm][0] + _K_OFF[nm][1]])
               for nm in ("rw_r", "rw_k", "rw_v", "rw_g", "rw_wa", "misc")]

        z2d = z.reshape(n, D_MODEL)
        p = _inproj(z2d, row(mix_norm[i]), w_cat, tm)

        gk_up_pad = _pad_rows(gla_gk_up[i], MISC_GK, LANE).astype(BF16)
        y_gla = _gla_mixer(p, gk_up_pad, row(gla_gk_bias[i]), row(gla_norm[i]), tril, bsz, tp, tt_mix)
        y_hg = _hgrn_mixer(p, row(lb_all[i]), row(hg_norm[i]), tril, bsz, tp, tt_mix)

        prm = {
            "w0": row(rw_w0[i]), "w2": _pad_rows(rw_w2[i], 0, LANE).astype(BF16),
            "a0": row(rw_a0[i]), "a2": _pad_rows(rw_a2[i], RW_W_RANK, LANE).astype(BF16),
            "g2": rw_g2[i].astype(BF16),
            "kk": row(rw_kk[i]), "ka": row(rw_ka[i]), "rk": row(rw_rk[i]),
            "lnw": row(rw_ln_w[i]), "lnb": row(rw_ln_b[i]),
        }
        if i > 0:
            prm["v0"] = row(rw_v0[i - 1])
            prm["v2"] = _pad_rows(rw_v2[i - 1], MISC_VR, LANE).astype(BF16)
        y_rw, v_first = _rwkv_mixer(p, v_first, mus, prm, seg, tril, bsz, tp, tt_mix)

        z2d = _merge(z2d, p, y_gla, y_rw, y_hg, w_out_gla[i].astype(BF16), w_out_rw[i].astype(BF16),
                     w_out_hg[i].astype(BF16), w_out[i].astype(BF16), tm_merge)

        chunked = lambda a: a.reshape(a.shape[0], nj, FFN_FC).transpose(1, 0, 2)
        wug = chunked(w_up[i][:, :D_FF]).astype(BF16)
        wuv = chunked(w_up[i][:, D_FF:]).astype(BF16)
        cwg = jnp.pad(chunked(conv_w[i][:, :D_FF]), ((0, 0), (0, SUBLANE - CONV_W), (0, 0))).astype(F32)
        cwv = jnp.pad(chunked(conv_w[i][:, D_FF:]), ((0, 0), (0, SUBLANE - CONV_W), (0, 0))).astype(F32)
        cbg = chunked(conv_b[i][None, :D_FF]).astype(F32)
        cbv = chunked(conv_b[i][None, D_FF:]).astype(F32)
        wd = w_down[i].astype(BF16)
        z = _ffn(z2d.reshape(bsz, tp, D_MODEL), row(ffn_norm[i]), wug, wuv, cwg, cwv, cbg, cbv, wd,
                 row(final_norm), tt, final=(i == depth - 1))
    return z[:, N_META:t_real]
```

```python
import functools

import jax
import jax.numpy as jnp
from jax import lax
from jax.experimental import pallas as pl
from jax.experimental.pallas import tpu as pltpu

F32 = jnp.float32
BF16 = jnp.bfloat16

D_MODEL = 1024
N_META = 16
F_TINY = 1e-30
NORM_EPS = 1e-6

GLA_HEADS, GLA_DK, GLA_DV = 4, 64, 128
GLA_K, GLA_V = GLA_HEADS * GLA_DK, GLA_HEADS * GLA_DV
GLA_GATE_RANK = 16
GLA_GATE_NORM = 16.0

RW_HEADS, RW_HD = 8, 64
RW_DIM = RW_HEADS * RW_HD
RW_W_RANK, RW_A_RANK, RW_V_RANK, RW_G_RANK = 64, 64, 32, 128
RW_GN_EPS = 64e-5
RW_GROUP_HEADS = 4
RW_GROUP = RW_GROUP_HEADS * RW_HD

HG_HEADS, HG_DK, HG_DV = 4, 128, 128
HG_K, HG_V = HG_HEADS * HG_DK, HG_HEADS * HG_DV

D_FF = 2816
CONV_W = 3

_REF_LAYOUT = (
    ("gla_q", GLA_K), ("gla_k", GLA_K), ("gla_v", GLA_V), ("gla_gk", GLA_GATE_RANK), ("gla_g", GLA_V),
    ("hg_q", HG_K), ("hg_f", HG_K), ("hg_i", HG_V), ("hg_g", HG_V),
    ("gate_gla", D_MODEL), ("gate_rw", D_MODEL), ("gate_hg", D_MODEL),
    ("rw_r", RW_DIM), ("rw_w", RW_W_RANK), ("rw_k", RW_DIM), ("rw_v", RW_DIM), ("rw_a", RW_A_RANK),
    ("rw_g", RW_G_RANK),
)
_REF_OFF = {}
_o = 0
for _n, _w in _REF_LAYOUT:
    _REF_OFF[_n] = (_o, _w)
    _o += _w
W_IN = _o

LANE = 128
SUBLANE = 8
_K_LAYOUT = (
    ("gate_gla", 1024), ("gate_rw", 1024), ("gate_hg", 1024),
    ("gla_v", 512), ("gla_g", 512), ("hg_q", 512), ("hg_f", 512), ("hg_i", 512), ("hg_g", 512),
    ("rw_r", 512), ("rw_k", 512), ("rw_v", 512),
    ("gla_q", 256), ("gla_k", 256),
    ("rw_g", 128), ("rw_wa", 128), ("misc", 128),
)
_K_OFF = {}
_o = 0
for _n, _w in _K_LAYOUT:
    assert _o % _w == 0
    _K_OFF[_n] = (_o, _w)
    _o += _w
IN_TN = 512
WP = -(-_o // IN_TN) * IN_TN
N_CT = WP // IN_TN
CT_QK = _K_OFF["gla_q"][0] // IN_TN
CT_SMALL = _K_OFF["rw_g"][0] // IN_TN
assert _K_OFF["gla_k"][0] // IN_TN == CT_QK and _K_OFF["misc"][0] // IN_TN == CT_SMALL
SMALL_G, SMALL_WA, SMALL_MISC = (_K_OFF[_n][0] - CT_SMALL * IN_TN for _n in ("rw_g", "rw_wa", "misc"))
MISC_GK = 0
MISC_VR = GLA_GATE_RANK

CHUNK = 64
SUB = 32
VMEM_LIMIT = 56 * 1024 * 1024


def _mm(a, b):
    return jnp.dot(a.astype(BF16), b.astype(BF16), preferred_element_type=F32)


def _mm_nt(a, b):
    return lax.dot_general(a.astype(BF16), b.astype(BF16), (((1,), (1,)), ((), ())),
                           preferred_element_type=F32)


def _mm_tn(a, b):
    return lax.dot_general(a.astype(BF16), b.astype(BF16), (((0,), (0,)), ((), ())),
                           preferred_element_type=F32)


def _split2(x):
    hi = x.astype(BF16)
    lo = (x - hi.astype(F32)).astype(BF16)
    return hi, lo


def _mm_exact_rhs(a_f32, b_bf16):
    m = a_f32.shape[0]
    out = jnp.dot(jnp.concatenate(_split2(a_f32), axis=0), b_bf16, preferred_element_type=F32)
    return out[:m] + out[m:]


def _cumsum_rows(tril_bf16, g):
    w = g.shape[1]
    out = jnp.dot(tril_bf16, jnp.concatenate(_split2(g), axis=1), preferred_element_type=F32)
    return out[:, :w] + out[:, w:]


def _sigmoid(x):
    return 1.0 / (1.0 + jnp.exp(-x))


def _softplus(x):
    return jnp.maximum(x, 0.0) + jnp.log(1.0 + jnp.exp(-jnp.abs(x)))


def _rms(x, w):
    return x * lax.rsqrt(jnp.mean(x * x, axis=-1, keepdims=True) + NORM_EPS) * w


def _stack_heads(x, heads, width):
    lane = lax.broadcasted_iota(jnp.int32, x.shape, 1)
    return jnp.concatenate(
        [jnp.where((lane >= h * width) & (lane < (h + 1) * width), x, 0.0) for h in range(heads)], axis=0)


def _inproj_kernel(z_ref, nw_ref, w_ref, o_ref, h_ref):
    @pl.when(pl.program_id(1) == 0)
    def _():
        h_ref[...] = _rms(z_ref[...], nw_ref[...]).astype(BF16)

    o_ref[...] = jnp.dot(h_ref[...], w_ref[...], preferred_element_type=F32).astype(o_ref.dtype)


def _inproj(z2d, norm_w, w_bf16, tm):
    n = z2d.shape[0]
    return pl.pallas_call(
        _inproj_kernel,
        grid=(n // tm, N_CT),
        in_specs=[
            pl.BlockSpec((tm, D_MODEL), lambda i, j: (i, 0)),
            pl.BlockSpec((1, D_MODEL), lambda i, j: (0, 0)),
            pl.BlockSpec((None, D_MODEL, IN_TN), lambda i, j: (j, 0, 0)),
        ],
        out_specs=pl.BlockSpec((None, tm, IN_TN), lambda i, j: (j, i, 0)),
        out_shape=jax.ShapeDtypeStruct((N_CT, n, IN_TN), BF16),
        scratch_shapes=[pltpu.VMEM((tm, D_MODEL), BF16)],
        compiler_params=pltpu.CompilerParams(
            dimension_semantics=("arbitrary", "arbitrary"), vmem_limit_bytes=VMEM_LIMIT),
        name="inproj",
    )(z2d, norm_w, w_bf16)


def _gla_tile(q, k, v, g, tril_bd, s_ref, norm_w, heads, dk, first):
    tt, hk = q.shape
    assert tt % CHUNK == 0 and CHUNK == 2 * SUB
    dv = v.shape[1] // heads

    @pl.when(first)
    def _():
        s_ref[...] = jnp.zeros_like(s_ref)

    c_all = _cumsum_rows(tril_bd, g)
    blk1 = lax.broadcasted_iota(jnp.int32, (CHUNK, hk), 0) >= SUB
    st = functools.partial(_stack_heads, heads=heads, width=dk)
    n_st = heads * CHUNK
    causal = (lax.broadcasted_iota(jnp.int32, (n_st, n_st), 1)
              <= lax.broadcasted_iota(jnp.int32, (n_st, n_st), 0))
    chunks = []
    for j in range(tt // CHUNK):
        rows = slice(j * CHUNK, (j + 1) * CHUNK)
        c, qj, kj, vj = c_all[rows], q[rows], k[rows], v[rows]
        r0 = c[SUB // 2 - 1:SUB // 2, :]
        r1 = c[SUB + SUB // 2 - 1:SUB + SUB // 2, :]
        c_last = c[CHUNK - 1:CHUNK, :]
        kt = kj * jnp.exp(jnp.where(blk1, r1, r0) - c)
        qa = qj * jnp.exp(c - r0)
        qb = jnp.where(blk1, qj * jnp.exp(jnp.where(blk1, c - r1, 0.0)), 0.0)
        kt0 = jnp.where(blk1, 0.0, kt)
        kt1 = jnp.where(blk1, kt, 0.0)
        qs = jnp.concatenate([st(qa), st(qb)], axis=1).astype(BF16)
        ks = jnp.concatenate([st(kt0), st(kt1)], axis=1).astype(BF16)
        vs = jnp.concatenate([vj[:, h * dv:(h + 1) * dv] for h in range(heads)], axis=0).astype(BF16)
        chunks.append(dict(qs=qs, ks=ks, vs=vs, qe=st(qj * jnp.exp(c)).astype(BF16),
                           kd=st(kj * jnp.exp(c_last - c)).astype(BF16), dec=jnp.exp(c_last)))
    for ch in chunks:
        ch["a"] = jnp.where(causal, _mm_nt(ch["qs"], ch["ks"]), 0.0)
    for ch in chunks:
        ch["oa"] = _mm(ch["a"], ch["vs"])
        ch["kv"] = _mm_tn(ch["vs"], ch["kd"])
    s_t = s_ref[...]
    outs = []
    for ch in chunks:
        o = ch["oa"] + _mm_nt(ch["qe"], s_t)
        s_t = s_t * ch["dec"] + ch["kv"]
        o = o * lax.rsqrt(jnp.mean(o * o, axis=-1, keepdims=True) + NORM_EPS) * norm_w
        outs.append(jnp.concatenate([o[h * CHUNK:(h + 1) * CHUNK] for h in range(heads)], axis=1))
    s_ref[...] = s_t
    return jnp.concatenate(outs, axis=0)


def _silu(x):
    return x * _sigmoid(x)


def _gla_kernel(qk_ref, v_ref, small_ref, og_ref, gkup_ref, gkb_ref, nw_ref, tril_ref, y_ref, s_ref):
    qk = qk_ref[...].astype(F32)
    q = qk[:, :GLA_K] * (GLA_DK ** -0.5)
    x = _mm(small_ref[:, SMALL_MISC:SMALL_MISC + LANE], gkup_ref[...]) + gkb_ref[...]
    g = -_softplus(-x) * (1.0 / GLA_GATE_NORM)
    o = _gla_tile(q, qk[:, GLA_K:], v_ref[...].astype(F32), g, tril_ref[...], s_ref, nw_ref[...],
                  GLA_HEADS, GLA_DK, pl.program_id(1) == 0)
    y_ref[...] = (o * _silu(og_ref[...].astype(F32))).astype(y_ref.dtype)


def _hgrn_kernel(q_ref, f_ref, i_ref, og_ref, lb_ref, nw_ref, tril_ref, y_ref, s_ref):
    z = f_ref[...].astype(F32)
    lb = lb_ref[...]
    forget = lb + (1.0 - lb) * _sigmoid(z)
    g = jnp.log(jnp.maximum(forget, F_TINY))
    k = (1.0 - lb) * _sigmoid(-z)
    o = _gla_tile(q_ref[...].astype(F32), k, i_ref[...].astype(F32), g, tril_ref[...], s_ref, nw_ref[...],
                  HG_HEADS, HG_DK, pl.program_id(1) == 0)
    y_ref[...] = (o * _silu(og_ref[...].astype(F32))).astype(y_ref.dtype)


def _pcol(name, tp, tt):
    ct = (name if isinstance(name, int) else _K_OFF[name][0] // IN_TN)
    nt = tp // tt
    return pl.BlockSpec((None, tt, IN_TN), lambda b, t: (ct, b * nt + t, 0))


def _const_spec(shape):
    nd = len(shape)
    return pl.BlockSpec(shape, lambda b, t: (0,) * nd)


def _seq_params():
    return pltpu.CompilerParams(dimension_semantics=("arbitrary", "arbitrary"), vmem_limit_bytes=VMEM_LIMIT)


def _gla_mixer(p, gk_up_pad, gk_bias, norm_w, tril, bsz, tp, tt):
    n = p.shape[1]
    col = functools.partial(_pcol, tp=tp, tt=tt)
    return pl.pallas_call(
        _gla_kernel,
        grid=(bsz, tp // tt),
        in_specs=[col(CT_QK), col("gla_v"), col(CT_SMALL), col("gla_g"),
                  _const_spec(gk_up_pad.shape), _const_spec(gk_bias.shape),
                  _const_spec(norm_w.shape), _const_spec(tril.shape)],
        out_specs=pl.BlockSpec((tt, GLA_V), lambda b, t: (b * (tp // tt) + t, 0)),
        out_shape=jax.ShapeDtypeStruct((n, GLA_V), BF16),
        scratch_shapes=[pltpu.VMEM((GLA_DV, GLA_K), F32)],
        compiler_params=_seq_params(),
        name="gla_mixer",
    )(p, p, p, p, gk_up_pad, gk_bias, norm_w, tril)


def _hgrn_mixer(p, lb, norm_w, tril, bsz, tp, tt):
    n = p.shape[1]
    col = functools.partial(_pcol, tp=tp, tt=tt)
    return pl.pallas_call(
        _hgrn_kernel,
        grid=(bsz, tp // tt),
        in_specs=[col("hg_q"), col("hg_f"), col("hg_i"), col("hg_g"),
                  _const_spec(lb.shape), _const_spec(norm_w.shape), _const_spec(tril.shape)],
        out_specs=pl.BlockSpec((tt, HG_V), lambda b, t: (b * (tp // tt) + t, 0)),
        out_shape=jax.ShapeDtypeStruct((n, HG_V), BF16),
        scratch_shapes=[pltpu.VMEM((HG_DV, HG_K), F32)],
        compiler_params=_seq_params(),
        name="hgrn_mixer",
    )(p, p, p, p, lb, norm_w, tril)


def _unit_lower_inverses(n_list, blk16, blk32):
    size = n_list[0].shape[0]
    ri = lax.broadcasted_iota(jnp.int32, (size, size), 0)
    ci = lax.broadcasted_iota(jnp.int32, (size, size), 1)
    eye = jnp.where(ri == ci, 1.0, 0.0)
    n16 = [jnp.where(blk16, n, 0.0) for n in n_list]
    n32 = [jnp.where(blk32 & jnp.logical_not(blk16), n, 0.0).astype(BF16) for n in n_list]
    n64 = [jnp.where(blk32, 0.0, n).astype(BF16) for n in n_list]
    t = [eye + n for n in n16]
    m = [n.astype(BF16) for n in n16]
    for it in range(3):
        m = [_mm(x, x).astype(BF16) for x in m]
        t = [ti + _mm(mi, ti) for mi, ti in zip(m, t)]
    for off_diag in (n32, n64):
        tb = [ti.astype(BF16) for ti in t]
        w = [_mm(ti, ni).astype(BF16) for ti, ni in zip(tb, off_diag)]
        t = [ti + _mm(wi, tbi) for ti, wi, tbi in zip(t, w, tb)]
    return t


def _rwkv_kernel(has_vres, *refs):
    if has_vres:
        (r_ref, k_ref, v_ref, small_ref, vf_ref,
         mu_r_ref, mu_k_ref, mu_v_ref, mu_small_ref,
         w0_ref, w2_ref, a0_ref, a2_ref, g2_ref, v0_ref, v2_ref,
         kk_ref, ka_ref, rk_ref, lnw_ref, lnb_ref, seg_ref, tril_ref,
         y_ref, s_ref, carry_ref) = refs
    else:
        (r_ref, k_ref, v_ref, small_ref,
         mu_r_ref, mu_k_ref, mu_v_ref, mu_small_ref,
         w0_ref, w2_ref, a0_ref, a2_ref, g2_ref,
         kk_ref, ka_ref, rk_ref, lnw_ref, lnb_ref, seg_ref, tril_ref,
         y_ref, vf_out_ref, s_ref, carry_ref) = refs

    first = pl.program_id(1) == 0
    tt = r_ref.shape[0]

    @pl.when(first)
    def _():
        s_ref[...] = jnp.zeros_like(s_ref)
        carry_ref[...] = jnp.zeros_like(carry_ref)

    srcs = (r_ref, k_ref, v_ref, small_ref)
    mus = (mu_r_ref, mu_k_ref, mu_v_ref, mu_small_ref)
    shifted = []
    off = 0
    for src, mu in zip(srcs, mus):
        p = src[...].astype(F32)
        width = p.shape[1]
        prev_row = carry_ref[0:1, off:off + width]
        rolled = pltpu.roll(p, 1, 0)
        row = lax.broadcasted_iota(jnp.int32, p.shape, 0)
        prev = jnp.where(row == 0, prev_row, rolled)
        shifted.append(p + (prev - p) * mu[...])
        carry_ref[0:1, off:off + width] = p[tt - 1:tt, :]
        off += width
    r, k, v, small = shifted
    s_g = small[:, SMALL_G:SMALL_G + LANE]
    s_wa = small[:, SMALL_WA:SMALL_WA + LANE]
    s_misc = small[:, SMALL_MISC:SMALL_MISC + LANE]

    seg = seg_ref[...]
    w_log = -_softplus(-(w0_ref[...] + _mm(jnp.tanh(s_wa), w2_ref[...]))) - 0.5
    lw = -jnp.exp(w_log)
    a = _sigmoid(a0_ref[...] + _mm(s_wa, a2_ref[...]))
    g = _mm(_sigmoid(s_g), g2_ref[...])
    if has_vres:
        v = v + (vf_ref[...] - v) * _sigmoid(v0_ref[...] + _mm(s_misc, v2_ref[...]))
    else:
        vf_out_ref[...] = v
    kk = k * kk_ref[...]
    k = k * (1.0 + (a - 1.0) * ka_ref[...])
    sums = _mm_exact_rhs(jnp.concatenate([kk * kk, r * k * rk_ref[...]], axis=0), seg)
    kk = kk / jnp.maximum(jnp.sqrt(sums[:tt]), 1e-12)
    bonus = sums[tt:] * v
    alpha = -kk
    beta = kk * a
    c_all = _cumsum_rows(tril_ref[...], lw)
    ce_all = c_all - lw

    gs = RW_GROUP
    n_groups = RW_HEADS // RW_GROUP_HEADS
    n_st = RW_GROUP_HEADS * CHUNK
    ri = lax.broadcasted_iota(jnp.int32, (n_st, n_st), 0)
    ci = lax.broadcasted_iota(jnp.int32, (n_st, n_st), 1)
    strict = ci < ri
    blk16 = (ri >> 4) == (ci >> 4)
    blk32 = (ri >> 5) == (ci >> 5)
    tcol = lax.broadcasted_iota(jnp.int32, (CHUNK, n_st), 1) & (CHUNK - 1)
    trow = lax.broadcasted_iota(jnp.int32, (CHUNK, n_st), 0)
    incl = tcol <= trow
    st = functools.partial(_stack_heads, heads=RW_GROUP_HEADS, width=RW_HD)

    blocks = []
    for j in range(tt // CHUNK):
        rows = slice(j * CHUNK, (j + 1) * CHUNK)
        c, ce = c_all[rows], ce_all[rows]
        rj, kj, vj, al, be_ = r[rows], k[rows], v[rows], alpha[rows], beta[rows]
        mid = c[CHUNK // 2 - 1:CHUNK // 2, :]
        c_last = c[CHUNK - 1:CHUNK, :]
        e_out = jnp.exp(mid - c)
        e_end = jnp.exp(c_last - c)
        at = al * jnp.exp(ce - mid)
        rt = rj * jnp.exp(c - mid)
        bh = be_ * e_out
        kh = kj * e_out
        ae = al * jnp.exp(ce)
        rs = rj * jnp.exp(c)
        be = be_ * e_end
        ke = kj * e_end
        dec = jnp.exp(c_last)
        for gi in range(n_groups):
            sl = slice(gi * gs, (gi + 1) * gs)
            stb = lambda a_: st(a_[:, sl]).astype(BF16)
            blocks.append(dict(j=j, gi=gi, bm=stb(bh), km=stb(kh), am=stb(at), v_bd=stb(vj), aem=stb(ae),
                               bem=stb(be), kem=stb(ke), rt=rt[:, sl].astype(BF16),
                               rs=rs[:, sl].astype(BF16), dec=dec[:, sl]))
    for b in blocks:
        b["n_ab"] = jnp.where(strict, _mm_nt(b["am"], b["bm"]), 0.0)
    for b in blocks:
        b["n_ak"] = jnp.where(strict, _mm_nt(b["am"], b["km"]), 0.0).astype(BF16)
    for b in blocks:
        b["a_rb"] = jnp.where(incl, _mm_nt(b["rt"], b["bm"]), 0.0).astype(BF16)
        b["a_rk"] = jnp.where(incl, _mm_nt(b["rt"], b["km"]), 0.0).astype(BF16)
    for b, t_inv in zip(blocks, _unit_lower_inverses([b["n_ab"] for b in blocks], blk16, blk32)):
        b["t_inv"] = t_inv.astype(BF16)
    for b in blocks:
        b["w1"] = _mm(b["n_ak"], b["v_bd"])
        b["y0"] = _mm(b["a_rk"], b["v_bd"])
        b["vk"] = _mm_tn(b["v_bd"], b["kem"])

    state = [s_ref[gi] for gi in range(n_groups)]
    y_rows = []
    for j in range(tt // CHUNK):
        cur = [b for b in blocks if b["j"] == j]
        sb = [state[b["gi"]].astype(BF16) for b in cur]
        xs = [_mm_nt(b["aem"], s) + b["w1"] for b, s in zip(cur, sb)]
        us = [_mm(b["t_inv"], x).astype(BF16) for b, x in zip(cur, xs)]
        for b, u in zip(cur, us):
            state[b["gi"]] = state[b["gi"]] * b["dec"] + _mm_tn(u, b["bem"]) + b["vk"]
        ys = [_mm_nt(b["rs"], s) + _mm(b["a_rb"], u) + b["y0"] for b, s, u in zip(cur, sb, us)]
        y_rows.append(jnp.concatenate(ys, axis=1))
    for gi in range(n_groups):
        s_ref[gi] = state[gi]
    y = jnp.concatenate(y_rows, axis=0)

    inv_hd = 1.0 / RW_HD
    mean = _mm_exact_rhs(y, seg) * inv_hd
    yc = y - mean
    var = _mm_exact_rhs(yc * yc, seg) * inv_hd
    y = yc * lax.rsqrt(var + RW_GN_EPS) * lnw_ref[...] + lnb_ref[...]
    y_ref[...] = ((y + bonus) * g).astype(y_ref.dtype)


def _rwkv_mixer(p, v_first, mus, prm, seg, tril, bsz, tp, tt):
    n = p.shape[1]
    nt = tp // tt
    has_vres = v_first is not None
    row_spec = pl.BlockSpec((tt, RW_DIM), lambda b, t: (b * nt + t, 0))
    col = functools.partial(_pcol, tp=tp, tt=tt)
    in_specs = [col("rw_r"), col("rw_k"), col("rw_v"), col(CT_SMALL)]
    args = [p, p, p, p]
    if has_vres:
        in_specs.append(row_spec)
        args.append(v_first)
    consts = list(mus) + [prm["w0"], prm["w2"], prm["a0"], prm["a2"], prm["g2"]]
    if has_vres:
        consts += [prm["v0"], prm["v2"]]
    consts += [prm["kk"], prm["ka"], prm["rk"], prm["lnw"], prm["lnb"], seg, tril]
    in_specs += [_const_spec(c.shape) for c in consts]
    args += consts
    out_shape = [jax.ShapeDtypeStruct((n, RW_DIM), BF16)]
    out_specs = [row_spec]
    if not has_vres:
        out_shape.append(jax.ShapeDtypeStruct((n, RW_DIM), F32))
        out_specs.append(row_spec)
    carry_w = 3 * RW_DIM + IN_TN
    res = pl.pallas_call(
        functools.partial(_rwkv_kernel, has_vres),
        grid=(bsz, nt),
        in_specs=in_specs,
        out_specs=out_specs,
        out_shape=out_shape,
        scratch_shapes=[pltpu.VMEM((RW_HEADS // RW_GROUP_HEADS, RW_GROUP, RW_GROUP), F32),
                        pltpu.VMEM((SUBLANE, carry_w), F32)],
        compiler_params=_seq_params(),
        name="rwkv_mixer",
    )(*args)
    return (res[0], v_first) if has_vres else (res[0], res[1])


def _merge_kernel(z_ref, gg_ref, gr_ref, gh_ref, yg_ref, yr_ref, yh_ref, wa_ref, wb_ref, wc_ref, wo_ref, o_ref):
    d = lambda y, w: jnp.dot(y[...], w[...], preferred_element_type=F32)
    sg = lambda g: _sigmoid(jnp.concatenate([g[t] for t in range(g.shape[0])], axis=1).astype(F32))
    m = sg(gg_ref) * d(yg_ref, wa_ref) + sg(gr_ref) * d(yr_ref, wb_ref) + sg(gh_ref) * d(yh_ref, wc_ref)
    o_ref[...] = z_ref[...] + _mm(m, wo_ref[...])


def _merge(z2d, p, y_gla, y_rw, y_hg, wa, wb, wc, wo, tm):
    n = z2d.shape[0]
    row = lambda w: pl.BlockSpec((tm, w), lambda i: (i, 0))
    gate = lambda name: pl.BlockSpec((D_MODEL // IN_TN, tm, IN_TN),
                                     lambda i, cb=_K_OFF[name][0] // D_MODEL: (cb, i, 0))
    full = lambda a: pl.BlockSpec(a.shape, lambda i: (0, 0))
    return pl.pallas_call(
        _merge_kernel,
        grid=(n // tm,),
        in_specs=[row(D_MODEL), gate("gate_gla"), gate("gate_rw"), gate("gate_hg"),
                  row(GLA_V), row(RW_DIM), row(HG_V), full(wa), full(wb), full(wc), full(wo)],
        out_specs=row(D_MODEL),
        out_shape=jax.ShapeDtypeStruct((n, D_MODEL), F32),
        compiler_params=pltpu.CompilerParams(dimension_semantics=("arbitrary",), vmem_limit_bytes=VMEM_LIMIT),
        name="merge",
    )(z2d, p, p, p, y_gla, y_rw, y_hg, wa, wb, wc, wo)


FFN_FC = 256
FFN_HALO = SUBLANE


def _ffn_kernel(final, z_ref, halo_ref, nw_ref, wug_ref, wuv_ref, cwg_ref, cwv_ref, cbg_ref, cbv_ref, wd_ref,
                fnw_ref, o_ref, act_ref):
    tt = z_ref.shape[0]
    zt = z_ref[...]
    nw = nw_ref[...]
    h = jnp.concatenate([_rms(halo_ref[...], nw), _rms(zt, nw)], axis=0).astype(BF16)
    has_prev = pl.program_id(1) > 0

    def up(j):
        return (jnp.dot(h, wug_ref[j], preferred_element_type=F32),
                jnp.dot(h, wuv_ref[j], preferred_element_type=F32))

    def conv(u, cw, cb):
        u = jnp.concatenate([jnp.where(has_prev, u[:FFN_HALO], 0.0), u[FFN_HALO:]], axis=0)
        inner = u * cw[1:2, :] + pltpu.roll(u * cw[0:1, :], 1, 0)
        c = cb + u * cw[2:3, :] + pltpu.roll(inner, 1, 0)
        return c[FFN_HALO:, :]

    n_chunks = wug_ref.shape[0]
    fc = wug_ref.shape[2]
    split = (n_chunks + 1) // 2
    out = zt
    u_next = up(0)
    for j in range(n_chunks):
        ug, uv = u_next
        if j + 1 < n_chunks:
            u_next = up(j + 1)
        cg = conv(ug, cwg_ref[j], cbg_ref[j])
        cv = conv(uv, cwv_ref[j], cbv_ref[j])
        act_ref[:, j * fc:(j + 1) * fc] = (cg * cv / (1.0 + jnp.exp(-cg))).astype(BF16)
        if j + 1 in (split, n_chunks):
            lo = 0 if j + 1 == split else split
            out = out + jnp.dot(act_ref[:, lo * fc:(j + 1) * fc], wd_ref[lo * fc:(j + 1) * fc, :],
                                preferred_element_type=F32)
    if final:
        out = _rms(out, fnw_ref[...])
    o_ref[...] = out


def _ffn(z3d, norm_w, wug, wuv, cwg, cwv, cbg, cbv, wd, final_w, tt, final):
    bsz, tp, _ = z3d.shape
    hb = tt // FFN_HALO
    once = lambda a: pl.BlockSpec(a.shape, lambda b, t: (0,) * a.ndim, pipeline_mode=pl.Buffered(1))
    return pl.pallas_call(
        functools.partial(_ffn_kernel, final),
        grid=(bsz, tp // tt),
        in_specs=[pl.BlockSpec((None, tt, D_MODEL), lambda b, t: (b, t, 0)),
                  pl.BlockSpec((None, FFN_HALO, D_MODEL), lambda b, t: (b, jnp.maximum(t * hb - 1, 0), 0)),
                  once(norm_w), once(wug), once(wuv), once(cwg), once(cwv), once(cbg), once(cbv), once(wd),
                  once(final_w)],
        out_specs=pl.BlockSpec((None, tt, D_MODEL), lambda b, t: (b, t, 0)),
        out_shape=jax.ShapeDtypeStruct((bsz, tp, D_MODEL), F32),
        scratch_shapes=[pltpu.VMEM((tt, D_FF), BF16)],
        compiler_params=_seq_params(),
        name="conv_ffn",
    )(z3d, z3d, norm_w, wug, wuv, cwg, cwv, cbg, cbv, wd, final_w)


def _largest_tile(n, cands):
    for c in cands:
        if n % c == 0:
            return c
    raise ValueError(f"no tile for {n}")


def _ref_cols(w, name):
    off, width = _REF_OFF[name]
    return w[..., off:off + width]


def _pack_columns(w, vres):
    lead = w.shape[:-1]
    zeros = lambda n: jnp.zeros(lead + (n,), w.dtype)
    vr = vres if vres is not None else zeros(RW_V_RANK)
    parts = []
    for name, width in _K_LAYOUT:
        if name == "rw_wa":
            parts.append(jnp.concatenate([_ref_cols(w, "rw_w"), _ref_cols(w, "rw_a")], axis=-1))
        elif name == "misc":
            parts.append(jnp.concatenate(
                [_ref_cols(w, "gla_gk"), vr, zeros(width - GLA_GATE_RANK - RW_V_RANK)], axis=-1))
        else:
            parts.append(_ref_cols(w, name))
    used = sum(wd for _, wd in _K_LAYOUT)
    parts.append(zeros(WP - used))
    return jnp.concatenate(parts, axis=-1)


def _pad_rows(w, lo, total):
    return jnp.pad(w, ((lo, total - lo - w.shape[0]), (0, 0)))


def kernel(x, meta, mix_norm, w_in, w_in_vres, rw_mu, rw_mu_vres, gla_gk_up, gla_gk_bias, gla_norm, rw_w0,
           rw_w2, rw_a0, rw_a2, rw_v0, rw_v2, rw_g2, rw_kk, rw_ka, rw_rk, rw_ln_w, rw_ln_b, hg_lb_logits,
           hg_norm, w_out_gla, w_out_rw, w_out_hg, w_out, ffn_norm, w_up, conv_w, conv_b, w_down, final_norm):
    bsz, seq, _ = x.shape
    depth = w_in.shape[0]
    t_real = seq + N_META
    tp = -(-t_real // CHUNK) * CHUNK
    n = bsz * tp
    tm = _largest_tile(n, (1024, 512, 256, 128, 64))
    tm_merge = _largest_tile(n, (512, 256, 128, 64))
    tt = _largest_tile(tp, (704, 512, 384, 256, 192, 128, 64))
    tt_mix = _largest_tile(tp, (3 * CHUNK, 2 * CHUNK, CHUNK))

    z = jnp.concatenate([jnp.broadcast_to(meta.astype(x.dtype)[None], (bsz, N_META, D_MODEL)), x,
                         jnp.zeros((bsz, tp - t_real, D_MODEL), x.dtype)], axis=1)

    row = lambda a: a.reshape(1, -1).astype(F32)
    t_idx = jnp.arange(tt_mix)
    tril = ((t_idx[:, None] // CHUNK == t_idx[None, :] // CHUNK)
            & (t_idx[None, :] <= t_idx[:, None])).astype(BF16)
    head_of = jnp.arange(RW_DIM) // RW_HD
    seg = (head_of[:, None] == head_of[None, :]).astype(BF16)
    lb_p = jax.nn.softmax(hg_lb_logits.astype(F32), axis=0)
    lb_all = jnp.cumsum(lb_p, axis=0) - lb_p[0:1]

    nj = D_FF // FFN_FC
    v_first = None
    for i in range(depth):
        vres_w = w_in_vres[i - 1] if i > 0 else None
        vres_mu = rw_mu_vres[i - 1] if i > 0 else None
        w_cat = _pack_columns(w_in[i], vres_w).astype(BF16)
        w_cat = w_cat.reshape(D_MODEL, N_CT, IN_TN).transpose(1, 0, 2)
        mu_full = jnp.concatenate([jnp.zeros((W_IN - rw_mu.shape[1],), F32), rw_mu[i].astype(F32)])
        mu_cat = _pack_columns(mu_full, vres_mu).reshape(N_CT, 1, IN_TN)
        mus = [mu_cat[_K_OFF[nm][0] // IN_TN] for nm in ("rw_r", "rw_k", "rw_v", "rw_g")]

        z2d = z.reshape(n, D_MODEL)
        p = _inproj(z2d, row(mix_norm[i]), w_cat, tm)

        gk_up_pad = _pad_rows(gla_gk_up[i], MISC_GK, LANE).astype(BF16)
        y_gla = _gla_mixer(p, gk_up_pad, row(gla_gk_bias[i]), row(gla_norm[i]), tril, bsz, tp, tt_mix)
        y_hg = _hgrn_mixer(p, row(lb_all[i]), row(hg_norm[i]), tril, bsz, tp, tt_mix)

        prm = {
            "w0": row(rw_w0[i]), "w2": _pad_rows(rw_w2[i], 0, LANE).astype(BF16),
            "a0": row(rw_a0[i]), "a2": _pad_rows(rw_a2[i], RW_W_RANK, LANE).astype(BF16),
            "g2": rw_g2[i].astype(BF16),
            "kk": row(rw_kk[i]), "ka": row(rw_ka[i]), "rk": row(rw_rk[i]),
            "lnw": row(rw_ln_w[i]), "lnb": row(rw_ln_b[i]),
        }
        if i > 0:
            prm["v0"] = row(rw_v0[i - 1])
            prm["v2"] = _pad_rows(rw_v2[i - 1], MISC_VR, LANE).astype(BF16)
        y_rw, v_first = _rwkv_mixer(p, v_first, mus, prm, seg, tril, bsz, tp, tt_mix)

        z2d = _merge(z2d, p, y_gla, y_rw, y_hg, w_out_gla[i].astype(BF16), w_out_rw[i].astype(BF16),
                     w_out_hg[i].astype(BF16), w_out[i].astype(BF16), tm_merge)

        chunked = lambda a: a.reshape(a.shape[0], nj, FFN_FC).transpose(1, 0, 2)
        wug = chunked(w_up[i][:, :D_FF]).astype(BF16)
        wuv = chunked(w_up[i][:, D_FF:]).astype(BF16)
        cwg = jnp.pad(chunked(conv_w[i][:, :D_FF]), ((0, 0), (0, SUBLANE - CONV_W), (0, 0))).astype(F32)
        cwv = jnp.pad(chunked(conv_w[i][:, D_FF:]), ((0, 0), (0, SUBLANE - CONV_W), (0, 0))).astype(F32)
        cbg = chunked(conv_b[i][None, :D_FF]).astype(F32)
        cbv = chunked(conv_b[i][None, D_FF:]).astype(F32)
        wd = w_down[i].astype(BF16)
        z = _ffn(z2d.reshape(bsz, tp, D_MODEL), row(ffn_norm[i]), wug, wuv, cwg, cwv, cbg, cbv, wd,
                 row(final_norm), tt, final=(i == depth - 1))
    return z[:, N_META:t_real]
```

```python
import functools

import jax
import jax.numpy as jnp
from jax import lax
from jax.experimental import pallas as pl
from jax.experimental.pallas import tpu as pltpu

F32 = jnp.float32
BF16 = jnp.bfloat16

D_MODEL = 1024
N_META = 16
F_TINY = 1e-30
NORM_EPS = 1e-6

GLA_HEADS, GLA_DK, GLA_DV = 4, 64, 128
GLA_K, GLA_V = GLA_HEADS * GLA_DK, GLA_HEADS * GLA_DV
GLA_GATE_RANK = 16
GLA_GATE_NORM = 16.0

RW_HEADS, RW_HD = 8, 64
RW_DIM = RW_HEADS * RW_HD
RW_W_RANK, RW_A_RANK, RW_V_RANK, RW_G_RANK = 64, 64, 32, 128
RW_GN_EPS = 64e-5
RW_GROUP_HEADS = 4
RW_GROUP = RW_GROUP_HEADS * RW_HD

HG_HEADS, HG_DK, HG_DV = 4, 128, 128
HG_K, HG_V = HG_HEADS * HG_DK, HG_HEADS * HG_DV

D_FF = 2816
CONV_W = 3

_REF_LAYOUT = (
    ("gla_q", GLA_K), ("gla_k", GLA_K), ("gla_v", GLA_V), ("gla_gk", GLA_GATE_RANK), ("gla_g", GLA_V),
    ("hg_q", HG_K), ("hg_f", HG_K), ("hg_i", HG_V), ("hg_g", HG_V),
    ("gate_gla", D_MODEL), ("gate_rw", D_MODEL), ("gate_hg", D_MODEL),
    ("rw_r", RW_DIM), ("rw_w", RW_W_RANK), ("rw_k", RW_DIM), ("rw_v", RW_DIM), ("rw_a", RW_A_RANK),
    ("rw_g", RW_G_RANK),
)
_REF_OFF = {}
_o = 0
for _n, _w in _REF_LAYOUT:
    _REF_OFF[_n] = (_o, _w)
    _o += _w
W_IN = _o

LANE = 128
SUBLANE = 8
_K_LAYOUT = (
    ("gate_gla", 1024), ("gate_rw", 1024), ("gate_hg", 1024),
    ("gla_v", 512), ("gla_g", 512), ("hg_q", 512), ("hg_f", 512), ("hg_i", 512), ("hg_g", 512),
    ("rw_r", 512), ("rw_k", 512), ("rw_v", 512),
    ("gla_q", 256), ("gla_k", 256),
    ("rw_g", 128), ("rw_wa", 128), ("misc", 128),
)
_K_OFF = {}
_o = 0
for _n, _w in _K_LAYOUT:
    assert _o % _w == 0
    _K_OFF[_n] = (_o, _w)
    _o += _w
IN_TN = 512
WP = -(-_o // IN_TN) * IN_TN
N_CT = WP // IN_TN
CT_QK = _K_OFF["gla_q"][0] // IN_TN
CT_SMALL = _K_OFF["rw_g"][0] // IN_TN
assert _K_OFF["gla_k"][0] // IN_TN == CT_QK and _K_OFF["misc"][0] // IN_TN == CT_SMALL
SMALL_G, SMALL_WA, SMALL_MISC = (_K_OFF[_n][0] - CT_SMALL * IN_TN for _n in ("rw_g", "rw_wa", "misc"))
MISC_GK = 0
MISC_VR = GLA_GATE_RANK

CHUNK = 64
SUB = 32
VMEM_LIMIT = 56 * 1024 * 1024


def _mm(a, b):
    return jnp.dot(a.astype(BF16), b.astype(BF16), preferred_element_type=F32)


def _mm_nt(a, b):
    return lax.dot_general(a.astype(BF16), b.astype(BF16), (((1,), (1,)), ((), ())),
                           preferred_element_type=F32)


def _mm_tn(a, b):
    return lax.dot_general(a.astype(BF16), b.astype(BF16), (((0,), (0,)), ((), ())),
                           preferred_element_type=F32)


def _split2(x):
    hi = x.astype(BF16)
    lo = (x - hi.astype(F32)).astype(BF16)
    return hi, lo


def _mm_exact_rhs(a_f32, b_bf16):
    m = a_f32.shape[0]
    out = jnp.dot(jnp.concatenate(_split2(a_f32), axis=0), b_bf16, preferred_element_type=F32)
    return out[:m] + out[m:]


def _cumsum_rows(tril_bf16, g):
    w = g.shape[1]
    out = jnp.dot(tril_bf16, jnp.concatenate(_split2(g), axis=1), preferred_element_type=F32)
    return out[:, :w] + out[:, w:]


def _sigmoid(x):
    return 1.0 / (1.0 + jnp.exp(-x))


def _softplus(x):
    return jnp.maximum(x, 0.0) + jnp.log(1.0 + jnp.exp(-jnp.abs(x)))


def _rms(x, w):
    return x * lax.rsqrt(jnp.mean(x * x, axis=-1, keepdims=True) + NORM_EPS) * w


def _stack_heads(x, heads, width):
    lane = lax.broadcasted_iota(jnp.int32, x.shape, 1)
    return jnp.concatenate(
        [jnp.where((lane >= h * width) & (lane < (h + 1) * width), x, 0.0) for h in range(heads)], axis=0)


def _inproj_kernel(z_ref, nw_ref, w_ref, o_ref, h_ref):
    @pl.when(pl.program_id(1) == 0)
    def _():
        h_ref[...] = _rms(z_ref[...], nw_ref[...]).astype(BF16)

    o_ref[...] = jnp.dot(h_ref[...], w_ref[...], preferred_element_type=F32).astype(o_ref.dtype)


def _inproj(z2d, norm_w, w_bf16, tm):
    n = z2d.shape[0]
    return pl.pallas_call(
        _inproj_kernel,
        grid=(n // tm, N_CT),
        in_specs=[
            pl.BlockSpec((tm, D_MODEL), lambda i, j: (i, 0)),
            pl.BlockSpec((1, D_MODEL), lambda i, j: (0, 0)),
            pl.BlockSpec((None, D_MODEL, IN_TN), lambda i, j: (j, 0, 0)),
        ],
        out_specs=pl.BlockSpec((None, tm, IN_TN), lambda i, j: (j, i, 0)),
        out_shape=jax.ShapeDtypeStruct((N_CT, n, IN_TN), BF16),
        scratch_shapes=[pltpu.VMEM((tm, D_MODEL), BF16)],
        compiler_params=pltpu.CompilerParams(
            dimension_semantics=("arbitrary", "arbitrary"), vmem_limit_bytes=VMEM_LIMIT),
        name="inproj",
    )(z2d, norm_w, w_bf16)


def _gla_tile(q, k, v, g, tril_bd, s_ref, norm_w, heads, dk, first):
    tt, hk = q.shape
    assert tt % CHUNK == 0 and CHUNK == 2 * SUB
    dv = v.shape[1] // heads

    @pl.when(first)
    def _():
        s_ref[...] = jnp.zeros_like(s_ref)

    c_all = _cumsum_rows(tril_bd, g)
    blk1 = lax.broadcasted_iota(jnp.int32, (CHUNK, hk), 0) >= SUB
    st = functools.partial(_stack_heads, heads=heads, width=dk)
    n_st = heads * CHUNK
    causal = (lax.broadcasted_iota(jnp.int32, (n_st, n_st), 1)
              <= lax.broadcasted_iota(jnp.int32, (n_st, n_st), 0))
    chunks = []
    for j in range(tt // CHUNK):
        rows = slice(j * CHUNK, (j + 1) * CHUNK)
        c, qj, kj, vj = c_all[rows], q[rows], k[rows], v[rows]
        r0 = c[SUB // 2 - 1:SUB // 2, :]
        r1 = c[SUB + SUB // 2 - 1:SUB + SUB // 2, :]
        c_last = c[CHUNK - 1:CHUNK, :]
        kt = kj * jnp.exp(jnp.where(blk1, r1, r0) - c)
        qa = qj * jnp.exp(c - r0)
        qb = jnp.where(blk1, qj * jnp.exp(jnp.where(blk1, c - r1, 0.0)), 0.0)
        kt0 = jnp.where(blk1, 0.0, kt)
        kt1 = jnp.where(blk1, kt, 0.0)
        qs = jnp.concatenate([st(qa), st(qb)], axis=1).astype(BF16)
        ks = jnp.concatenate([st(kt0), st(kt1)], axis=1).astype(BF16)
        vs = jnp.concatenate([vj[:, h * dv:(h + 1) * dv] for h in range(heads)], axis=0).astype(BF16)
        chunks.append(dict(qs=qs, ks=ks, vs=vs, qe=st(qj * jnp.exp(c)).astype(BF16),
                           kd=st(kj * jnp.exp(c_last - c)).astype(BF16), dec=jnp.exp(c_last)))
    for ch in chunks:
        ch["a"] = jnp.where(causal, _mm_nt(ch["qs"], ch["ks"]), 0.0)
    for ch in chunks:
        ch["oa"] = _mm(ch["a"], ch["vs"])
        ch["kv"] = _mm_tn(ch["vs"], ch["kd"])
    s_t = s_ref[...]
    outs = []
    for ch in chunks:
        o = ch["oa"] + _mm_nt(ch["qe"], s_t)
        s_t = s_t * ch["dec"] + ch["kv"]
        o = o * lax.rsqrt(jnp.mean(o * o, axis=-1, keepdims=True) + NORM_EPS) * norm_w
        outs.append(jnp.concatenate([o[h * CHUNK:(h + 1) * CHUNK] for h in range(heads)], axis=1))
    s_ref[...] = s_t
    return jnp.concatenate(outs, axis=0)


def _silu(x):
    return x * _sigmoid(x)


def _gla_kernel(qk_ref, v_ref, small_ref, og_ref, gkup_ref, gkb_ref, nw_ref, tril_ref, y_ref, s_ref):
    qk = qk_ref[...].astype(F32)
    q = qk[:, :GLA_K] * (GLA_DK ** -0.5)
    x = _mm(small_ref[:, SMALL_MISC:SMALL_MISC + LANE], gkup_ref[...]) + gkb_ref[...]
    g = -_softplus(-x) * (1.0 / GLA_GATE_NORM)
    o = _gla_tile(q, qk[:, GLA_K:], v_ref[...].astype(F32), g, tril_ref[...], s_ref, nw_ref[...],
                  GLA_HEADS, GLA_DK, pl.program_id(1) == 0)
    y_ref[...] = (o * _silu(og_ref[...].astype(F32))).astype(y_ref.dtype)


def _hgrn_kernel(q_ref, f_ref, i_ref, og_ref, lb_ref, nw_ref, tril_ref, y_ref, s_ref):
    z = f_ref[...].astype(F32)
    lb = lb_ref[...]
    forget = lb + (1.0 - lb) * _sigmoid(z)
    g = jnp.log(jnp.maximum(forget, F_TINY))
    k = (1.0 - lb) * _sigmoid(-z)
    o = _gla_tile(q_ref[...].astype(F32), k, i_ref[...].astype(F32), g, tril_ref[...], s_ref, nw_ref[...],
                  HG_HEADS, HG_DK, pl.program_id(1) == 0)
    y_ref[...] = (o * _silu(og_ref[...].astype(F32))).astype(y_ref.dtype)


def _pcol(name, tp, tt):
    ct = (name if isinstance(name, int) else _K_OFF[name][0] // IN_TN)
    nt = tp // tt
    return pl.BlockSpec((None, tt, IN_TN), lambda b, t: (ct, b * nt + t, 0))


def _const_spec(shape):
    nd = len(shape)
    return pl.BlockSpec(shape, lambda b, t: (0,) * nd)


def _seq_params():
    return pltpu.CompilerParams(dimension_semantics=("arbitrary", "arbitrary"), vmem_limit_bytes=VMEM_LIMIT)


def _gla_mixer(p, gk_up_pad, gk_bias, norm_w, tril, bsz, tp, tt):
    n = p.shape[1]
    col = functools.partial(_pcol, tp=tp, tt=tt)
    return pl.pallas_call(
        _gla_kernel,
        grid=(bsz, tp // tt),
        in_specs=[col(CT_QK), col("gla_v"), col(CT_SMALL), col("gla_g"),
                  _const_spec(gk_up_pad.shape), _const_spec(gk_bias.shape),
                  _const_spec(norm_w.shape), _const_spec(tril.shape)],
        out_specs=pl.BlockSpec((tt, GLA_V), lambda b, t: (b * (tp // tt) + t, 0)),
        out_shape=jax.ShapeDtypeStruct((n, GLA_V), BF16),
        scratch_shapes=[pltpu.VMEM((GLA_DV, GLA_K), F32)],
        compiler_params=_seq_params(),
        name="gla_mixer",
    )(p, p, p, p, gk_up_pad, gk_bias, norm_w, tril)


def _hgrn_mixer(p, lb, norm_w, tril, bsz, tp, tt):
    n = p.shape[1]
    col = functools.partial(_pcol, tp=tp, tt=tt)
    return pl.pallas_call(
        _hgrn_kernel,
        grid=(bsz, tp // tt),
        in_specs=[col("hg_q"), col("hg_f"), col("hg_i"), col("hg_g"),
                  _const_spec(lb.shape), _const_spec(norm_w.shape), _const_spec(tril.shape)],
        out_specs=pl.BlockSpec((tt, HG_V), lambda b, t: (b * (tp // tt) + t, 0)),
        out_shape=jax.ShapeDtypeStruct((n, HG_V), BF16),
        scratch_shapes=[pltpu.VMEM((HG_DV, HG_K), F32)],
        compiler_params=_seq_params(),
        name="hgrn_mixer",
    )(p, p, p, p, lb, norm_w, tril)


def _unit_lower_inverses(n_list, blk16, blk32):
    size = n_list[0].shape[0]
    ri = lax.broadcasted_iota(jnp.int32, (size, size), 0)
    ci = lax.broadcasted_iota(jnp.int32, (size, size), 1)
    eye = jnp.where(ri == ci, 1.0, 0.0)
    n16 = [jnp.where(blk16, n, 0.0) for n in n_list]
    n32 = [jnp.where(blk32 & jnp.logical_not(blk16), n, 0.0).astype(BF16) for n in n_list]
    n64 = [jnp.where(blk32, 0.0, n).astype(BF16) for n in n_list]
    t = [eye + n for n in n16]
    m = [n.astype(BF16) for n in n16]
    for it in range(3):
        m = [_mm(x, x).astype(BF16) for x in m]
        t = [ti + _mm(mi, ti) for mi, ti in zip(m, t)]
    for off_diag in (n32, n64):
        tb = [ti.astype(BF16) for ti in t]
        w = [_mm(ti, ni).astype(BF16) for ti, ni in zip(tb, off_diag)]
        t = [ti + _mm(wi, tbi) for ti, wi, tbi in zip(t, w, tb)]
    return t


def _rwkv_kernel(has_vres, *refs):
    if has_vres:
        (r_ref, k_ref, v_ref, small_ref, vf_ref,
         mu_r_ref, mu_k_ref, mu_v_ref, mu_small_ref,
         w0_ref, w2_ref, a0_ref, a2_ref, g2_ref, v0_ref, v2_ref,
         kk_ref, ka_ref, rk_ref, lnw_ref, lnb_ref, seg_ref, tril_ref,
         y_ref, s_ref, carry_ref) = refs
    else:
        (r_ref, k_ref, v_ref, small_ref,
         mu_r_ref, mu_k_ref, mu_v_ref, mu_small_ref,
         w0_ref, w2_ref, a0_ref, a2_ref, g2_ref,
         kk_ref, ka_ref, rk_ref, lnw_ref, lnb_ref, seg_ref, tril_ref,
         y_ref, vf_out_ref, s_ref, carry_ref) = refs

    first = pl.program_id(1) == 0
    nb, tt = r_ref.shape[0], r_ref.shape[1]

    @pl.when(first)
    def _():
        s_ref[...] = jnp.zeros_like(s_ref)
        carry_ref[...] = jnp.zeros_like(carry_ref)

    seg = seg_ref[...]
    tiles = []
    for bi in range(nb):
        srcs = (r_ref, k_ref, v_ref, small_ref)
        mus = (mu_r_ref, mu_k_ref, mu_v_ref, mu_small_ref)
        shifted = []
        off = 0
        crow = bi * SUBLANE
        for src, mu in zip(srcs, mus):
            p = src[bi].astype(F32)
            width = p.shape[1]
            prev_row = carry_ref[crow:crow + 1, off:off + width]
            rolled = pltpu.roll(p, 1, 0)
            row = lax.broadcasted_iota(jnp.int32, p.shape, 0)
            prev = jnp.where(row == 0, prev_row, rolled)
            shifted.append(p + (prev - p) * mu[...])
            carry_ref[crow:crow + 1, off:off + width] = p[tt - 1:tt, :]
            off += width
        r, k, v, small = shifted
        s_g = small[:, SMALL_G:SMALL_G + LANE]
        s_wa = small[:, SMALL_WA:SMALL_WA + LANE]
        s_misc = small[:, SMALL_MISC:SMALL_MISC + LANE]

        w_log = -_softplus(-(w0_ref[...] + _mm(jnp.tanh(s_wa), w2_ref[...]))) - 0.5
        lw = -jnp.exp(w_log)
        a = _sigmoid(a0_ref[...] + _mm(s_wa, a2_ref[...]))
        g = _mm(_sigmoid(s_g), g2_ref[...])
        if has_vres:
            v = v + (vf_ref[bi] - v) * _sigmoid(v0_ref[...] + _mm(s_misc, v2_ref[...]))
        else:
            vf_out_ref[bi] = v
        kk = k * kk_ref[...]
        k = k * (1.0 + (a - 1.0) * ka_ref[...])
        sums = _mm_exact_rhs(jnp.concatenate([kk * kk, r * k * rk_ref[...]], axis=0), seg)
        kk = kk / jnp.maximum(jnp.sqrt(sums[:tt]), 1e-12)
        c_all = _cumsum_rows(tril_ref[...], lw)
        tiles.append(dict(r=r, k=k, v=v, g=g, bonus=sums[tt:] * v, alpha=-kk, beta=kk * a,
                          c=c_all, ce=c_all - lw))

    gs = RW_GROUP
    n_groups = RW_HEADS // RW_GROUP_HEADS
    n_st = RW_GROUP_HEADS * CHUNK
    ri = lax.broadcasted_iota(jnp.int32, (n_st, n_st), 0)
    ci = lax.broadcasted_iota(jnp.int32, (n_st, n_st), 1)
    strict = ci < ri
    blk16 = (ri >> 4) == (ci >> 4)
    blk32 = (ri >> 5) == (ci >> 5)
    tcol = lax.broadcasted_iota(jnp.int32, (CHUNK, n_st), 1) & (CHUNK - 1)
    trow = lax.broadcasted_iota(jnp.int32, (CHUNK, n_st), 0)
    incl = tcol <= trow
    st = functools.partial(_stack_heads, heads=RW_GROUP_HEADS, width=RW_HD)

    blocks = []
    for bi, j in [(bi, j) for j in range(tt // CHUNK) for bi in range(nb)]:
        rows = slice(j * CHUNK, (j + 1) * CHUNK)
        tl = tiles[bi]
        c, ce = tl["c"][rows], tl["ce"][rows]
        rj, kj, vj, al, be_ = tl["r"][rows], tl["k"][rows], tl["v"][rows], tl["alpha"][rows], tl["beta"][rows]
        mid = c[CHUNK // 2 - 1:CHUNK // 2, :]
        c_last = c[CHUNK - 1:CHUNK, :]
        e_out = jnp.exp(mid - c)
        e_end = jnp.exp(c_last - c)
        at = al * jnp.exp(ce - mid)
        rt = rj * jnp.exp(c - mid)
        bh = be_ * e_out
        kh = kj * e_out
        ae = al * jnp.exp(ce)
        rs = rj * jnp.exp(c)
        be = be_ * e_end
        ke = kj * e_end
        dec = jnp.exp(c_last)
        for gi in range(n_groups):
            sl = slice(gi * gs, (gi + 1) * gs)
            stb = lambda a_: st(a_[:, sl].astype(BF16))
            blocks.append(dict(j=j, si=bi * n_groups + gi, bm=stb(bh), km=stb(kh), am=stb(at), v_bd=stb(vj),
                               aem=stb(ae),
                               bem=stb(be), kem=stb(ke), rt=rt[:, sl].astype(BF16),
                               rs=rs[:, sl].astype(BF16), dec=dec[:, sl]))
    for b in blocks:
        b["n_ab"] = jnp.where(strict, _mm_nt(b["am"], b["bm"]), 0.0)
    for b in blocks:
        b["n_ak"] = jnp.where(strict, _mm_nt(b["am"], b["km"]), 0.0).astype(BF16)
    for b in blocks:
        b["a_rb"] = jnp.where(incl, _mm_nt(b["rt"], b["bm"]), 0.0).astype(BF16)
        b["a_rk"] = jnp.where(incl, _mm_nt(b["rt"], b["km"]), 0.0).astype(BF16)
    for b, t_inv in zip(blocks, _unit_lower_inverses([b["n_ab"] for b in blocks], blk16, blk32)):
        b["t_inv"] = t_inv.astype(BF16)
    for b in blocks:
        b["w1"] = _mm(b["n_ak"], b["v_bd"])
        b["y0"] = _mm(b["a_rk"], b["v_bd"])
        b["vk"] = _mm_tn(b["v_bd"], b["kem"])

    n_state = nb * n_groups
    state = [s_ref[si] for si in range(n_state)]
    y_rows = [[] for _ in range(nb)]
    for j in range(tt // CHUNK):
        cur = [b for b in blocks if b["j"] == j]
        sb = [state[b["si"]].astype(BF16) for b in cur]
        xs = [_mm_nt(b["aem"], s) + b["w1"] for b, s in zip(cur, sb)]
        us = [_mm(b["t_inv"], x).astype(BF16) for b, x in zip(cur, xs)]
        for b, u in zip(cur, us):
            state[b["si"]] = state[b["si"]] * b["dec"] + _mm_tn(u, b["bem"]) + b["vk"]
        ys = [_mm_nt(b["rs"], s) + _mm(b["a_rb"], u) + b["y0"] for b, s, u in zip(cur, sb, us)]
        for bi in range(nb):
            y_rows[bi].append(jnp.concatenate(ys[bi * n_groups:(bi + 1) * n_groups], axis=1))
    for si in range(n_state):
        s_ref[si] = state[si]

    inv_hd = 1.0 / RW_HD
    for bi in range(nb):
        y = jnp.concatenate(y_rows[bi], axis=0)
        mean = _mm_exact_rhs(y, seg) * inv_hd
        yc = y - mean
        var = _mm_exact_rhs(yc * yc, seg) * inv_hd
        y = yc * lax.rsqrt(var + RW_GN_EPS) * lnw_ref[...] + lnb_ref[...]
        y_ref[bi] = ((y + tiles[bi]["bonus"]) * tiles[bi]["g"]).astype(y_ref.dtype)


def _rwkv_mixer(p, v_first, mus, prm, seg, tril, bsz, tp, tt, nb):
    nt = tp // tt
    p = p.reshape(N_CT, bsz, tp, IN_TN)
    has_vres = v_first is not None
    row_spec = pl.BlockSpec((nb, tt, RW_DIM), lambda b, t: (b, t, 0))
    col = lambda name: pl.BlockSpec(
        (None, nb, tt, IN_TN),
        lambda b, t, ct=(name if isinstance(name, int) else _K_OFF[name][0] // IN_TN): (ct, b, t, 0))
    in_specs = [col("rw_r"), col("rw_k"), col("rw_v"), col(CT_SMALL)]
    args = [p, p, p, p]
    if has_vres:
        in_specs.append(row_spec)
        args.append(v_first)
    consts = list(mus) + [prm["w0"], prm["w2"], prm["a0"], prm["a2"], prm["g2"]]
    if has_vres:
        consts += [prm["v0"], prm["v2"]]
    consts += [prm["kk"], prm["ka"], prm["rk"], prm["lnw"], prm["lnb"], seg, tril]
    in_specs += [_const_spec(c.shape) for c in consts]
    args += consts
    out_shape = [jax.ShapeDtypeStruct((bsz, tp, RW_DIM), BF16)]
    out_specs = [row_spec]
    if not has_vres:
        out_shape.append(jax.ShapeDtypeStruct((bsz, tp, RW_DIM), F32))
        out_specs.append(row_spec)
    carry_w = 3 * RW_DIM + IN_TN
    res = pl.pallas_call(
        functools.partial(_rwkv_kernel, has_vres),
        grid=(bsz // nb, nt),
        in_specs=in_specs,
        out_specs=out_specs,
        out_shape=out_shape,
        scratch_shapes=[pltpu.VMEM((nb * (RW_HEADS // RW_GROUP_HEADS), RW_GROUP, RW_GROUP), F32),
                        pltpu.VMEM((nb * SUBLANE, carry_w), F32)],
        compiler_params=_seq_params(),
        name="rwkv_mixer",
    )(*args)
    y = res[0].reshape(bsz * tp, RW_DIM)
    return (y, v_first) if has_vres else (y, res[1])


def _merge_kernel(z_ref, gg_ref, gr_ref, gh_ref, yg_ref, yr_ref, yh_ref, wa_ref, wb_ref, wc_ref, wo_ref, o_ref):
    d = lambda y, w: jnp.dot(y[...], w[...], preferred_element_type=F32)
    sg = lambda g: _sigmoid(jnp.concatenate([g[t] for t in range(g.shape[0])], axis=1).astype(F32))
    m = sg(gg_ref) * d(yg_ref, wa_ref) + sg(gr_ref) * d(yr_ref, wb_ref) + sg(gh_ref) * d(yh_ref, wc_ref)
    o_ref[...] = z_ref[...] + _mm(m, wo_ref[...])


def _merge(z2d, p, y_gla, y_rw, y_hg, wa, wb, wc, wo, tm):
    n = z2d.shape[0]
    row = lambda w: pl.BlockSpec((tm, w), lambda i: (i, 0))
    gate = lambda name: pl.BlockSpec((D_MODEL // IN_TN, tm, IN_TN),
                                     lambda i, cb=_K_OFF[name][0] // D_MODEL: (cb, i, 0))
    full = lambda a: pl.BlockSpec(a.shape, lambda i: (0, 0))
    return pl.pallas_call(
        _merge_kernel,
        grid=(n // tm,),
        in_specs=[row(D_MODEL), gate("gate_gla"), gate("gate_rw"), gate("gate_hg"),
                  row(GLA_V), row(RW_DIM), row(HG_V), full(wa), full(wb), full(wc), full(wo)],
        out_specs=row(D_MODEL),
        out_shape=jax.ShapeDtypeStruct((n, D_MODEL), F32),
        compiler_params=pltpu.CompilerParams(dimension_semantics=("arbitrary",), vmem_limit_bytes=VMEM_LIMIT),
        name="merge",
    )(z2d, p, p, p, y_gla, y_rw, y_hg, wa, wb, wc, wo)


FFN_FC = 256
FFN_HALO = SUBLANE


def _ffn_kernel(final, z_ref, halo_ref, nw_ref, wug_ref, wuv_ref, cwg_ref, cwv_ref, cbg_ref, cbv_ref, wd_ref,
                fnw_ref, o_ref, act_ref):
    tt = z_ref.shape[0]
    zt = z_ref[...]
    nw = nw_ref[...]
    h = jnp.concatenate([_rms(halo_ref[...], nw), _rms(zt, nw)], axis=0).astype(BF16)
    has_prev = pl.program_id(1) > 0

    def up(j):
        return (jnp.dot(h, wug_ref[j], preferred_element_type=F32),
                jnp.dot(h, wuv_ref[j], preferred_element_type=F32))

    def conv(u, cw, cb):
        u = jnp.concatenate([jnp.where(has_prev, u[:FFN_HALO], 0.0), u[FFN_HALO:]], axis=0)
        inner = u * cw[1:2, :] + pltpu.roll(u * cw[0:1, :], 1, 0)
        c = cb + u * cw[2:3, :] + pltpu.roll(inner, 1, 0)
        return c[FFN_HALO:, :]

    n_chunks = wug_ref.shape[0]
    fc = wug_ref.shape[2]
    split = (n_chunks + 1) // 2
    out = zt
    u_next = up(0)
    for j in range(n_chunks):
        ug, uv = u_next
        if j + 1 < n_chunks:
            u_next = up(j + 1)
        cg = conv(ug, cwg_ref[j], cbg_ref[j])
        cv = conv(uv, cwv_ref[j], cbv_ref[j])
        act_ref[:, j * fc:(j + 1) * fc] = (cg * cv / (1.0 + jnp.exp(-cg))).astype(BF16)
        if j + 1 in (split, n_chunks):
            lo = 0 if j + 1 == split else split
            out = out + jnp.dot(act_ref[:, lo * fc:(j + 1) * fc], wd_ref[lo * fc:(j + 1) * fc, :],
                                preferred_element_type=F32)
    if final:
        out = _rms(out, fnw_ref[...])
    o_ref[...] = out


def _ffn(z3d, norm_w, wug, wuv, cwg, cwv, cbg, cbv, wd, final_w, tt, final):
    bsz, tp, _ = z3d.shape
    hb = tt // FFN_HALO
    once = lambda a: pl.BlockSpec(a.shape, lambda b, t: (0,) * a.ndim, pipeline_mode=pl.Buffered(1))
    return pl.pallas_call(
        functools.partial(_ffn_kernel, final),
        grid=(bsz, tp // tt),
        in_specs=[pl.BlockSpec((None, tt, D_MODEL), lambda b, t: (b, t, 0)),
                  pl.BlockSpec((None, FFN_HALO, D_MODEL), lambda b, t: (b, jnp.maximum(t * hb - 1, 0), 0)),
                  once(norm_w), once(wug), once(wuv), once(cwg), once(cwv), once(cbg), once(cbv), once(wd),
                  once(final_w)],
        out_specs=pl.BlockSpec((None, tt, D_MODEL), lambda b, t: (b, t, 0)),
        out_shape=jax.ShapeDtypeStruct((bsz, tp, D_MODEL), F32),
        scratch_shapes=[pltpu.VMEM((tt, D_FF), BF16)],
        compiler_params=_seq_params(),
        name="conv_ffn",
    )(z3d, z3d, norm_w, wug, wuv, cwg, cwv, cbg, cbv, wd, final_w)


def _largest_tile(n, cands):
    for c in cands:
        if n % c == 0:
            return c
    raise ValueError(f"no tile for {n}")


def _ref_cols(w, name):
    off, width = _REF_OFF[name]
    return w[..., off:off + width]


def _pack_columns(w, vres):
    lead = w.shape[:-1]
    zeros = lambda n: jnp.zeros(lead + (n,), w.dtype)
    vr = vres if vres is not None else zeros(RW_V_RANK)
    parts = []
    for name, width in _K_LAYOUT:
        if name == "rw_wa":
            parts.append(jnp.concatenate([_ref_cols(w, "rw_w"), _ref_cols(w, "rw_a")], axis=-1))
        elif name == "misc":
            parts.append(jnp.concatenate(
                [_ref_cols(w, "gla_gk"), vr, zeros(width - GLA_GATE_RANK - RW_V_RANK)], axis=-1))
        else:
            parts.append(_ref_cols(w, name))
    used = sum(wd for _, wd in _K_LAYOUT)
    parts.append(zeros(WP - used))
    return jnp.concatenate(parts, axis=-1)


def _pad_rows(w, lo, total):
    return jnp.pad(w, ((lo, total - lo - w.shape[0]), (0, 0)))


def kernel(x, meta, mix_norm, w_in, w_in_vres, rw_mu, rw_mu_vres, gla_gk_up, gla_gk_bias, gla_norm, rw_w0,
           rw_w2, rw_a0, rw_a2, rw_v0, rw_v2, rw_g2, rw_kk, rw_ka, rw_rk, rw_ln_w, rw_ln_b, hg_lb_logits,
           hg_norm, w_out_gla, w_out_rw, w_out_hg, w_out, ffn_norm, w_up, conv_w, conv_b, w_down, final_norm):
    bsz, seq, _ = x.shape
    depth = w_in.shape[0]
    t_real = seq + N_META
    tp = -(-t_real // CHUNK) * CHUNK
    n = bsz * tp
    tm = _largest_tile(n, (1024, 512, 256, 128, 64))
    tm_merge = _largest_tile(n, (512, 256, 128, 64))
    tt = _largest_tile(tp, (704, 512, 384, 256, 192, 128, 64))
    tt_mix = _largest_tile(tp, (3 * CHUNK, 2 * CHUNK, CHUNK))
    nb_mix = _largest_tile(bsz, (4, 2, 1))

    z = jnp.concatenate([jnp.broadcast_to(meta.astype(x.dtype)[None], (bsz, N_META, D_MODEL)), x,
                         jnp.zeros((bsz, tp - t_real, D_MODEL), x.dtype)], axis=1)

    row = lambda a: a.reshape(1, -1).astype(F32)
    t_idx = jnp.arange(tt_mix)
    tril = ((t_idx[:, None] // CHUNK == t_idx[None, :] // CHUNK)
            & (t_idx[None, :] <= t_idx[:, None])).astype(BF16)
    head_of = jnp.arange(RW_DIM) // RW_HD
    seg = (head_of[:, None] == head_of[None, :]).astype(BF16)
    lb_p = jax.nn.softmax(hg_lb_logits.astype(F32), axis=0)
    lb_all = jnp.cumsum(lb_p, axis=0) - lb_p[0:1]

    nj = D_FF // FFN_FC
    v_first = None
    for i in range(depth):
        vres_w = w_in_vres[i - 1] if i > 0 else None
        vres_mu = rw_mu_vres[i - 1] if i > 0 else None
        w_cat = _pack_columns(w_in[i], vres_w).astype(BF16)
        w_cat = w_cat.reshape(D_MODEL, N_CT, IN_TN).transpose(1, 0, 2)
        mu_full = jnp.concatenate([jnp.zeros((W_IN - rw_mu.shape[1],), F32), rw_mu[i].astype(F32)])
        mu_cat = _pack_columns(mu_full, vres_mu).reshape(N_CT, 1, IN_TN)
        mus = [mu_cat[_K_OFF[nm][0] // IN_TN] for nm in ("rw_r", "rw_k", "rw_v", "rw_g")]

        z2d = z.reshape(n, D_MODEL)
        p = _inproj(z2d, row(mix_norm[i]), w_cat, tm)

        gk_up_pad = _pad_rows(gla_gk_up[i], MISC_GK, LANE).astype(BF16)
        y_gla = _gla_mixer(p, gk_up_pad, row(gla_gk_bias[i]), row(gla_norm[i]), tril, bsz, tp, tt_mix)
        y_hg = _hgrn_mixer(p, row(lb_all[i]), row(hg_norm[i]), tril, bsz, tp, tt_mix)

        prm = {
            "w0": row(rw_w0[i]), "w2": _pad_rows(rw_w2[i], 0, LANE).astype(BF16),
            "a0": row(rw_a0[i]), "a2": _pad_rows(rw_a2[i], RW_W_RANK, LANE).astype(BF16),
            "g2": rw_g2[i].astype(BF16),
            "kk": row(rw_kk[i]), "ka": row(rw_ka[i]), "rk": row(rw_rk[i]),
            "lnw": row(rw_ln_w[i]), "lnb": row(rw_ln_b[i]),
        }
        if i > 0:
            prm["v0"] = row(rw_v0[i - 1])
            prm["v2"] = _pad_rows(rw_v2[i - 1], MISC_VR, LANE).astype(BF16)
        y_rw, v_first = _rwkv_mixer(p, v_first, mus, prm, seg, tril, bsz, tp, tt_mix, nb_mix)

        z2d = _merge(z2d, p, y_gla, y_rw, y_hg, w_out_gla[i].astype(BF16), w_out_rw[i].astype(BF16),
                     w_out_hg[i].astype(BF16), w_out[i].astype(BF16), tm_merge)

        chunked = lambda a: a.reshape(a.shape[0], nj, FFN_FC).transpose(1, 0, 2)
        wug = chunked(w_up[i][:, :D_FF]).astype(BF16)
        wuv = chunked(w_up[i][:, D_FF:]).astype(BF16)
        cwg = jnp.pad(chunked(conv_w[i][:, :D_FF]), ((0, 0), (0, SUBLANE - CONV_W), (0, 0))).astype(F32)
        cwv = jnp.pad(chunked(conv_w[i][:, D_FF:]), ((0, 0), (0, SUBLANE - CONV_W), (0, 0))).astype(F32)
        cbg = chunked(conv_b[i][None, :D_FF]).astype(F32)
        cbv = chunked(conv_b[i][None, D_FF:]).astype(F32)
        wd = w_down[i].astype(BF16)
        z = _ffn(z2d.reshape(bsz, tp, D_MODEL), row(ffn_norm[i]), wug, wuv, cwg, cwv, cbg, cbv, wd,
                 row(final_norm), tt, final=(i == depth - 1))
    return z[:, N_META:t_real]
```

```python
import functools

import jax
import jax.numpy as jnp
from jax import lax
from jax.experimental import pallas as pl
from jax.experimental.pallas import tpu as pltpu

F32 = jnp.float32
BF16 = jnp.bfloat16

D_MODEL = 1024
N_META = 16
F_TINY = 1e-30
NORM_EPS = 1e-6

GLA_HEADS, GLA_DK, GLA_DV = 4, 64, 128
GLA_K, GLA_V = GLA_HEADS * GLA_DK, GLA_HEADS * GLA_DV
GLA_GATE_RANK = 16
GLA_GATE_NORM = 16.0

RW_HEADS, RW_HD = 8, 64
RW_DIM = RW_HEADS * RW_HD
RW_W_RANK, RW_A_RANK, RW_V_RANK, RW_G_RANK = 64, 64, 32, 128
RW_GN_EPS = 64e-5
RW_GROUP_HEADS = 4
RW_GROUP = RW_GROUP_HEADS * RW_HD

HG_HEADS, HG_DK, HG_DV = 4, 128, 128
HG_K, HG_V = HG_HEADS * HG_DK, HG_HEADS * HG_DV

D_FF = 2816
CONV_W = 3

_REF_LAYOUT = (
    ("gla_q", GLA_K), ("gla_k", GLA_K), ("gla_v", GLA_V), ("gla_gk", GLA_GATE_RANK), ("gla_g", GLA_V),
    ("hg_q", HG_K), ("hg_f", HG_K), ("hg_i", HG_V), ("hg_g", HG_V),
    ("gate_gla", D_MODEL), ("gate_rw", D_MODEL), ("gate_hg", D_MODEL),
    ("rw_r", RW_DIM), ("rw_w", RW_W_RANK), ("rw_k", RW_DIM), ("rw_v", RW_DIM), ("rw_a", RW_A_RANK),
    ("rw_g", RW_G_RANK),
)
_REF_OFF = {}
_o = 0
for _n, _w in _REF_LAYOUT:
    _REF_OFF[_n] = (_o, _w)
    _o += _w
W_IN = _o

LANE = 128
SUBLANE = 8
_K_LAYOUT = (
    ("gate_gla", 1024), ("gate_rw", 1024), ("gate_hg", 1024),
    ("gla_v", 512), ("gla_g", 512), ("hg_q", 512), ("hg_f", 512), ("hg_i", 512), ("hg_g", 512),
    ("rw_r", 512), ("rw_k", 512), ("rw_v", 512),
    ("gla_q", 256), ("gla_k", 256),
    ("rw_g", 128), ("rw_wa", 128), ("misc", 128),
)
_K_OFF = {}
_o = 0
for _n, _w in _K_LAYOUT:
    assert _o % _w == 0
    _K_OFF[_n] = (_o, _w)
    _o += _w
IN_TN = 512
WP = -(-_o // IN_TN) * IN_TN
N_CT = WP // IN_TN
CT_QK = _K_OFF["gla_q"][0] // IN_TN
CT_SMALL = _K_OFF["rw_g"][0] // IN_TN
assert _K_OFF["gla_k"][0] // IN_TN == CT_QK and _K_OFF["misc"][0] // IN_TN == CT_SMALL
SMALL_G, SMALL_WA, SMALL_MISC = (_K_OFF[_n][0] - CT_SMALL * IN_TN for _n in ("rw_g", "rw_wa", "misc"))
MISC_GK = 0
MISC_VR = GLA_GATE_RANK

CHUNK = 64
SUB = 32
VMEM_LIMIT = 56 * 1024 * 1024


def _mm(a, b):
    return jnp.dot(a.astype(BF16), b.astype(BF16), preferred_element_type=F32)


def _mm_nt(a, b):
    return lax.dot_general(a.astype(BF16), b.astype(BF16), (((1,), (1,)), ((), ())),
                           preferred_element_type=F32)


def _mm_tn(a, b):
    return lax.dot_general(a.astype(BF16), b.astype(BF16), (((0,), (0,)), ((), ())),
                           preferred_element_type=F32)


def _split2(x):
    hi = x.astype(BF16)
    lo = (x - hi.astype(F32)).astype(BF16)
    return hi, lo


def _head_sums(x, seg_group):
    gw = seg_group.shape[0]
    xb = x.astype(BF16)
    return jnp.concatenate([jnp.dot(xb[:, i:i + gw], seg_group, preferred_element_type=F32)
                            for i in range(0, x.shape[1], gw)], axis=1)


def _cumsum_rows(tril_bf16, g):
    w = g.shape[1]
    out = jnp.dot(tril_bf16, jnp.concatenate(_split2(g), axis=1), preferred_element_type=F32)
    return out[:, :w] + out[:, w:]


def _sigmoid(x):
    return 1.0 / (1.0 + jnp.exp(-x))


def _softplus(x):
    return jnp.maximum(x, 0.0) + jnp.log(1.0 + jnp.exp(-jnp.abs(x)))


def _rms(x, w):
    return x * lax.rsqrt(jnp.mean(x * x, axis=-1, keepdims=True) + NORM_EPS) * w


def _stack_heads(x, heads, width):
    lane = lax.broadcasted_iota(jnp.int32, x.shape, 1)
    return jnp.concatenate(
        [jnp.where((lane >= h * width) & (lane < (h + 1) * width), x, 0.0) for h in range(heads)], axis=0)


def _inproj_kernel(z_ref, nw_ref, w_ref, o_ref, h_ref):
    @pl.when(pl.program_id(1) == 0)
    def _():
        h_ref[...] = _rms(z_ref[...], nw_ref[...]).astype(BF16)

    o_ref[...] = jnp.dot(h_ref[...], w_ref[...], preferred_element_type=F32).astype(o_ref.dtype)


def _inproj(z2d, norm_w, w_bf16, tm):
    n = z2d.shape[0]
    return pl.pallas_call(
        _inproj_kernel,
        grid=(n // tm, N_CT),
        in_specs=[
            pl.BlockSpec((tm, D_MODEL), lambda i, j: (i, 0)),
            pl.BlockSpec((1, D_MODEL), lambda i, j: (0, 0)),
            pl.BlockSpec((None, D_MODEL, IN_TN), lambda i, j: (j, 0, 0)),
        ],
        out_specs=pl.BlockSpec((None, tm, IN_TN), lambda i, j: (j, i, 0)),
        out_shape=jax.ShapeDtypeStruct((N_CT, n, IN_TN), BF16),
        scratch_shapes=[pltpu.VMEM((tm, D_MODEL), BF16)],
        compiler_params=pltpu.CompilerParams(
            dimension_semantics=("arbitrary", "arbitrary"), vmem_limit_bytes=VMEM_LIMIT),
        name="inproj",
    )(z2d, norm_w, w_bf16)


def _gla_tiles(qkvg, tril_bd, s_ref, norm_w, heads, dk, first):
    tt, hk = qkvg[0][0].shape
    assert tt % CHUNK == 0 and CHUNK == 2 * SUB
    dv = qkvg[0][2].shape[1] // heads
    nb = len(qkvg)

    @pl.when(first)
    def _():
        s_ref[...] = jnp.zeros_like(s_ref)

    c_alls = [_cumsum_rows(tril_bd, g) for _, _, _, g in qkvg]
    blk1 = lax.broadcasted_iota(jnp.int32, (CHUNK, hk), 0) >= SUB
    st = functools.partial(_stack_heads, heads=heads, width=dk)
    n_st = heads * CHUNK
    causal = (lax.broadcasted_iota(jnp.int32, (n_st, n_st), 1)
              <= lax.broadcasted_iota(jnp.int32, (n_st, n_st), 0))
    chunks = []
    for bi, j in [(bi, j) for j in range(tt // CHUNK) for bi in range(nb)]:
        rows = slice(j * CHUNK, (j + 1) * CHUNK)
        q, k, v, _ = qkvg[bi]
        c, qj, kj, vj = c_alls[bi][rows], q[rows], k[rows], v[rows]
        r0 = c[SUB // 2 - 1:SUB // 2, :]
        r1 = c[SUB + SUB // 2 - 1:SUB + SUB // 2, :]
        c_last = c[CHUNK - 1:CHUNK, :]
        kt = kj * jnp.exp(jnp.where(blk1, r1, r0) - c)
        qa = qj * jnp.exp(c - r0)
        qb = jnp.where(blk1, qj * jnp.exp(jnp.where(blk1, c - r1, 0.0)), 0.0)
        kt0 = jnp.where(blk1, 0.0, kt)
        kt1 = jnp.where(blk1, kt, 0.0)
        qs = jnp.concatenate([st(qa), st(qb)], axis=1).astype(BF16)
        ks = jnp.concatenate([st(kt0), st(kt1)], axis=1).astype(BF16)
        vs = jnp.concatenate([vj[:, h * dv:(h + 1) * dv] for h in range(heads)], axis=0).astype(BF16)
        chunks.append(dict(bi=bi, qs=qs, ks=ks, vs=vs, qe=st(qj * jnp.exp(c)).astype(BF16),
                           kd=st(kj * jnp.exp(c_last - c)).astype(BF16), dec=jnp.exp(c_last)))
    for ch in chunks:
        ch["a"] = jnp.where(causal, _mm_nt(ch["qs"], ch["ks"]), 0.0)
    for ch in chunks:
        ch["oa"] = _mm(ch["a"], ch["vs"])
        ch["kv"] = _mm_tn(ch["vs"], ch["kd"])
    s_t = [s_ref[bi] for bi in range(nb)]
    outs = [[] for _ in range(nb)]
    for ch in chunks:
        bi = ch["bi"]
        o = ch["oa"] + _mm_nt(ch["qe"], s_t[bi])
        s_t[bi] = s_t[bi] * ch["dec"] + ch["kv"]
        o = o * lax.rsqrt(jnp.mean(o * o, axis=-1, keepdims=True) + NORM_EPS) * norm_w
        outs[bi].append(jnp.concatenate([o[h * CHUNK:(h + 1) * CHUNK] for h in range(heads)], axis=1))
    for bi in range(nb):
        s_ref[bi] = s_t[bi]
    return [jnp.concatenate(o, axis=0) for o in outs]


def _silu(x):
    return x * _sigmoid(x)


def _gla_kernel(qk_ref, v_ref, small_ref, og_ref, gkup_ref, gkb_ref, nw_ref, tril_ref, y_ref, s_ref):
    qkvg = []
    for bi in range(qk_ref.shape[0]):
        qk = qk_ref[bi].astype(F32)
        x = _mm(small_ref[bi, :, SMALL_MISC:SMALL_MISC + LANE], gkup_ref[...]) + gkb_ref[...]
        g = -_softplus(-x) * (1.0 / GLA_GATE_NORM)
        qkvg.append((qk[:, :GLA_K] * (GLA_DK ** -0.5), qk[:, GLA_K:], v_ref[bi].astype(F32), g))
    outs = _gla_tiles(qkvg, tril_ref[...], s_ref, nw_ref[...], GLA_HEADS, GLA_DK, pl.program_id(1) == 0)
    for bi, o in enumerate(outs):
        y_ref[bi] = (o * _silu(og_ref[bi].astype(F32))).astype(y_ref.dtype)


def _hgrn_kernel(q_ref, f_ref, i_ref, og_ref, lb_ref, nw_ref, tril_ref, y_ref, s_ref):
    lb = lb_ref[...]
    qkvg = []
    for bi in range(q_ref.shape[0]):
        z = f_ref[bi].astype(F32)
        forget = lb + (1.0 - lb) * _sigmoid(z)
        g = jnp.log(jnp.maximum(forget, F_TINY))
        qkvg.append((q_ref[bi].astype(F32), (1.0 - lb) * _sigmoid(-z), i_ref[bi].astype(F32), g))
    outs = _gla_tiles(qkvg, tril_ref[...], s_ref, nw_ref[...], HG_HEADS, HG_DK, pl.program_id(1) == 0)
    for bi, o in enumerate(outs):
        y_ref[bi] = (o * _silu(og_ref[bi].astype(F32))).astype(y_ref.dtype)


def _pcol(name, nb, tt):
    ct = (name if isinstance(name, int) else _K_OFF[name][0] // IN_TN)
    return pl.BlockSpec((None, nb, tt, IN_TN), lambda b, t: (ct, b, t, 0))


def _const_spec(shape):
    nd = len(shape)
    return pl.BlockSpec(shape, lambda b, t: (0,) * nd)


def _seq_params():
    return pltpu.CompilerParams(dimension_semantics=("arbitrary", "arbitrary"), vmem_limit_bytes=VMEM_LIMIT)


def _gla_mixer(p, gk_up_pad, gk_bias, norm_w, tril, tt, nb):
    _, bsz, tp, _ = p.shape
    col = functools.partial(_pcol, nb=nb, tt=tt)
    y = pl.pallas_call(
        _gla_kernel,
        grid=(bsz // nb, tp // tt),
        in_specs=[col(CT_QK), col("gla_v"), col(CT_SMALL), col("gla_g"),
                  _const_spec(gk_up_pad.shape), _const_spec(gk_bias.shape),
                  _const_spec(norm_w.shape), _const_spec(tril.shape)],
        out_specs=pl.BlockSpec((nb, tt, GLA_V), lambda b, t: (b, t, 0)),
        out_shape=jax.ShapeDtypeStruct((bsz, tp, GLA_V), BF16),
        scratch_shapes=[pltpu.VMEM((nb, GLA_DV, GLA_K), F32)],
        compiler_params=_seq_params(),
        name="gla_mixer",
    )(p, p, p, p, gk_up_pad, gk_bias, norm_w, tril)
    return y.reshape(bsz * tp, GLA_V)


def _hgrn_mixer(p, lb, norm_w, tril, tt, nb):
    _, bsz, tp, _ = p.shape
    col = functools.partial(_pcol, nb=nb, tt=tt)
    y = pl.pallas_call(
        _hgrn_kernel,
        grid=(bsz // nb, tp // tt),
        in_specs=[col("hg_q"), col("hg_f"), col("hg_i"), col("hg_g"),
                  _const_spec(lb.shape), _const_spec(norm_w.shape), _const_spec(tril.shape)],
        out_specs=pl.BlockSpec((nb, tt, HG_V), lambda b, t: (b, t, 0)),
        out_shape=jax.ShapeDtypeStruct((bsz, tp, HG_V), BF16),
        scratch_shapes=[pltpu.VMEM((nb, HG_DV, HG_K), F32)],
        compiler_params=_seq_params(),
        name="hgrn_mixer",
    )(p, p, p, p, lb, norm_w, tril)
    return y.reshape(bsz * tp, HG_V)


def _unit_lower_inverses(n_list, blk16, blk32):
    size = n_list[0].shape[0]
    ri = lax.broadcasted_iota(jnp.int32, (size, size), 0)
    ci = lax.broadcasted_iota(jnp.int32, (size, size), 1)
    eye = jnp.where(ri == ci, 1.0, 0.0)
    n16 = [jnp.where(blk16, n, 0.0) for n in n_list]
    n32 = [jnp.where(blk32 & jnp.logical_not(blk16), n, 0.0).astype(BF16) for n in n_list]
    n64 = [jnp.where(blk32, 0.0, n).astype(BF16) for n in n_list]
    t = [eye + n for n in n16]
    m = [n.astype(BF16) for n in n16]
    for it in range(3):
        m = [_mm(x, x).astype(BF16) for x in m]
        t = [ti + _mm(mi, ti) for mi, ti in zip(m, t)]
    for off_diag in (n32, n64):
        tb = [ti.astype(BF16) for ti in t]
        w = [_mm(ti, ni).astype(BF16) for ti, ni in zip(tb, off_diag)]
        t = [ti + _mm(wi, tbi) for ti, wi, tbi in zip(t, w, tb)]
    return t


def _rwkv_kernel(has_vres, *refs):
    if has_vres:
        (r_ref, k_ref, v_ref, small_ref, vf_ref,
         mu_r_ref, mu_k_ref, mu_v_ref, mu_small_ref,
         w0_ref, w2_ref, a0_ref, a2_ref, g2_ref, v0_ref, v2_ref,
         kk_ref, ka_ref, rk_ref, lnw_ref, lnb_ref, seg_ref, tril_ref,
         y_ref, s_ref, carry_ref) = refs
    else:
        (r_ref, k_ref, v_ref, small_ref,
         mu_r_ref, mu_k_ref, mu_v_ref, mu_small_ref,
         w0_ref, w2_ref, a0_ref, a2_ref, g2_ref,
         kk_ref, ka_ref, rk_ref, lnw_ref, lnb_ref, seg_ref, tril_ref,
         y_ref, vf_out_ref, s_ref, carry_ref) = refs

    first = pl.program_id(1) == 0
    nb, tt = r_ref.shape[0], r_ref.shape[1]

    @pl.when(first)
    def _():
        s_ref[...] = jnp.zeros_like(s_ref)
        carry_ref[...] = jnp.zeros_like(carry_ref)

    seg = seg_ref[...]
    tiles = []
    for bi in range(nb):
        srcs = (r_ref, k_ref, v_ref, small_ref)
        mus = (mu_r_ref, mu_k_ref, mu_v_ref, mu_small_ref)
        shifted = []
        off = 0
        crow = bi * SUBLANE
        for src, mu in zip(srcs, mus):
            p = src[bi].astype(F32)
            width = p.shape[1]
            prev_row = carry_ref[crow:crow + 1, off:off + width]
            rolled = pltpu.roll(p, 1, 0)
            row = lax.broadcasted_iota(jnp.int32, p.shape, 0)
            prev = jnp.where(row == 0, prev_row, rolled)
            shifted.append(p + (prev - p) * mu[...])
            carry_ref[crow:crow + 1, off:off + width] = p[tt - 1:tt, :]
            off += width
        r, k, v, small = shifted
        s_g = small[:, SMALL_G:SMALL_G + LANE]
        s_wa = small[:, SMALL_WA:SMALL_WA + LANE]
        s_misc = small[:, SMALL_MISC:SMALL_MISC + LANE]

        w_log = -_softplus(-(w0_ref[...] + _mm(jnp.tanh(s_wa), w2_ref[...]))) - 0.5
        lw = -jnp.exp(w_log)
        a = _sigmoid(a0_ref[...] + _mm(s_wa, a2_ref[...]))
        g = _mm(_sigmoid(s_g), g2_ref[...])
        if has_vres:
            v = v + (vf_ref[bi] - v) * _sigmoid(v0_ref[...] + _mm(s_misc, v2_ref[...]))
        else:
            vf_out_ref[bi] = v
        kk = k * kk_ref[...]
        k = k * (1.0 + (a - 1.0) * ka_ref[...])
        sums = _head_sums(jnp.concatenate([kk * kk, r * k * rk_ref[...]], axis=0), seg)
        kk = kk / jnp.maximum(jnp.sqrt(sums[:tt]), 1e-12)
        c_all = _cumsum_rows(tril_ref[...], lw)
        tiles.append(dict(r=r, k=k, v=v, g=g, bonus=sums[tt:] * v, alpha=-kk, beta=kk * a,
                          c=c_all, ce=c_all - lw))

    gs = RW_GROUP
    n_groups = RW_HEADS // RW_GROUP_HEADS
    n_st = RW_GROUP_HEADS * CHUNK
    ri = lax.broadcasted_iota(jnp.int32, (n_st, n_st), 0)
    ci = lax.broadcasted_iota(jnp.int32, (n_st, n_st), 1)
    strict = ci < ri
    blk16 = (ri >> 4) == (ci >> 4)
    blk32 = (ri >> 5) == (ci >> 5)
    tcol = lax.broadcasted_iota(jnp.int32, (CHUNK, n_st), 1) & (CHUNK - 1)
    trow = lax.broadcasted_iota(jnp.int32, (CHUNK, n_st), 0)
    incl = tcol <= trow
    st = functools.partial(_stack_heads, heads=RW_GROUP_HEADS, width=RW_HD)

    blocks = []
    for bi, j in [(bi, j) for j in range(tt // CHUNK) for bi in range(nb)]:
        rows = slice(j * CHUNK, (j + 1) * CHUNK)
        tl = tiles[bi]
        c, ce = tl["c"][rows], tl["ce"][rows]
        rj, kj, vj, al, be_ = tl["r"][rows], tl["k"][rows], tl["v"][rows], tl["alpha"][rows], tl["beta"][rows]
        mid = c[CHUNK // 2 - 1:CHUNK // 2, :]
        c_last = c[CHUNK - 1:CHUNK, :]
        e_out = jnp.exp(mid - c)
        e_end = jnp.exp(c_last - c)
        at = al * jnp.exp(ce - mid)
        rt = rj * jnp.exp(c - mid)
        bh = be_ * e_out
        kh = kj * e_out
        ae = al * jnp.exp(ce)
        rs = rj * jnp.exp(c)
        be = be_ * e_end
        ke = kj * e_end
        dec = jnp.exp(c_last)
        for gi in range(n_groups):
            sl = slice(gi * gs, (gi + 1) * gs)
            stb = lambda a_: st(a_[:, sl].astype(BF16))
            blocks.append(dict(j=j, si=bi * n_groups + gi, bm=stb(bh), km=stb(kh), am=stb(at), v_bd=stb(vj),
                               aem=stb(ae),
                               bem=stb(be), kem=stb(ke), rt=rt[:, sl].astype(BF16),
                               rs=rs[:, sl].astype(BF16), dec=dec[:, sl]))
    for b in blocks:
        b["n_ab"] = jnp.where(strict, _mm_nt(b["am"], b["bm"]), 0.0)
    for b in blocks:
        b["n_ak"] = jnp.where(strict, _mm_nt(b["am"], b["km"]), 0.0).astype(BF16)
    for b in blocks:
        b["a_rb"] = jnp.where(incl, _mm_nt(b["rt"], b["bm"]), 0.0).astype(BF16)
        b["a_rk"] = jnp.where(incl, _mm_nt(b["rt"], b["km"]), 0.0).astype(BF16)
    for b, t_inv in zip(blocks, _unit_lower_inverses([b["n_ab"] for b in blocks], blk16, blk32)):
        b["t_inv"] = t_inv.astype(BF16)
    for b in blocks:
        b["w1"] = _mm(b["n_ak"], b["v_bd"])
        b["y0"] = _mm(b["a_rk"], b["v_bd"])
        b["vk"] = _mm_tn(b["v_bd"], b["kem"])

    n_state = nb * n_groups
    state = [s_ref[si] for si in range(n_state)]
    y_rows = [[] for _ in range(nb)]
    for j in range(tt // CHUNK):
        cur = [b for b in blocks if b["j"] == j]
        sb = [state[b["si"]].astype(BF16) for b in cur]
        xs = [_mm_nt(b["aem"], s) + b["w1"] for b, s in zip(cur, sb)]
        us = [_mm(b["t_inv"], x).astype(BF16) for b, x in zip(cur, xs)]
        for b, u in zip(cur, us):
            state[b["si"]] = state[b["si"]] * b["dec"] + _mm_tn(u, b["bem"]) + b["vk"]
        ys = [_mm_nt(b["rs"], s) + _mm(b["a_rb"], u) + b["y0"] for b, s, u in zip(cur, sb, us)]
        for bi in range(nb):
            y_rows[bi].append(jnp.concatenate(ys[bi * n_groups:(bi + 1) * n_groups], axis=1))
    for si in range(n_state):
        s_ref[si] = state[si]

    inv_hd = 1.0 / RW_HD
    for bi in range(nb):
        y = jnp.concatenate(y_rows[bi], axis=0)
        mean = _head_sums(y, seg) * inv_hd
        yc = y - mean
        var = _head_sums(yc * yc, seg) * inv_hd
        y = yc * lax.rsqrt(var + RW_GN_EPS) * lnw_ref[...] + lnb_ref[...]
        y_ref[bi] = ((y + tiles[bi]["bonus"]) * tiles[bi]["g"]).astype(y_ref.dtype)


def _rwkv_mixer(p, v_first, mus, prm, seg, tril, tt, nb):
    _, bsz, tp, _ = p.shape
    nt = tp // tt
    has_vres = v_first is not None
    row_spec = pl.BlockSpec((nb, tt, RW_DIM), lambda b, t: (b, t, 0))
    col = functools.partial(_pcol, nb=nb, tt=tt)
    in_specs = [col("rw_r"), col("rw_k"), col("rw_v"), col(CT_SMALL)]
    args = [p, p, p, p]
    if has_vres:
        in_specs.append(row_spec)
        args.append(v_first)
    consts = list(mus) + [prm["w0"], prm["w2"], prm["a0"], prm["a2"], prm["g2"]]
    if has_vres:
        consts += [prm["v0"], prm["v2"]]
    consts += [prm["kk"], prm["ka"], prm["rk"], prm["lnw"], prm["lnb"], seg, tril]
    in_specs += [_const_spec(c.shape) for c in consts]
    args += consts
    out_shape = [jax.ShapeDtypeStruct((bsz, tp, RW_DIM), BF16)]
    out_specs = [row_spec]
    if not has_vres:
        out_shape.append(jax.ShapeDtypeStruct((bsz, tp, RW_DIM), F32))
        out_specs.append(row_spec)
    carry_w = 3 * RW_DIM + IN_TN
    res = pl.pallas_call(
        functools.partial(_rwkv_kernel, has_vres),
        grid=(bsz // nb, nt),
        in_specs=in_specs,
        out_specs=out_specs,
        out_shape=out_shape,
        scratch_shapes=[pltpu.VMEM((nb * (RW_HEADS // RW_GROUP_HEADS), RW_GROUP, RW_GROUP), F32),
                        pltpu.VMEM((nb * SUBLANE, carry_w), F32)],
        compiler_params=_seq_params(),
        name="rwkv_mixer",
    )(*args)
    y = res[0].reshape(bsz * tp, RW_DIM)
    return (y, v_first) if has_vres else (y, res[1])


def _merge_kernel(z_ref, gg_ref, gr_ref, gh_ref, yg_ref, yr_ref, yh_ref, wa_ref, wb_ref, wc_ref, wo_ref, o_ref):
    d = lambda y, w: jnp.dot(y[...], w[...], preferred_element_type=F32)
    sg = lambda g: _sigmoid(jnp.concatenate([g[t] for t in range(g.shape[0])], axis=1).astype(F32))
    m = sg(gg_ref) * d(yg_ref, wa_ref) + sg(gr_ref) * d(yr_ref, wb_ref) + sg(gh_ref) * d(yh_ref, wc_ref)
    o_ref[...] = z_ref[...] + _mm(m, wo_ref[...])


def _merge(z2d, p, y_gla, y_rw, y_hg, wa, wb, wc, wo, tm):
    n = z2d.shape[0]
    row = lambda w: pl.BlockSpec((tm, w), lambda i: (i, 0))
    gate = lambda name: pl.BlockSpec((D_MODEL // IN_TN, tm, IN_TN),
                                     lambda i, cb=_K_OFF[name][0] // D_MODEL: (cb, i, 0))
    full = lambda a: pl.BlockSpec(a.shape, lambda i: (0, 0))
    return pl.pallas_call(
        _merge_kernel,
        grid=(n // tm,),
        in_specs=[row(D_MODEL), gate("gate_gla"), gate("gate_rw"), gate("gate_hg"),
                  row(GLA_V), row(RW_DIM), row(HG_V), full(wa), full(wb), full(wc), full(wo)],
        out_specs=row(D_MODEL),
        out_shape=jax.ShapeDtypeStruct((n, D_MODEL), F32),
        compiler_params=pltpu.CompilerParams(dimension_semantics=("arbitrary",), vmem_limit_bytes=VMEM_LIMIT),
        name="merge",
    )(z2d, p, p, p, y_gla, y_rw, y_hg, wa, wb, wc, wo)


FFN_FC = 256
FFN_HALO = SUBLANE


def _ffn_kernel(final, z_ref, halo_ref, nw_ref, wug_ref, wuv_ref, cwg_ref, cwv_ref, cbg_ref, cbv_ref, wd_ref,
                fnw_ref, o_ref, act_ref):
    tt = z_ref.shape[0]
    zt = z_ref[...]
    nw = nw_ref[...]
    h = jnp.concatenate([_rms(halo_ref[...], nw), _rms(zt, nw)], axis=0).astype(BF16)
    has_prev = pl.program_id(1) > 0

    def up(j):
        return (jnp.dot(h, wug_ref[j], preferred_element_type=F32),
                jnp.dot(h, wuv_ref[j], preferred_element_type=F32))

    def conv(u, cw, cb):
        u = jnp.concatenate([jnp.where(has_prev, u[:FFN_HALO], 0.0), u[FFN_HALO:]], axis=0)
        inner = u * cw[1:2, :] + pltpu.roll(u * cw[0:1, :], 1, 0)
        c = cb + u * cw[2:3, :] + pltpu.roll(inner, 1, 0)
        return c[FFN_HALO:, :]

    n_chunks = wug_ref.shape[0]
    fc = wug_ref.shape[2]
    split = (n_chunks + 1) // 2
    out = zt
    u_next = up(0)
    for j in range(n_chunks):
        ug, uv = u_next
        if j + 1 < n_chunks:
            u_next = up(j + 1)
        cg = conv(ug, cwg_ref[j], cbg_ref[j])
        cv = conv(uv, cwv_ref[j], cbv_ref[j])
        act_ref[:, j * fc:(j + 1) * fc] = (cg * cv / (1.0 + jnp.exp(-cg))).astype(BF16)
        if j + 1 in (split, n_chunks):
            lo = 0 if j + 1 == split else split
            out = out + jnp.dot(act_ref[:, lo * fc:(j + 1) * fc], wd_ref[lo * fc:(j + 1) * fc, :],
                                preferred_element_type=F32)
    if final:
        out = _rms(out, fnw_ref[...])
    o_ref[...] = out


def _ffn(z3d, norm_w, wug, wuv, cwg, cwv, cbg, cbv, wd, final_w, tt, final):
    bsz, tp, _ = z3d.shape
    hb = tt // FFN_HALO
    once = lambda a: pl.BlockSpec(a.shape, lambda b, t: (0,) * a.ndim, pipeline_mode=pl.Buffered(1))
    return pl.pallas_call(
        functools.partial(_ffn_kernel, final),
        grid=(bsz, tp // tt),
        in_specs=[pl.BlockSpec((None, tt, D_MODEL), lambda b, t: (b, t, 0)),
                  pl.BlockSpec((None, FFN_HALO, D_MODEL), lambda b, t: (b, jnp.maximum(t * hb - 1, 0), 0)),
                  once(norm_w), once(wug), once(wuv), once(cwg), once(cwv), once(cbg), once(cbv), once(wd),
                  once(final_w)],
        out_specs=pl.BlockSpec((None, tt, D_MODEL), lambda b, t: (b, t, 0)),
        out_shape=jax.ShapeDtypeStruct((bsz, tp, D_MODEL), F32),
        scratch_shapes=[pltpu.VMEM((tt, D_FF), BF16)],
        compiler_params=_seq_params(),
        name="conv_ffn",
    )(z3d, z3d, norm_w, wug, wuv, cwg, cwv, cbg, cbv, wd, final_w)


def _largest_tile(n, cands):
    for c in cands:
        if n % c == 0:
            return c
    raise ValueError(f"no tile for {n}")


def _ref_cols(w, name):
    off, width = _REF_OFF[name]
    return w[..., off:off + width]


def _pack_columns(w, vres):
    lead = w.shape[:-1]
    zeros = lambda n: jnp.zeros(lead + (n,), w.dtype)
    vr = vres if vres is not None else zeros(RW_V_RANK)
    parts = []
    for name, width in _K_LAYOUT:
        if name == "rw_wa":
            parts.append(jnp.concatenate([_ref_cols(w, "rw_w"), _ref_cols(w, "rw_a")], axis=-1))
        elif name == "misc":
            parts.append(jnp.concatenate(
                [_ref_cols(w, "gla_gk"), vr, zeros(width - GLA_GATE_RANK - RW_V_RANK)], axis=-1))
        else:
            parts.append(_ref_cols(w, name))
    used = sum(wd for _, wd in _K_LAYOUT)
    parts.append(zeros(WP - used))
    return jnp.concatenate(parts, axis=-1)


def _pad_rows(w, lo, total):
    return jnp.pad(w, ((lo, total - lo - w.shape[0]), (0, 0)))


def kernel(x, meta, mix_norm, w_in, w_in_vres, rw_mu, rw_mu_vres, gla_gk_up, gla_gk_bias, gla_norm, rw_w0,
           rw_w2, rw_a0, rw_a2, rw_v0, rw_v2, rw_g2, rw_kk, rw_ka, rw_rk, rw_ln_w, rw_ln_b, hg_lb_logits,
           hg_norm, w_out_gla, w_out_rw, w_out_hg, w_out, ffn_norm, w_up, conv_w, conv_b, w_down, final_norm):
    bsz, seq, _ = x.shape
    depth = w_in.shape[0]
    t_real = seq + N_META
    tp = -(-t_real // CHUNK) * CHUNK
    n = bsz * tp
    tm = _largest_tile(n, (3072, 2048, 1024, 512, 256, 128, 64))
    tm_merge = _largest_tile(n, (512, 256, 128, 64))
    tt = _largest_tile(tp, (704, 512, 384, 256, 192, 128, 64))
    tt_mix = _largest_tile(tp, (3 * CHUNK, 2 * CHUNK, CHUNK))
    nb_mix = _largest_tile(bsz, (4, 2, 1))

    z = jnp.concatenate([jnp.broadcast_to(meta.astype(x.dtype)[None], (bsz, N_META, D_MODEL)), x,
                         jnp.zeros((bsz, tp - t_real, D_MODEL), x.dtype)], axis=1)

    row = lambda a: a.reshape(1, -1).astype(F32)
    t_idx = jnp.arange(tt_mix)
    tril = ((t_idx[:, None] // CHUNK == t_idx[None, :] // CHUNK)
            & (t_idx[None, :] <= t_idx[:, None])).astype(BF16)
    head_of = jnp.arange(RW_GROUP) // RW_HD
    seg = (head_of[:, None] == head_of[None, :]).astype(BF16)
    lb_p = jax.nn.softmax(hg_lb_logits.astype(F32), axis=0)
    lb_all = jnp.cumsum(lb_p, axis=0) - lb_p[0:1]

    nj = D_FF // FFN_FC
    v_first = None
    for i in range(depth):
        vres_w = w_in_vres[i - 1] if i > 0 else None
        vres_mu = rw_mu_vres[i - 1] if i > 0 else None
        w_cat = _pack_columns(w_in[i], vres_w).astype(BF16)
        w_cat = w_cat.reshape(D_MODEL, N_CT, IN_TN).transpose(1, 0, 2)
        mu_full = jnp.concatenate([jnp.zeros((W_IN - rw_mu.shape[1],), F32), rw_mu[i].astype(F32)])
        mu_cat = _pack_columns(mu_full, vres_mu).reshape(N_CT, 1, IN_TN)
        mus = [mu_cat[_K_OFF[nm][0] // IN_TN] for nm in ("rw_r", "rw_k", "rw_v", "rw_g")]

        z2d = z.reshape(n, D_MODEL)
        p = _inproj(z2d, row(mix_norm[i]), w_cat, tm)

        gk_up_pad = _pad_rows(gla_gk_up[i], MISC_GK, LANE).astype(BF16)
        p4 = p.reshape(N_CT, bsz, tp, IN_TN)
        y_gla = _gla_mixer(p4, gk_up_pad, row(gla_gk_bias[i]), row(gla_norm[i]), tril, tt_mix, nb_mix)
        y_hg = _hgrn_mixer(p4, row(lb_all[i]), row(hg_norm[i]), tril, tt_mix, nb_mix)

        prm = {
            "w0": row(rw_w0[i]), "w2": _pad_rows(rw_w2[i], 0, LANE).astype(BF16),
            "a0": row(rw_a0[i]), "a2": _pad_rows(rw_a2[i], RW_W_RANK, LANE).astype(BF16),
            "g2": rw_g2[i].astype(BF16),
            "kk": row(rw_kk[i]), "ka": row(rw_ka[i]), "rk": row(rw_rk[i]),
            "lnw": row(rw_ln_w[i]), "lnb": row(rw_ln_b[i]),
        }
        if i > 0:
            prm["v0"] = row(rw_v0[i - 1])
            prm["v2"] = _pad_rows(rw_v2[i - 1], MISC_VR, LANE).astype(BF16)
        y_rw, v_first = _rwkv_mixer(p4, v_first, mus, prm, seg, tril, tt_mix, nb_mix)

        z2d = _merge(z2d, p, y_gla, y_rw, y_hg, w_out_gla[i].astype(BF16), w_out_rw[i].astype(BF16),
                     w_out_hg[i].astype(BF16), w_out[i].astype(BF16), tm_merge)

        chunked = lambda a: a.reshape(a.shape[0], nj, FFN_FC).transpose(1, 0, 2)
        wug = chunked(w_up[i][:, :D_FF]).astype(BF16)
        wuv = chunked(w_up[i][:, D_FF:]).astype(BF16)
        cwg = jnp.pad(chunked(conv_w[i][:, :D_FF]), ((0, 0), (0, SUBLANE - CONV_W), (0, 0))).astype(F32)
        cwv = jnp.pad(chunked(conv_w[i][:, D_FF:]), ((0, 0), (0, SUBLANE - CONV_W), (0, 0))).astype(F32)
        cbg = chunked(conv_b[i][None, :D_FF]).astype(F32)
        cbv = chunked(conv_b[i][None, D_FF:]).astype(F32)
        wd = w_down[i].astype(BF16)
        z = _ffn(z2d.reshape(bsz, tp, D_MODEL), row(ffn_norm[i]), wug, wuv, cwg, cwv, cbg, cbv, wd,
                 row(final_norm), tt, final=(i == depth - 1))
    return z[:, N_META:t_real]
```

```python
import functools

import jax
import jax.numpy as jnp
from jax import lax
from jax.experimental import pallas as pl
from jax.experimental.pallas import tpu as pltpu

F32 = jnp.float32
BF16 = jnp.bfloat16

D_MODEL = 1024
N_META = 16
F_TINY = 1e-30
NORM_EPS = 1e-6

GLA_HEADS, GLA_DK, GLA_DV = 4, 64, 128
GLA_K, GLA_V = GLA_HEADS * GLA_DK, GLA_HEADS * GLA_DV
GLA_GATE_RANK = 16
GLA_GATE_NORM = 16.0

RW_HEADS, RW_HD = 8, 64
RW_DIM = RW_HEADS * RW_HD
RW_W_RANK, RW_A_RANK, RW_V_RANK, RW_G_RANK = 64, 64, 32, 128
RW_GN_EPS = 64e-5
RW_GROUP_HEADS = 4
RW_GROUP = RW_GROUP_HEADS * RW_HD

HG_HEADS, HG_DK, HG_DV = 4, 128, 128
HG_K, HG_V = HG_HEADS * HG_DK, HG_HEADS * HG_DV

D_FF = 2816
CONV_W = 3

_REF_LAYOUT = (
    ("gla_q", GLA_K), ("gla_k", GLA_K), ("gla_v", GLA_V), ("gla_gk", GLA_GATE_RANK), ("gla_g", GLA_V),
    ("hg_q", HG_K), ("hg_f", HG_K), ("hg_i", HG_V), ("hg_g", HG_V),
    ("gate_gla", D_MODEL), ("gate_rw", D_MODEL), ("gate_hg", D_MODEL),
    ("rw_r", RW_DIM), ("rw_w", RW_W_RANK), ("rw_k", RW_DIM), ("rw_v", RW_DIM), ("rw_a", RW_A_RANK),
    ("rw_g", RW_G_RANK),
)
_REF_OFF = {}
_o = 0
for _n, _w in _REF_LAYOUT:
    _REF_OFF[_n] = (_o, _w)
    _o += _w
W_IN = _o

LANE = 128
SUBLANE = 8
_K_LAYOUT = (
    ("gate_gla", 1024), ("gate_rw", 1024), ("gate_hg", 1024),
    ("gla_v", 512), ("gla_g", 512), ("hg_q", 512), ("hg_f", 512), ("hg_i", 512), ("hg_g", 512),
    ("rw_r", 512), ("rw_k", 512), ("rw_v", 512),
    ("gla_q", 256), ("gla_k", 256),
    ("rw_g", 128), ("rw_wa", 128), ("misc", 128),
)
_K_OFF = {}
_o = 0
for _n, _w in _K_LAYOUT:
    assert _o % _w == 0
    _K_OFF[_n] = (_o, _w)
    _o += _w
IN_TN = 512
WP = -(-_o // IN_TN) * IN_TN
N_CT = WP // IN_TN
CT_QK = _K_OFF["gla_q"][0] // IN_TN
CT_SMALL = _K_OFF["rw_g"][0] // IN_TN
assert _K_OFF["gla_k"][0] // IN_TN == CT_QK and _K_OFF["misc"][0] // IN_TN == CT_SMALL
SMALL_G, SMALL_WA, SMALL_MISC = (_K_OFF[_n][0] - CT_SMALL * IN_TN for _n in ("rw_g", "rw_wa", "misc"))
MISC_GK = 0
MISC_VR = GLA_GATE_RANK

CHUNK = 64
SUB = 32
VMEM_LIMIT = 56 * 1024 * 1024


def _mm(a, b):
    return jnp.dot(a.astype(BF16), b.astype(BF16), preferred_element_type=F32)


def _mm_nt(a, b):
    return lax.dot_general(a.astype(BF16), b.astype(BF16), (((1,), (1,)), ((), ())),
                           preferred_element_type=F32)


def _mm_tn(a, b):
    return lax.dot_general(a.astype(BF16), b.astype(BF16), (((0,), (0,)), ((), ())),
                           preferred_element_type=F32)


def _split2(x):
    hi = x.astype(BF16)
    lo = (x - hi.astype(F32)).astype(BF16)
    return hi, lo


def _head_sums(x, seg_group):
    gw = seg_group.shape[0]
    xb = x.astype(BF16)
    return jnp.concatenate([jnp.dot(xb[:, i:i + gw], seg_group, preferred_element_type=F32)
                            for i in range(0, x.shape[1], gw)], axis=1)


def _cumsum_rows(tril_bf16, g):
    w = g.shape[1]
    out = jnp.dot(tril_bf16, jnp.concatenate(_split2(g), axis=1), preferred_element_type=F32)
    return out[:, :w] + out[:, w:]


def _sigmoid(x):
    return 1.0 / (1.0 + jnp.exp(-x))


def _softplus(x):
    return jnp.maximum(x, 0.0) + jnp.log(1.0 + jnp.exp(-jnp.abs(x)))


def _rms(x, w):
    return x * lax.rsqrt(jnp.mean(x * x, axis=-1, keepdims=True) + NORM_EPS) * w


def _stack_heads(x, heads, width):
    lane = lax.broadcasted_iota(jnp.int32, x.shape, 1)
    return jnp.concatenate(
        [jnp.where((lane >= h * width) & (lane < (h + 1) * width), x, 0.0) for h in range(heads)], axis=0)


def _inproj_kernel(z_ref, nw_ref, w_ref, o_ref, h_ref):
    @pl.when(pl.program_id(1) == 0)
    def _():
        h_ref[...] = _rms(z_ref[...], nw_ref[...]).astype(BF16)

    o_ref[...] = jnp.dot(h_ref[...], w_ref[...], preferred_element_type=F32).astype(o_ref.dtype)


def _inproj(z2d, norm_w, w_bf16, tm):
    n = z2d.shape[0]
    return pl.pallas_call(
        _inproj_kernel,
        grid=(n // tm, N_CT),
        in_specs=[
            pl.BlockSpec((tm, D_MODEL), lambda i, j: (i, 0)),
            pl.BlockSpec((1, D_MODEL), lambda i, j: (0, 0)),
            pl.BlockSpec((D_MODEL, IN_TN), lambda i, j: (0, j)),
        ],
        out_specs=pl.BlockSpec((None, tm, IN_TN), lambda i, j: (j, i, 0)),
        out_shape=jax.ShapeDtypeStruct((N_CT, n, IN_TN), BF16),
        scratch_shapes=[pltpu.VMEM((tm, D_MODEL), BF16)],
        compiler_params=pltpu.CompilerParams(
            dimension_semantics=("arbitrary", "arbitrary"), vmem_limit_bytes=VMEM_LIMIT),
        name="inproj",
    )(z2d, norm_w, w_bf16)


def _gla_tiles(qkvg, tril_bd, s_ref, norm_w, heads, dk, group_heads, first):
    tt, hk = qkvg[0][0].shape
    assert tt % CHUNK == 0 and CHUNK == 2 * SUB and heads % group_heads == 0
    dv = qkvg[0][2].shape[1] // heads
    nb = len(qkvg)
    gw = group_heads * dk
    n_grp = heads // group_heads

    @pl.when(first)
    def _():
        s_ref[...] = jnp.zeros_like(s_ref)

    c_alls = [_cumsum_rows(tril_bd, g) for _, _, _, g in qkvg]
    blk1 = lax.broadcasted_iota(jnp.int32, (CHUNK, gw), 0) >= SUB
    st = (functools.partial(_stack_heads, heads=group_heads, width=dk) if group_heads > 1 else (lambda a_: a_))
    n_st = group_heads * CHUNK
    causal = (lax.broadcasted_iota(jnp.int32, (n_st, n_st), 1)
              <= lax.broadcasted_iota(jnp.int32, (n_st, n_st), 0))
    chunks = []
    for bi, j, gi in [(bi, j, gi) for j in range(tt // CHUNK) for bi in range(nb) for gi in range(n_grp)]:
        rows = slice(j * CHUNK, (j + 1) * CHUNK)
        lanes = slice(gi * gw, (gi + 1) * gw)
        q, k, v, _ = qkvg[bi]
        c, qj, kj = c_alls[bi][rows, lanes], q[rows, lanes], k[rows, lanes]
        vj = v[rows, gi * group_heads * dv:(gi + 1) * group_heads * dv]
        r0 = c[SUB // 2 - 1:SUB // 2, :]
        r1 = c[SUB + SUB // 2 - 1:SUB + SUB // 2, :]
        c_last = c[CHUNK - 1:CHUNK, :]
        kt = kj * jnp.exp(jnp.where(blk1, r1, r0) - c)
        qa = qj * jnp.exp(c - r0)
        qb = jnp.where(blk1, qj * jnp.exp(jnp.where(blk1, c - r1, 0.0)), 0.0)
        kt0 = jnp.where(blk1, 0.0, kt)
        kt1 = jnp.where(blk1, kt, 0.0)
        qs = jnp.concatenate([st(qa), st(qb)], axis=1).astype(BF16)
        ks = jnp.concatenate([st(kt0), st(kt1)], axis=1).astype(BF16)
        vs = jnp.concatenate([vj[:, h * dv:(h + 1) * dv] for h in range(group_heads)], axis=0).astype(BF16)
        chunks.append(dict(bi=bi, gi=gi, qs=qs, ks=ks, vs=vs, qe=st(qj * jnp.exp(c)).astype(BF16),
                           kd=st(kj * jnp.exp(c_last - c)).astype(BF16), dec=jnp.exp(c_last)))
    for ch in chunks:
        ch["a"] = jnp.where(causal, _mm_nt(ch["qs"], ch["ks"]), 0.0)
    for ch in chunks:
        ch["oa"] = _mm(ch["a"], ch["vs"])
        ch["kv"] = _mm_tn(ch["vs"], ch["kd"])
    s_t = {(bi, gi): s_ref[bi, :, gi * gw:(gi + 1) * gw] for bi in range(nb) for gi in range(n_grp)}
    outs = [[[] for _ in range(n_grp)] for _ in range(nb)]
    for ch in chunks:
        key = (ch["bi"], ch["gi"])
        o = ch["oa"] + _mm_nt(ch["qe"], s_t[key])
        s_t[key] = s_t[key] * ch["dec"] + ch["kv"]
        o = o * lax.rsqrt(jnp.mean(o * o, axis=-1, keepdims=True) + NORM_EPS) * norm_w
        outs[ch["bi"]][ch["gi"]].append(
            jnp.concatenate([o[h * CHUNK:(h + 1) * CHUNK] for h in range(group_heads)], axis=1))
    for (bi, gi), s in s_t.items():
        s_ref[bi, :, gi * gw:(gi + 1) * gw] = s
    return [jnp.concatenate([jnp.concatenate(grp, axis=0) for grp in row_outs], axis=1) for row_outs in outs]


def _silu(x):
    return x * _sigmoid(x)


def _gla_kernel(qk_ref, v_ref, small_ref, og_ref, gkup_ref, gkb_ref, nw_ref, tril_ref, y_ref, s_ref):
    qkvg = []
    for bi in range(qk_ref.shape[0]):
        qk = qk_ref[bi].astype(F32)
        x = _mm(small_ref[bi, :, SMALL_MISC:SMALL_MISC + LANE], gkup_ref[...]) + gkb_ref[...]
        g = -_softplus(-x) * (1.0 / GLA_GATE_NORM)
        qkvg.append((qk[:, :GLA_K] * (GLA_DK ** -0.5), qk[:, GLA_K:], v_ref[bi].astype(F32), g))
    outs = _gla_tiles(qkvg, tril_ref[...], s_ref, nw_ref[...], GLA_HEADS, GLA_DK, GLA_HEADS,
                      pl.program_id(1) == 0)
    for bi, o in enumerate(outs):
        y_ref[bi] = (o * _silu(og_ref[bi].astype(F32))).astype(y_ref.dtype)


def _hgrn_kernel(q_ref, f_ref, i_ref, og_ref, lb_ref, nw_ref, tril_ref, y_ref, s_ref):
    lb = lb_ref[...]
    qkvg = []
    for bi in range(q_ref.shape[0]):
        z = f_ref[bi].astype(F32)
        forget = lb + (1.0 - lb) * _sigmoid(z)
        g = jnp.log(jnp.maximum(forget, F_TINY))
        qkvg.append((q_ref[bi].astype(F32), (1.0 - lb) * _sigmoid(-z), i_ref[bi].astype(F32), g))
    outs = _gla_tiles(qkvg, tril_ref[...], s_ref, nw_ref[...], HG_HEADS, HG_DK, 1, pl.program_id(1) == 0)
    for bi, o in enumerate(outs):
        y_ref[bi] = (o * _silu(og_ref[bi].astype(F32))).astype(y_ref.dtype)


def _pcol(name, nb, tt):
    ct = (name if isinstance(name, int) else _K_OFF[name][0] // IN_TN)
    return pl.BlockSpec((None, nb, tt, IN_TN), lambda b, t: (ct, b, t, 0))


def _const_spec(shape):
    nd = len(shape)
    return pl.BlockSpec(shape, lambda b, t: (0,) * nd)


def _seq_params():
    return pltpu.CompilerParams(dimension_semantics=("arbitrary", "arbitrary"), vmem_limit_bytes=VMEM_LIMIT)


def _gla_mixer(p, gk_up_pad, gk_bias, norm_w, tril, tt, nb):
    _, bsz, tp, _ = p.shape
    col = functools.partial(_pcol, nb=nb, tt=tt)
    y = pl.pallas_call(
        _gla_kernel,
        grid=(bsz // nb, tp // tt),
        in_specs=[col(CT_QK), col("gla_v"), col(CT_SMALL), col("gla_g"),
                  _const_spec(gk_up_pad.shape), _const_spec(gk_bias.shape),
                  _const_spec(norm_w.shape), _const_spec(tril.shape)],
        out_specs=pl.BlockSpec((nb, tt, GLA_V), lambda b, t: (b, t, 0)),
        out_shape=jax.ShapeDtypeStruct((bsz, tp, GLA_V), BF16),
        scratch_shapes=[pltpu.VMEM((nb, GLA_DV, GLA_K), F32)],
        compiler_params=_seq_params(),
        name="gla_mixer",
    )(p, p, p, p, gk_up_pad, gk_bias, norm_w, tril)
    return y.reshape(bsz * tp, GLA_V)


def _hgrn_mixer(p, lb, norm_w, tril, tt, nb):
    _, bsz, tp, _ = p.shape
    col = functools.partial(_pcol, nb=nb, tt=tt)
    y = pl.pallas_call(
        _hgrn_kernel,
        grid=(bsz // nb, tp // tt),
        in_specs=[col("hg_q"), col("hg_f"), col("hg_i"), col("hg_g"),
                  _const_spec(lb.shape), _const_spec(norm_w.shape), _const_spec(tril.shape)],
        out_specs=pl.BlockSpec((nb, tt, HG_V), lambda b, t: (b, t, 0)),
        out_shape=jax.ShapeDtypeStruct((bsz, tp, HG_V), BF16),
        scratch_shapes=[pltpu.VMEM((nb, HG_DV, HG_K), F32)],
        compiler_params=_seq_params(),
        name="hgrn_mixer",
    )(p, p, p, p, lb, norm_w, tril)
    return y.reshape(bsz * tp, HG_V)


def _unit_lower_inverses(n_list, blk16, blk32):
    size = n_list[0].shape[0]
    ri = lax.broadcasted_iota(jnp.int32, (size, size), 0)
    ci = lax.broadcasted_iota(jnp.int32, (size, size), 1)
    eye = jnp.where(ri == ci, 1.0, 0.0)
    n16 = [jnp.where(blk16, n, 0.0) for n in n_list]
    n32 = [jnp.where(blk32 & jnp.logical_not(blk16), n, 0.0).astype(BF16) for n in n_list]
    n64 = [jnp.where(blk32, 0.0, n).astype(BF16) for n in n_list]
    t = [eye + n for n in n16]
    m = [n.astype(BF16) for n in n16]
    for it in range(3):
        m = [_mm(x, x).astype(BF16) for x in m]
        t = [ti + _mm(mi, ti) for mi, ti in zip(m, t)]
    for off_diag in (n32, n64):
        tb = [ti.astype(BF16) for ti in t]
        w = [_mm(ti, ni).astype(BF16) for ti, ni in zip(tb, off_diag)]
        t = [ti + _mm(wi, tbi) for ti, wi, tbi in zip(t, w, tb)]
    return t


def _rwkv_kernel(has_vres, *refs):
    if has_vres:
        (r_ref, k_ref, v_ref, small_ref, vf_ref,
         mu_r_ref, mu_k_ref, mu_v_ref, mu_small_ref,
         w0_ref, w2_ref, a0_ref, a2_ref, g2_ref, v0_ref, v2_ref,
         kk_ref, ka_ref, rk_ref, lnw_ref, lnb_ref, seg_ref, tril_ref,
         y_ref, s_ref, carry_ref) = refs
    else:
        (r_ref, k_ref, v_ref, small_ref,
         mu_r_ref, mu_k_ref, mu_v_ref, mu_small_ref,
         w0_ref, w2_ref, a0_ref, a2_ref, g2_ref,
         kk_ref, ka_ref, rk_ref, lnw_ref, lnb_ref, seg_ref, tril_ref,
         y_ref, vf_out_ref, s_ref, carry_ref) = refs

    first = pl.program_id(1) == 0
    nb, tt = r_ref.shape[0], r_ref.shape[1]

    @pl.when(first)
    def _():
        s_ref[...] = jnp.zeros_like(s_ref)
        carry_ref[...] = jnp.zeros_like(carry_ref)

    seg = seg_ref[...]
    tiles = {}

    def prologue(bi):
        srcs = (r_ref, k_ref, v_ref, small_ref)
        mus = (mu_r_ref, mu_k_ref, mu_v_ref, mu_small_ref)
        shifted = []
        off = 0
        crow = bi * SUBLANE
        for src, mu in zip(srcs, mus):
            p = src[bi].astype(F32)
            width = p.shape[1]
            prev_row = carry_ref[crow:crow + 1, off:off + width]
            rolled = pltpu.roll(p, 1, 0)
            row = lax.broadcasted_iota(jnp.int32, p.shape, 0)
            prev = jnp.where(row == 0, prev_row, rolled)
            shifted.append(p + (prev - p) * mu[...])
            carry_ref[crow:crow + 1, off:off + width] = p[tt - 1:tt, :]
            off += width
        r, k, v, small = shifted
        s_g = small[:, SMALL_G:SMALL_G + LANE]
        s_wa = small[:, SMALL_WA:SMALL_WA + LANE]
        s_misc = small[:, SMALL_MISC:SMALL_MISC + LANE]

        w_log = -_softplus(-(w0_ref[...] + _mm(jnp.tanh(s_wa), w2_ref[...]))) - 0.5
        lw = -jnp.exp(w_log)
        a = _sigmoid(a0_ref[...] + _mm(s_wa, a2_ref[...]))
        g = _mm(_sigmoid(s_g), g2_ref[...])
        if has_vres:
            v = v + (vf_ref[bi] - v) * _sigmoid(v0_ref[...] + _mm(s_misc, v2_ref[...]))
        else:
            vf_out_ref[bi] = v
        kk = k * kk_ref[...]
        k = k * (1.0 + (a - 1.0) * ka_ref[...])
        sums = _head_sums(jnp.concatenate([kk * kk, r * k * rk_ref[...]], axis=0), seg)
        kk = kk / jnp.maximum(jnp.sqrt(sums[:tt]), 1e-12)
        c_all = _cumsum_rows(tril_ref[...], lw)
        tiles[bi] = dict(r=r, k=k, v=v, g=g, bonus=sums[tt:] * v, alpha=-kk, beta=kk * a,
                         c=c_all, ce=c_all - lw)

    gs = RW_GROUP
    n_groups = RW_HEADS // RW_GROUP_HEADS
    n_st = RW_GROUP_HEADS * CHUNK
    ri = lax.broadcasted_iota(jnp.int32, (n_st, n_st), 0)
    ci = lax.broadcasted_iota(jnp.int32, (n_st, n_st), 1)
    strict = ci < ri
    blk16 = (ri >> 4) == (ci >> 4)
    blk32 = (ri >> 5) == (ci >> 5)
    tcol = lax.broadcasted_iota(jnp.int32, (CHUNK, n_st), 1) & (CHUNK - 1)
    trow = lax.broadcasted_iota(jnp.int32, (CHUNK, n_st), 0)
    incl = tcol <= trow
    st = functools.partial(_stack_heads, heads=RW_GROUP_HEADS, width=RW_HD)

    for bi in range(nb):
        prologue(bi)
    blocks = _rwkv_blocks(tiles, range(nb), tt, n_groups, gs, st, strict, incl, blk16, blk32)

    n_state = nb * n_groups
    state = [s_ref[si] for si in range(n_state)]
    y_rows = [[] for _ in range(nb)]
    for j in range(tt // CHUNK):
        cur = sorted([b for b in blocks if b["j"] == j], key=lambda b: b["si"])
        sb = [state[b["si"]].astype(BF16) for b in cur]
        xs = [_mm_nt(b["aem"], s) + b["w1"] for b, s in zip(cur, sb)]
        us = [_mm(b["t_inv"], x).astype(BF16) for b, x in zip(cur, xs)]
        for b, u in zip(cur, us):
            state[b["si"]] = state[b["si"]] * b["dec"] + _mm_tn(u, b["bem"]) + b["vk"]
        ys = [_mm_nt(b["rs"], s) + _mm(b["a_rb"], u) + b["y0"] for b, s, u in zip(cur, sb, us)]
        for bi in range(nb):
            y_rows[bi].append(jnp.concatenate(ys[bi * n_groups:(bi + 1) * n_groups], axis=1))
    for si in range(n_state):
        s_ref[si] = state[si]

    inv_hd = 1.0 / RW_HD
    for bi in range(nb):
        y = jnp.concatenate(y_rows[bi], axis=0)
        mean = _head_sums(y, seg) * inv_hd
        yc = y - mean
        var = _head_sums(yc * yc, seg) * inv_hd
        y = yc * lax.rsqrt(var + RW_GN_EPS) * lnw_ref[...] + lnb_ref[...]
        y_ref[bi] = ((y + tiles[bi]["bonus"]) * tiles[bi]["g"]).astype(y_ref.dtype)


def _rwkv_blocks(tiles, batch_rows, tt, n_groups, gs, st, strict, incl, blk16, blk32):
    blocks = []
    for bi, j in [(bi, j) for j in range(tt // CHUNK) for bi in batch_rows]:
        rows = slice(j * CHUNK, (j + 1) * CHUNK)
        tl = tiles[bi]
        c, ce = tl["c"][rows], tl["ce"][rows]
        rj, kj, vj, al, be_ = tl["r"][rows], tl["k"][rows], tl["v"][rows], tl["alpha"][rows], tl["beta"][rows]
        mid = c[CHUNK // 2 - 1:CHUNK // 2, :]
        c_last = c[CHUNK - 1:CHUNK, :]
        e_out = jnp.exp(mid - c)
        e_end = jnp.exp(c_last - c)
        at = al * jnp.exp(ce - mid)
        rt = rj * jnp.exp(c - mid)
        bh = be_ * e_out
        kh = kj * e_out
        ae = al * jnp.exp(ce)
        rs = rj * jnp.exp(c)
        be = be_ * e_end
        ke = kj * e_end
        dec = jnp.exp(c_last)
        for gi in range(n_groups):
            sl = slice(gi * gs, (gi + 1) * gs)
            stb = lambda a_: st(a_[:, sl].astype(BF16))
            blocks.append(dict(j=j, si=bi * n_groups + gi, bm=stb(bh), km=stb(kh), am=stb(at), v_bd=stb(vj),
                               aem=stb(ae),
                               bem=stb(be), kem=stb(ke), rt=rt[:, sl].astype(BF16),
                               rs=rs[:, sl].astype(BF16), dec=dec[:, sl]))
    for b in blocks:
        b["n_ab"] = jnp.where(strict, _mm_nt(b["am"], b["bm"]), 0.0)
    for b in blocks:
        b["n_ak"] = jnp.where(strict, _mm_nt(b["am"], b["km"]), 0.0).astype(BF16)
    for b in blocks:
        b["a_rb"] = jnp.where(incl, _mm_nt(b["rt"], b["bm"]), 0.0).astype(BF16)
        b["a_rk"] = jnp.where(incl, _mm_nt(b["rt"], b["km"]), 0.0).astype(BF16)
    for b, t_inv in zip(blocks, _unit_lower_inverses([b["n_ab"] for b in blocks], blk16, blk32)):
        b["t_inv"] = t_inv.astype(BF16)
    for b in blocks:
        b["w1"] = _mm(b["n_ak"], b["v_bd"])
        b["y0"] = _mm(b["a_rk"], b["v_bd"])
        b["vk"] = _mm_tn(b["v_bd"], b["kem"])
    return blocks


def _rwkv_mixer(p, v_first, mus, prm, seg, tril, tt, nb):
    _, bsz, tp, _ = p.shape
    nt = tp // tt
    has_vres = v_first is not None
    row_spec = pl.BlockSpec((nb, tt, RW_DIM), lambda b, t: (b, t, 0))
    col = functools.partial(_pcol, nb=nb, tt=tt)
    in_specs = [col("rw_r"), col("rw_k"), col("rw_v"), col(CT_SMALL)]
    args = [p, p, p, p]
    if has_vres:
        in_specs.append(row_spec)
        args.append(v_first)
    consts = list(mus) + [prm["w0"], prm["w2"], prm["a0"], prm["a2"], prm["g2"]]
    if has_vres:
        consts += [prm["v0"], prm["v2"]]
    consts += [prm["kk"], prm["ka"], prm["rk"], prm["lnw"], prm["lnb"], seg, tril]
    in_specs += [_const_spec(c.shape) for c in consts]
    args += consts
    out_shape = [jax.ShapeDtypeStruct((bsz, tp, RW_DIM), BF16)]
    out_specs = [row_spec]
    if not has_vres:
        out_shape.append(jax.ShapeDtypeStruct((bsz, tp, RW_DIM), F32))
        out_specs.append(row_spec)
    carry_w = 3 * RW_DIM + IN_TN
    res = pl.pallas_call(
        functools.partial(_rwkv_kernel, has_vres),
        grid=(bsz // nb, nt),
        in_specs=in_specs,
        out_specs=out_specs,
        out_shape=out_shape,
        scratch_shapes=[pltpu.VMEM((nb * (RW_HEADS // RW_GROUP_HEADS), RW_GROUP, RW_GROUP), F32),
                        pltpu.VMEM((nb * SUBLANE, carry_w), F32)],
        compiler_params=_seq_params(),
        name="rwkv_mixer",
    )(*args)
    y = res[0].reshape(bsz * tp, RW_DIM)
    return (y, v_first) if has_vres else (y, res[1])


def _merge_kernel(z_ref, gg_ref, gr_ref, gh_ref, yg_ref, yr_ref, yh_ref, wa_ref, wb_ref, wc_ref, wo_ref, o_ref):
    d = lambda y, w: jnp.dot(y[...], w[...], preferred_element_type=F32)
    sg = lambda g: _sigmoid(jnp.concatenate([g[t] for t in range(g.shape[0])], axis=1).astype(F32))
    m = sg(gg_ref) * d(yg_ref, wa_ref) + sg(gr_ref) * d(yr_ref, wb_ref) + sg(gh_ref) * d(yh_ref, wc_ref)
    o_ref[...] = z_ref[...] + _mm(m, wo_ref[...])


def _merge(z2d, p, y_gla, y_rw, y_hg, wa, wb, wc, wo, tm):
    n = z2d.shape[0]
    row = lambda w: pl.BlockSpec((tm, w), lambda i: (i, 0))
    gate = lambda name: pl.BlockSpec((D_MODEL // IN_TN, tm, IN_TN),
                                     lambda i, cb=_K_OFF[name][0] // D_MODEL: (cb, i, 0))
    full = lambda a: pl.BlockSpec(a.shape, lambda i: (0, 0))
    return pl.pallas_call(
        _merge_kernel,
        grid=(n // tm,),
        in_specs=[row(D_MODEL), gate("gate_gla"), gate("gate_rw"), gate("gate_hg"),
                  row(GLA_V), row(RW_DIM), row(HG_V), full(wa), full(wb), full(wc), full(wo)],
        out_specs=row(D_MODEL),
        out_shape=jax.ShapeDtypeStruct((n, D_MODEL), F32),
        compiler_params=pltpu.CompilerParams(dimension_semantics=("arbitrary",), vmem_limit_bytes=VMEM_LIMIT),
        name="merge",
    )(z2d, p, p, p, y_gla, y_rw, y_hg, wa, wb, wc, wo)


FFN_FC = 256
FFN_HALO = SUBLANE


def _ffn_kernel(final, z_ref, halo_ref, nw_ref, wu_ref, cw_ref, cb_ref, wd_ref, fnw_ref, o_ref, act_ref):
    tt = z_ref.shape[0]
    zt = z_ref[...]
    nw = nw_ref[...]
    h = jnp.concatenate([_rms(halo_ref[...], nw), _rms(zt, nw)], axis=0).astype(BF16)
    has_prev = pl.program_id(1) > 0
    fc = FFN_FC
    n_chunks = D_FF // fc
    cols = lambda j, half: slice(half * D_FF + j * fc, half * D_FF + (j + 1) * fc)

    def up(j):
        return (jnp.dot(h, wu_ref[:, cols(j, 0)], preferred_element_type=F32),
                jnp.dot(h, wu_ref[:, cols(j, 1)], preferred_element_type=F32))

    def conv(u, cw, cb):
        u = jnp.concatenate([jnp.where(has_prev, u[:FFN_HALO], 0.0), u[FFN_HALO:]], axis=0)
        inner = u * cw[1:2, :] + pltpu.roll(u * cw[0:1, :], 1, 0)
        c = cb + u * cw[2:3, :] + pltpu.roll(inner, 1, 0)
        return c[FFN_HALO:, :]

    split = (n_chunks + 1) // 2
    out = zt
    u_next = up(0)
    for j in range(n_chunks):
        ug, uv = u_next
        if j + 1 < n_chunks:
            u_next = up(j + 1)
        cg = conv(ug, cw_ref[:, cols(j, 0)], cb_ref[:, cols(j, 0)])
        cv = conv(uv, cw_ref[:, cols(j, 1)], cb_ref[:, cols(j, 1)])
        act_ref[:, j * fc:(j + 1) * fc] = (cg * cv / (1.0 + jnp.exp(-cg))).astype(BF16)
        if j + 1 in (split, n_chunks):
            lo = 0 if j + 1 == split else split
            out = out + jnp.dot(act_ref[:, lo * fc:(j + 1) * fc], wd_ref[lo * fc:(j + 1) * fc, :],
                                preferred_element_type=F32)
    if final:
        out = _rms(out, fnw_ref[...])
    o_ref[...] = out


def _ffn(z3d, norm_w, wu, cw, cb, wd, final_w, tt, final):
    bsz, tp, _ = z3d.shape
    hb = tt // FFN_HALO
    once = lambda a: pl.BlockSpec(a.shape, lambda b, t: (0,) * a.ndim, pipeline_mode=pl.Buffered(1))
    return pl.pallas_call(
        functools.partial(_ffn_kernel, final),
        grid=(bsz, tp // tt),
        in_specs=[pl.BlockSpec((None, tt, D_MODEL), lambda b, t: (b, t, 0)),
                  pl.BlockSpec((None, FFN_HALO, D_MODEL), lambda b, t: (b, jnp.maximum(t * hb - 1, 0), 0)),
                  once(norm_w), once(wu), once(cw), once(cb), once(wd), once(final_w)],
        out_specs=pl.BlockSpec((None, tt, D_MODEL), lambda b, t: (b, t, 0)),
        out_shape=jax.ShapeDtypeStruct((bsz, tp, D_MODEL), F32),
        scratch_shapes=[pltpu.VMEM((tt, D_FF), BF16)],
        compiler_params=_seq_params(),
        name="conv_ffn",
    )(z3d, z3d, norm_w, wu, cw, cb, wd, final_w)


def _largest_tile(n, cands):
    for c in cands:
        if n % c == 0:
            return c
    raise ValueError(f"no tile for {n}")


def _ref_cols(w, name):
    off, width = _REF_OFF[name]
    return w[..., off:off + width]


def _pack_columns(w, vres):
    lead = w.shape[:-1]
    zeros = lambda n: jnp.zeros(lead + (n,), w.dtype)
    vr = vres if vres is not None else zeros(RW_V_RANK)
    parts = []
    for name, width in _K_LAYOUT:
        if name == "rw_wa":
            parts.append(jnp.concatenate([_ref_cols(w, "rw_w"), _ref_cols(w, "rw_a")], axis=-1))
        elif name == "misc":
            parts.append(jnp.concatenate(
                [_ref_cols(w, "gla_gk"), vr, zeros(width - GLA_GATE_RANK - RW_V_RANK)], axis=-1))
        else:
            parts.append(_ref_cols(w, name))
    used = sum(wd for _, wd in _K_LAYOUT)
    parts.append(zeros(WP - used))
    return jnp.concatenate(parts, axis=-1)


def _pad_rows(w, lo, total):
    return jnp.pad(w, ((lo, total - lo - w.shape[0]), (0, 0)))


def kernel(x, meta, mix_norm, w_in, w_in_vres, rw_mu, rw_mu_vres, gla_gk_up, gla_gk_bias, gla_norm, rw_w0,
           rw_w2, rw_a0, rw_a2, rw_v0, rw_v2, rw_g2, rw_kk, rw_ka, rw_rk, rw_ln_w, rw_ln_b, hg_lb_logits,
           hg_norm, w_out_gla, w_out_rw, w_out_hg, w_out, ffn_norm, w_up, conv_w, conv_b, w_down, final_norm):
    bsz, seq, _ = x.shape
    depth = w_in.shape[0]
    t_real = seq + N_META
    tp = -(-t_real // CHUNK) * CHUNK
    n = bsz * tp
    tm = _largest_tile(n, (3072, 2048, 1024, 512, 256, 128, 64))
    tm_merge = _largest_tile(n, (512, 256, 128, 64))
    tt = _largest_tile(tp, (704, 512, 384, 256, 192, 128, 64))
    tt_mix = _largest_tile(tp, (3 * CHUNK, 2 * CHUNK, CHUNK))
    nb_mix = _largest_tile(bsz, (4, 2, 1))

    z = jnp.concatenate([jnp.broadcast_to(meta.astype(x.dtype)[None], (bsz, N_META, D_MODEL)), x,
                         jnp.zeros((bsz, tp - t_real, D_MODEL), x.dtype)], axis=1)

    row = lambda a: a.reshape(1, -1).astype(F32)
    t_idx = jnp.arange(tt_mix)
    tril = ((t_idx[:, None] // CHUNK == t_idx[None, :] // CHUNK)
            & (t_idx[None, :] <= t_idx[:, None])).astype(BF16)
    head_of = jnp.arange(RW_GROUP) // RW_HD
    seg = (head_of[:, None] == head_of[None, :]).astype(BF16)
    lb_p = jax.nn.softmax(hg_lb_logits.astype(F32), axis=0)
    lb_all = jnp.cumsum(lb_p, axis=0) - lb_p[0:1]

    nj = D_FF // FFN_FC
    v_first = None
    for i in range(depth):
        vres_w = w_in_vres[i - 1] if i > 0 else None
        vres_mu = rw_mu_vres[i - 1] if i > 0 else None
        w_cat = _pack_columns(w_in[i], vres_w).astype(BF16)
        mu_full = jnp.concatenate([jnp.zeros((W_IN - rw_mu.shape[1],), F32), rw_mu[i].astype(F32)])
        mu_cat = _pack_columns(mu_full, vres_mu).reshape(N_CT, 1, IN_TN)
        mus = [mu_cat[_K_OFF[nm][0] // IN_TN] for nm in ("rw_r", "rw_k", "rw_v", "rw_g")]

        z2d = z.reshape(n, D_MODEL)
        p = _inproj(z2d, row(mix_norm[i]), w_cat, tm)

        gk_up_pad = _pad_rows(gla_gk_up[i], MISC_GK, LANE).astype(BF16)
        p4 = p.reshape(N_CT, bsz, tp, IN_TN)
        y_gla = _gla_mixer(p4, gk_up_pad, row(gla_gk_bias[i]), row(gla_norm[i]), tril, tt_mix, nb_mix)
        y_hg = _hgrn_mixer(p4, row(lb_all[i]), row(hg_norm[i]), tril, tt_mix, nb_mix)

        prm = {
            "w0": row(rw_w0[i]), "w2": _pad_rows(rw_w2[i], 0, LANE).astype(BF16),
            "a0": row(rw_a0[i]), "a2": _pad_rows(rw_a2[i], RW_W_RANK, LANE).astype(BF16),
            "g2": rw_g2[i].astype(BF16),
            "kk": row(rw_kk[i]), "ka": row(rw_ka[i]), "rk": row(rw_rk[i]),
            "lnw": row(rw_ln_w[i]), "lnb": row(rw_ln_b[i]),
        }
        if i > 0:
            prm["v0"] = row(rw_v0[i - 1])
            prm["v2"] = _pad_rows(rw_v2[i - 1], MISC_VR, LANE).astype(BF16)
        y_rw, v_first = _rwkv_mixer(p4, v_first, mus, prm, seg, tril, tt_mix, nb_mix)

        z2d = _merge(z2d, p, y_gla, y_rw, y_hg, w_out_gla[i].astype(BF16), w_out_rw[i].astype(BF16),
                     w_out_hg[i].astype(BF16), w_out[i].astype(BF16), tm_merge)

        cw = jnp.pad(conv_w[i].astype(F32), ((0, SUBLANE - CONV_W), (0, 0)))
        z = _ffn(z2d.reshape(bsz, tp, D_MODEL), row(ffn_norm[i]), w_up[i].astype(BF16), cw, row(conv_b[i]),
                 w_down[i].astype(BF16), row(final_norm), tt, final=(i == depth - 1))
    return z[:, N_META:t_real]
```

```python
import functools

import jax
import jax.numpy as jnp
from jax import lax
from jax.experimental import pallas as pl
from jax.experimental.pallas import tpu as pltpu

F32 = jnp.float32
BF16 = jnp.bfloat16

D_MODEL = 1024
N_META = 16
F_TINY = 1e-30
NORM_EPS = 1e-6

GLA_HEADS, GLA_DK, GLA_DV = 4, 64, 128
GLA_K, GLA_V = GLA_HEADS * GLA_DK, GLA_HEADS * GLA_DV
GLA_GATE_RANK = 16
GLA_GATE_NORM = 16.0

RW_HEADS, RW_HD = 8, 64
RW_DIM = RW_HEADS * RW_HD
RW_W_RANK, RW_A_RANK, RW_V_RANK, RW_G_RANK = 64, 64, 32, 128
RW_GN_EPS = 64e-5
RW_GROUP_HEADS = 4
RW_GROUP = RW_GROUP_HEADS * RW_HD

HG_HEADS, HG_DK, HG_DV = 4, 128, 128
HG_K, HG_V = HG_HEADS * HG_DK, HG_HEADS * HG_DV

D_FF = 2816
CONV_W = 3

_REF_LAYOUT = (
    ("gla_q", GLA_K), ("gla_k", GLA_K), ("gla_v", GLA_V), ("gla_gk", GLA_GATE_RANK), ("gla_g", GLA_V),
    ("hg_q", HG_K), ("hg_f", HG_K), ("hg_i", HG_V), ("hg_g", HG_V),
    ("gate_gla", D_MODEL), ("gate_rw", D_MODEL), ("gate_hg", D_MODEL),
    ("rw_r", RW_DIM), ("rw_w", RW_W_RANK), ("rw_k", RW_DIM), ("rw_v", RW_DIM), ("rw_a", RW_A_RANK),
    ("rw_g", RW_G_RANK),
)
_REF_OFF = {}
_o = 0
for _n, _w in _REF_LAYOUT:
    _REF_OFF[_n] = (_o, _w)
    _o += _w
W_IN = _o

LANE = 128
SUBLANE = 8
_K_LAYOUT = (
    ("gate_gla", 1024), ("gate_rw", 1024), ("gate_hg", 1024),
    ("gla_v", 512), ("gla_g", 512), ("hg_q", 512), ("hg_f", 512), ("hg_i", 512), ("hg_g", 512),
    ("rw_r", 512), ("rw_k", 512), ("rw_v", 512),
    ("gla_q", 256), ("gla_k", 256),
    ("rw_g", 128), ("rw_wa", 128), ("misc", 128),
)
_K_OFF = {}
_o = 0
for _n, _w in _K_LAYOUT:
    assert _o % _w == 0
    _K_OFF[_n] = (_o, _w)
    _o += _w
IN_TN = 512
WP = -(-_o // IN_TN) * IN_TN
N_CT = WP // IN_TN
CT_QK = _K_OFF["gla_q"][0] // IN_TN
CT_SMALL = _K_OFF["rw_g"][0] // IN_TN
assert _K_OFF["gla_k"][0] // IN_TN == CT_QK and _K_OFF["misc"][0] // IN_TN == CT_SMALL
SMALL_G, SMALL_WA, SMALL_MISC = (_K_OFF[_n][0] - CT_SMALL * IN_TN for _n in ("rw_g", "rw_wa", "misc"))
MISC_GK = 0
MISC_VR = GLA_GATE_RANK

CHUNK = 64
SUB = 32
VMEM_LIMIT = 56 * 1024 * 1024


def _mm(a, b):
    return jnp.dot(a.astype(BF16), b.astype(BF16), preferred_element_type=F32)


def _mm_nt(a, b):
    return lax.dot_general(a.astype(BF16), b.astype(BF16), (((1,), (1,)), ((), ())),
                           preferred_element_type=F32)


def _mm_tn(a, b):
    return lax.dot_general(a.astype(BF16), b.astype(BF16), (((0,), (0,)), ((), ())),
                           preferred_element_type=F32)


def _split2(x):
    hi = x.astype(BF16)
    lo = (x - hi.astype(F32)).astype(BF16)
    return hi, lo


def _head_sums(x, seg_group):
    gw = seg_group.shape[0]
    xb = x.astype(BF16)
    return jnp.concatenate([jnp.dot(xb[:, i:i + gw], seg_group, preferred_element_type=F32)
                            for i in range(0, x.shape[1], gw)], axis=1)


def _cumsum_rows(tril_bf16, g):
    w = g.shape[1]
    out = jnp.dot(tril_bf16, jnp.concatenate(_split2(g), axis=1), preferred_element_type=F32)
    return out[:, :w] + out[:, w:]


def _sigmoid(x):
    return 1.0 / (1.0 + jnp.exp(-x))


def _softplus(x):
    return jnp.maximum(x, 0.0) + jnp.log(1.0 + jnp.exp(-jnp.abs(x)))


def _rms(x, w):
    return x * lax.rsqrt(jnp.mean(x * x, axis=-1, keepdims=True) + NORM_EPS) * w


def _stack_heads(x, heads, width):
    lane = lax.broadcasted_iota(jnp.int32, x.shape, 1)
    return jnp.concatenate(
        [jnp.where((lane >= h * width) & (lane < (h + 1) * width), x, 0.0) for h in range(heads)], axis=0)


def _inproj_kernel(z_ref, nw_ref, w_ref, o_ref, *scratch):
    if not scratch:
        h = z_ref[...]
    else:
        h_ref, = scratch

        @pl.when(pl.program_id(1) == 0)
        def _():
            h_ref[...] = _rms(z_ref[...], nw_ref[...]).astype(BF16)

        h = h_ref[...]
    o_ref[...] = jnp.dot(h, w_ref[...], preferred_element_type=F32).astype(o_ref.dtype)


def _inproj(z2d, norm_w, w_bf16, tm):
    n = z2d.shape[0]
    return pl.pallas_call(
        _inproj_kernel,
        grid=(n // tm, N_CT),
        in_specs=[
            pl.BlockSpec((tm, D_MODEL), lambda i, j: (i, 0)),
            pl.BlockSpec((1, D_MODEL), lambda i, j: (0, 0)),
            pl.BlockSpec((D_MODEL, IN_TN), lambda i, j: (0, j)),
        ],
        out_specs=pl.BlockSpec((None, tm, IN_TN), lambda i, j: (j, i, 0)),
        out_shape=jax.ShapeDtypeStruct((N_CT, n, IN_TN), BF16),
        scratch_shapes=[] if z2d.dtype == BF16 else [pltpu.VMEM((tm, D_MODEL), BF16)],
        compiler_params=pltpu.CompilerParams(
            dimension_semantics=("arbitrary", "arbitrary"), vmem_limit_bytes=VMEM_LIMIT),
        name="inproj",
    )(z2d, norm_w, w_bf16)


def _gla_tiles(qkvg, tril_bd, s_ref, norm_w, heads, dk, group_heads, first):
    tt, hk = qkvg[0][0].shape
    assert tt % CHUNK == 0 and CHUNK == 2 * SUB and heads % group_heads == 0
    dv = qkvg[0][2].shape[1] // heads
    nb = len(qkvg)
    gw = group_heads * dk
    n_grp = heads // group_heads

    @pl.when(first)
    def _():
        s_ref[...] = jnp.zeros_like(s_ref)

    c_alls = [_cumsum_rows(tril_bd, g) for _, _, _, g in qkvg]
    blk1 = lax.broadcasted_iota(jnp.int32, (CHUNK, gw), 0) >= SUB
    st = (functools.partial(_stack_heads, heads=group_heads, width=dk) if group_heads > 1 else (lambda a_: a_))
    n_st = group_heads * CHUNK
    causal = (lax.broadcasted_iota(jnp.int32, (n_st, n_st), 1)
              <= lax.broadcasted_iota(jnp.int32, (n_st, n_st), 0))
    chunks = []
    for bi, j, gi in [(bi, j, gi) for j in range(tt // CHUNK) for bi in range(nb) for gi in range(n_grp)]:
        rows = slice(j * CHUNK, (j + 1) * CHUNK)
        lanes = slice(gi * gw, (gi + 1) * gw)
        q, k, v, _ = qkvg[bi]
        c, qj, kj = c_alls[bi][rows, lanes], q[rows, lanes], k[rows, lanes]
        vj = v[rows, gi * group_heads * dv:(gi + 1) * group_heads * dv]
        r0 = c[SUB // 2 - 1:SUB // 2, :]
        r1 = c[SUB + SUB // 2 - 1:SUB + SUB // 2, :]
        c_last = c[CHUNK - 1:CHUNK, :]
        kt = kj * jnp.exp(jnp.where(blk1, r1, r0) - c)
        qa = qj * jnp.exp(c - r0)
        qb = jnp.where(blk1, qj * jnp.exp(jnp.where(blk1, c - r1, 0.0)), 0.0)
        kt0 = jnp.where(blk1, 0.0, kt)
        kt1 = jnp.where(blk1, kt, 0.0)
        qs = jnp.concatenate([st(qa), st(qb)], axis=1).astype(BF16)
        ks = jnp.concatenate([st(kt0), st(kt1)], axis=1).astype(BF16)
        vs = jnp.concatenate([vj[:, h * dv:(h + 1) * dv] for h in range(group_heads)], axis=0).astype(BF16)
        chunks.append(dict(bi=bi, gi=gi, qs=qs, ks=ks, vs=vs, qe=st(qj * jnp.exp(c)).astype(BF16),
                           kd=st(kj * jnp.exp(c_last - c)).astype(BF16), dec=jnp.exp(c_last)))
    for ch in chunks:
        ch["a"] = jnp.where(causal, _mm_nt(ch["qs"], ch["ks"]), 0.0)
    for ch in chunks:
        ch["oa"] = _mm(ch["a"], ch["vs"])
        ch["kv"] = _mm_tn(ch["vs"], ch["kd"])
    s_t = {(bi, gi): s_ref[bi, :, gi * gw:(gi + 1) * gw] for bi in range(nb) for gi in range(n_grp)}
    outs = [[[] for _ in range(n_grp)] for _ in range(nb)]
    for ch in chunks:
        key = (ch["bi"], ch["gi"])
        o = ch["oa"] + _mm_nt(ch["qe"], s_t[key])
        s_t[key] = s_t[key] * ch["dec"] + ch["kv"]
        o = o * lax.rsqrt(jnp.mean(o * o, axis=-1, keepdims=True) + NORM_EPS) * norm_w
        outs[ch["bi"]][ch["gi"]].append(
            jnp.concatenate([o[h * CHUNK:(h + 1) * CHUNK] for h in range(group_heads)], axis=1))
    for (bi, gi), s in s_t.items():
        s_ref[bi, :, gi * gw:(gi + 1) * gw] = s
    return [jnp.concatenate([jnp.concatenate(grp, axis=0) for grp in row_outs], axis=1) for row_outs in outs]


def _silu(x):
    return x * _sigmoid(x)


def _gla_kernel(qk_ref, v_ref, small_ref, og_ref, gkup_ref, gkb_ref, nw_ref, tril_ref, y_ref, s_ref):
    qkvg = []
    for bi in range(qk_ref.shape[0]):
        qk = qk_ref[bi].astype(F32)
        x = _mm(small_ref[bi, :, SMALL_MISC:SMALL_MISC + LANE], gkup_ref[...]) + gkb_ref[...]
        g = -_softplus(-x) * (1.0 / GLA_GATE_NORM)
        qkvg.append((qk[:, :GLA_K] * (GLA_DK ** -0.5), qk[:, GLA_K:], v_ref[bi].astype(F32), g))
    outs = _gla_tiles(qkvg, tril_ref[...], s_ref, nw_ref[...], GLA_HEADS, GLA_DK, LANE // GLA_DK,
                      pl.program_id(1) == 0)
    for bi, o in enumerate(outs):
        y_ref[bi] = (o * _silu(og_ref[bi].astype(F32))).astype(y_ref.dtype)


def _hgrn_kernel(q_ref, f_ref, i_ref, og_ref, lb_ref, nw_ref, tril_ref, y_ref, s_ref):
    lb = lb_ref[...]
    qkvg = []
    for bi in range(q_ref.shape[0]):
        z = f_ref[bi].astype(F32)
        forget = lb + (1.0 - lb) * _sigmoid(z)
        g = jnp.log(jnp.maximum(forget, F_TINY))
        qkvg.append((q_ref[bi].astype(F32), (1.0 - lb) * _sigmoid(-z), i_ref[bi].astype(F32), g))
    outs = _gla_tiles(qkvg, tril_ref[...], s_ref, nw_ref[...], HG_HEADS, HG_DK, 1, pl.program_id(1) == 0)
    for bi, o in enumerate(outs):
        y_ref[bi] = (o * _silu(og_ref[bi].astype(F32))).astype(y_ref.dtype)


def _pcol(name, nb, tt):
    ct = (name if isinstance(name, int) else _K_OFF[name][0] // IN_TN)
    return pl.BlockSpec((None, nb, tt, IN_TN), lambda b, t: (ct, b, t, 0))


def _const_spec(shape):
    nd = len(shape)
    return pl.BlockSpec(shape, lambda b, t: (0,) * nd)


def _seq_params():
    return pltpu.CompilerParams(dimension_semantics=("arbitrary", "arbitrary"), vmem_limit_bytes=VMEM_LIMIT)


def _gla_mixer(p, gk_up_pad, gk_bias, norm_w, tril, tt, nb):
    _, bsz, tp, _ = p.shape
    col = functools.partial(_pcol, nb=nb, tt=tt)
    y = pl.pallas_call(
        _gla_kernel,
        grid=(bsz // nb, tp // tt),
        in_specs=[col(CT_QK), col("gla_v"), col(CT_SMALL), col("gla_g"),
                  _const_spec(gk_up_pad.shape), _const_spec(gk_bias.shape),
                  _const_spec(norm_w.shape), _const_spec(tril.shape)],
        out_specs=pl.BlockSpec((nb, tt, GLA_V), lambda b, t: (b, t, 0)),
        out_shape=jax.ShapeDtypeStruct((bsz, tp, GLA_V), BF16),
        scratch_shapes=[pltpu.VMEM((nb, GLA_DV, GLA_K), F32)],
        compiler_params=_seq_params(),
        name="gla_mixer",
    )(p, p, p, p, gk_up_pad, gk_bias, norm_w, tril)
    return y.reshape(bsz * tp, GLA_V)


def _hgrn_mixer(p, lb, norm_w, tril, tt, nb):
    _, bsz, tp, _ = p.shape
    col = functools.partial(_pcol, nb=nb, tt=tt)
    y = pl.pallas_call(
        _hgrn_kernel,
        grid=(bsz // nb, tp // tt),
        in_specs=[col("hg_q"), col("hg_f"), col("hg_i"), col("hg_g"),
                  _const_spec(lb.shape), _const_spec(norm_w.shape), _const_spec(tril.shape)],
        out_specs=pl.BlockSpec((nb, tt, HG_V), lambda b, t: (b, t, 0)),
        out_shape=jax.ShapeDtypeStruct((bsz, tp, HG_V), BF16),
        scratch_shapes=[pltpu.VMEM((nb, HG_DV, HG_K), F32)],
        compiler_params=_seq_params(),
        name="hgrn_mixer",
    )(p, p, p, p, lb, norm_w, tril)
    return y.reshape(bsz * tp, HG_V)


def _unit_lower_inverses(n_list, blk16, blk32):
    size = n_list[0].shape[0]
    ri = lax.broadcasted_iota(jnp.int32, (size, size), 0)
    ci = lax.broadcasted_iota(jnp.int32, (size, size), 1)
    eye = jnp.where(ri == ci, 1.0, 0.0)
    n16 = [jnp.where(blk16, n, 0.0) for n in n_list]
    n32 = [jnp.where(blk32 & jnp.logical_not(blk16), n, 0.0).astype(BF16) for n in n_list]
    n64 = [jnp.where(blk32, 0.0, n).astype(BF16) for n in n_list]
    t = [eye + n for n in n16]
    m = [n.astype(BF16) for n in n16]
    for it in range(3):
        m = [_mm(x, x).astype(BF16) for x in m]
        t = [ti + _mm(mi, ti) for mi, ti in zip(m, t)]
    for off_diag in (n32, n64):
        tb = [ti.astype(BF16) for ti in t]
        w = [_mm(ti, ni).astype(BF16) for ti, ni in zip(tb, off_diag)]
        t = [ti + _mm(wi, tbi) for ti, wi, tbi in zip(t, w, tb)]
    return t


def _rwkv_kernel(has_vres, *refs):
    if has_vres:
        (r_ref, k_ref, v_ref, small_ref, vf_ref,
         mu_r_ref, mu_k_ref, mu_v_ref, mu_small_ref,
         w0_ref, w2_ref, a0_ref, a2_ref, g2_ref, v0_ref, v2_ref,
         kk_ref, ka_ref, rk_ref, lnw_ref, lnb_ref, seg_ref, tril_ref,
         y_ref, s_ref, carry_ref) = refs
    else:
        (r_ref, k_ref, v_ref, small_ref,
         mu_r_ref, mu_k_ref, mu_v_ref, mu_small_ref,
         w0_ref, w2_ref, a0_ref, a2_ref, g2_ref,
         kk_ref, ka_ref, rk_ref, lnw_ref, lnb_ref, seg_ref, tril_ref,
         y_ref, vf_out_ref, s_ref, carry_ref) = refs

    first = pl.program_id(1) == 0
    nb, tt = r_ref.shape[0], r_ref.shape[1]

    @pl.when(first)
    def _():
        s_ref[...] = jnp.zeros_like(s_ref)
        carry_ref[...] = jnp.zeros_like(carry_ref)

    seg = seg_ref[...]
    tiles = {}

    def prologue(bi):
        srcs = (r_ref, k_ref, v_ref, small_ref)
        mus = (mu_r_ref, mu_k_ref, mu_v_ref, mu_small_ref)
        shifted = []
        off = 0
        crow = bi * SUBLANE
        for src, mu in zip(srcs, mus):
            p = src[bi].astype(F32)
            width = p.shape[1]
            prev_row = carry_ref[crow:crow + 1, off:off + width]
            rolled = pltpu.roll(p, 1, 0)
            row = lax.broadcasted_iota(jnp.int32, p.shape, 0)
            prev = jnp.where(row == 0, prev_row, rolled)
            shifted.append(p + (prev - p) * mu[...])
            carry_ref[crow:crow + 1, off:off + width] = p[tt - 1:tt, :]
            off += width
        r, k, v, small = shifted
        s_g = small[:, SMALL_G:SMALL_G + LANE]
        s_wa = small[:, SMALL_WA:SMALL_WA + LANE]
        s_misc = small[:, SMALL_MISC:SMALL_MISC + LANE]

        w_log = -_softplus(-(w0_ref[...] + _mm(jnp.tanh(s_wa), w2_ref[...]))) - 0.5
        lw = -jnp.exp(w_log)
        a = _sigmoid(a0_ref[...] + _mm(s_wa, a2_ref[...]))
        g = _mm(_sigmoid(s_g), g2_ref[...])
        if has_vres:
            v = v + (vf_ref[bi] - v) * _sigmoid(v0_ref[...] + _mm(s_misc, v2_ref[...]))
        else:
            vf_out_ref[bi] = v
        kk = k * kk_ref[...]
        k = k * (1.0 + (a - 1.0) * ka_ref[...])
        sums = _head_sums(jnp.concatenate([kk * kk, r * k * rk_ref[...]], axis=0), seg)
        kk = kk / jnp.maximum(jnp.sqrt(sums[:tt]), 1e-12)
        c_all = _cumsum_rows(tril_ref[...], lw)
        tiles[bi] = dict(r=r, k=k, v=v, g=g, bonus=sums[tt:] * v, alpha=-kk, beta=kk * a,
                         c=c_all, ce=c_all - lw)

    gs = RW_GROUP
    n_groups = RW_HEADS // RW_GROUP_HEADS
    n_st = RW_GROUP_HEADS * CHUNK
    ri = lax.broadcasted_iota(jnp.int32, (n_st, n_st), 0)
    ci = lax.broadcasted_iota(jnp.int32, (n_st, n_st), 1)
    strict = ci < ri
    blk16 = (ri >> 4) == (ci >> 4)
    blk32 = (ri >> 5) == (ci >> 5)
    tcol = lax.broadcasted_iota(jnp.int32, (CHUNK, n_st), 1) & (CHUNK - 1)
    trow = lax.broadcasted_iota(jnp.int32, (CHUNK, n_st), 0)
    incl = tcol <= trow
    st = functools.partial(_stack_heads, heads=RW_GROUP_HEADS, width=RW_HD)

    for bi in range(nb):
        prologue(bi)
    blocks = _rwkv_blocks(tiles, range(nb), tt, n_groups, gs, st, strict, incl, blk16, blk32)

    n_state = nb * n_groups
    state = [s_ref[si] for si in range(n_state)]
    y_rows = [[] for _ in range(nb)]
    for j in range(tt // CHUNK):
        cur = sorted([b for b in blocks if b["j"] == j], key=lambda b: b["si"])
        sb = [state[b["si"]].astype(BF16) for b in cur]
        xs = [_mm_nt(b["aem"], s) + b["w1"] for b, s in zip(cur, sb)]
        us = [_mm(b["t_inv"], x).astype(BF16) for b, x in zip(cur, xs)]
        for b, u in zip(cur, us):
            state[b["si"]] = state[b["si"]] * b["dec"] + _mm_tn(u, b["bem"]) + b["vk"]
        ys = [_mm_nt(b["rs"], s) + _mm(b["a_rb"], u) + b["y0"] for b, s, u in zip(cur, sb, us)]
        for bi in range(nb):
            y_rows[bi].append(jnp.concatenate(ys[bi * n_groups:(bi + 1) * n_groups], axis=1))
    for si in range(n_state):
        s_ref[si] = state[si]

    inv_hd = 1.0 / RW_HD
    for bi in range(nb):
        y = jnp.concatenate(y_rows[bi], axis=0)
        mean = _head_sums(y, seg) * inv_hd
        yc = y - mean
        var = _head_sums(yc * yc, seg) * inv_hd
        y = yc * lax.rsqrt(var + RW_GN_EPS) * lnw_ref[...] + lnb_ref[...]
        y_ref[bi] = ((y + tiles[bi]["bonus"]) * tiles[bi]["g"]).astype(y_ref.dtype)


def _rwkv_blocks(tiles, batch_rows, tt, n_groups, gs, st, strict, incl, blk16, blk32):
    blocks = []
    for bi, j in [(bi, j) for j in range(tt // CHUNK) for bi in batch_rows]:
        rows = slice(j * CHUNK, (j + 1) * CHUNK)
        tl = tiles[bi]
        c, ce = tl["c"][rows], tl["ce"][rows]
        rj, kj, vj, al, be_ = tl["r"][rows], tl["k"][rows], tl["v"][rows], tl["alpha"][rows], tl["beta"][rows]
        mid = c[CHUNK // 2 - 1:CHUNK // 2, :]
        c_last = c[CHUNK - 1:CHUNK, :]
        e_out = jnp.exp(mid - c)
        e_end = jnp.exp(c_last - c)
        at = al * jnp.exp(ce - mid)
        rt = rj * jnp.exp(c - mid)
        bh = be_ * e_out
        kh = kj * e_out
        ae = al * jnp.exp(ce)
        rs = rj * jnp.exp(c)
        be = be_ * e_end
        ke = kj * e_end
        dec = jnp.exp(c_last)
        for gi in range(n_groups):
            sl = slice(gi * gs, (gi + 1) * gs)
            stb = lambda a_: st(a_[:, sl].astype(BF16))
            blocks.append(dict(j=j, si=bi * n_groups + gi, bm=stb(bh), km=stb(kh), am=stb(at), v_bd=stb(vj),
                               aem=stb(ae),
                               bem=stb(be), kem=stb(ke), rt=rt[:, sl].astype(BF16),
                               rs=rs[:, sl].astype(BF16), dec=dec[:, sl]))
    for b in blocks:
        b["n_ab"] = jnp.where(strict, _mm_nt(b["am"], b["bm"]), 0.0)
    for b in blocks:
        b["n_ak"] = jnp.where(strict, _mm_nt(b["am"], b["km"]), 0.0).astype(BF16)
    for b in blocks:
        b["a_rb"] = jnp.where(incl, _mm_nt(b["rt"], b["bm"]), 0.0).astype(BF16)
        b["a_rk"] = jnp.where(incl, _mm_nt(b["rt"], b["km"]), 0.0).astype(BF16)
    for b, t_inv in zip(blocks, _unit_lower_inverses([b["n_ab"] for b in blocks], blk16, blk32)):
        b["t_inv"] = t_inv.astype(BF16)
    for b in blocks:
        b["w1"] = _mm(b["n_ak"], b["v_bd"])
        b["y0"] = _mm(b["a_rk"], b["v_bd"])
        b["vk"] = _mm_tn(b["v_bd"], b["kem"])
    return blocks


def _rwkv_mixer(p, v_first, mus, prm, seg, tril, tt, nb):
    _, bsz, tp, _ = p.shape
    nt = tp // tt
    has_vres = v_first is not None
    row_spec = pl.BlockSpec((nb, tt, RW_DIM), lambda b, t: (b, t, 0))
    col = functools.partial(_pcol, nb=nb, tt=tt)
    in_specs = [col("rw_r"), col("rw_k"), col("rw_v"), col(CT_SMALL)]
    args = [p, p, p, p]
    if has_vres:
        in_specs.append(row_spec)
        args.append(v_first)
    consts = list(mus) + [prm["w0"], prm["w2"], prm["a0"], prm["a2"], prm["g2"]]
    if has_vres:
        consts += [prm["v0"], prm["v2"]]
    consts += [prm["kk"], prm["ka"], prm["rk"], prm["lnw"], prm["lnb"], seg, tril]
    in_specs += [_const_spec(c.shape) for c in consts]
    args += consts
    out_shape = [jax.ShapeDtypeStruct((bsz, tp, RW_DIM), BF16)]
    out_specs = [row_spec]
    if not has_vres:
        out_shape.append(jax.ShapeDtypeStruct((bsz, tp, RW_DIM), F32))
        out_specs.append(row_spec)
    carry_w = 3 * RW_DIM + IN_TN
    res = pl.pallas_call(
        functools.partial(_rwkv_kernel, has_vres),
        grid=(bsz // nb, nt),
        in_specs=in_specs,
        out_specs=out_specs,
        out_shape=out_shape,
        scratch_shapes=[pltpu.VMEM((nb * (RW_HEADS // RW_GROUP_HEADS), RW_GROUP, RW_GROUP), F32),
                        pltpu.VMEM((nb * SUBLANE, carry_w), F32)],
        compiler_params=_seq_params(),
        name="rwkv_mixer",
    )(*args)
    y = res[0].reshape(bsz * tp, RW_DIM)
    return (y, v_first) if has_vres else (y, res[1])


def _merge_kernel(z_ref, gg_ref, gr_ref, gh_ref, yg_ref, yr_ref, yh_ref, wa_ref, wb_ref, wc_ref, wo_ref, o_ref):
    d = lambda y, w: jnp.dot(y[...], w[...], preferred_element_type=F32)
    sg = lambda g: _sigmoid(jnp.concatenate([g[t] for t in range(g.shape[0])], axis=1).astype(F32))
    m = sg(gg_ref) * d(yg_ref, wa_ref) + sg(gr_ref) * d(yr_ref, wb_ref) + sg(gh_ref) * d(yh_ref, wc_ref)
    o_ref[...] = z_ref[...] + _mm(m, wo_ref[...])


def _merge(z2d, p, y_gla, y_rw, y_hg, wa, wb, wc, wo, tm):
    n = z2d.shape[0]
    row = lambda w: pl.BlockSpec((tm, w), lambda i: (i, 0))
    gate = lambda name: pl.BlockSpec((D_MODEL // IN_TN, tm, IN_TN),
                                     lambda i, cb=_K_OFF[name][0] // D_MODEL: (cb, i, 0))
    full = lambda a: pl.BlockSpec(a.shape, lambda i: (0, 0))
    return pl.pallas_call(
        _merge_kernel,
        grid=(n // tm,),
        in_specs=[row(D_MODEL), gate("gate_gla"), gate("gate_rw"), gate("gate_hg"),
                  row(GLA_V), row(RW_DIM), row(HG_V), full(wa), full(wb), full(wc), full(wo)],
        out_specs=row(D_MODEL),
        out_shape=jax.ShapeDtypeStruct((n, D_MODEL), F32),
        compiler_params=pltpu.CompilerParams(dimension_semantics=("arbitrary",), vmem_limit_bytes=VMEM_LIMIT),
        name="merge",
    )(z2d, p, p, p, y_gla, y_rw, y_hg, wa, wb, wc, wo)


FFN_FC = 256
FFN_HALO = SUBLANE


def _ffn_kernel(final, z_ref, halo_ref, nw_ref, wu_ref, cw_ref, cb_ref, wd_ref, pnw_ref, o_ref, *rest):
    h_ref, act_ref = (None, rest[0]) if final else rest
    tt = z_ref.shape[0]
    zt = z_ref[...]
    nw = nw_ref[...]
    h = jnp.concatenate([_rms(halo_ref[...], nw), _rms(zt, nw)], axis=0).astype(BF16)
    has_prev = pl.program_id(1) > 0
    fc = FFN_FC
    n_chunks = D_FF // fc
    cols = lambda j, half: slice(half * D_FF + j * fc, half * D_FF + (j + 1) * fc)

    def up(j):
        return (jnp.dot(h, wu_ref[:, cols(j, 0)], preferred_element_type=F32),
                jnp.dot(h, wu_ref[:, cols(j, 1)], preferred_element_type=F32))

    def conv(u, cw, cb):
        u = jnp.concatenate([jnp.where(has_prev, u[:FFN_HALO], 0.0), u[FFN_HALO:]], axis=0)
        inner = u * cw[1:2, :] + pltpu.roll(u * cw[0:1, :], 1, 0)
        c = cb + u * cw[2:3, :] + pltpu.roll(inner, 1, 0)
        return c[FFN_HALO:, :]

    split = (n_chunks + 1) // 2
    out = zt
    u_next = up(0)
    for j in range(n_chunks):
        ug, uv = u_next
        if j + 1 < n_chunks:
            u_next = up(j + 1)
        cg = conv(ug, cw_ref[:, cols(j, 0)], cb_ref[:, cols(j, 0)])
        cv = conv(uv, cw_ref[:, cols(j, 1)], cb_ref[:, cols(j, 1)])
        act_ref[:, j * fc:(j + 1) * fc] = (cg * cv / (1.0 + jnp.exp(-cg))).astype(BF16)
        if j + 1 in (split, n_chunks):
            lo = 0 if j + 1 == split else split
            out = out + jnp.dot(act_ref[:, lo * fc:(j + 1) * fc], wd_ref[lo * fc:(j + 1) * fc, :],
                                preferred_element_type=F32)
    if final:
        o_ref[...] = _rms(out, pnw_ref[...])
    else:
        o_ref[...] = out
        h_ref[...] = _rms(out, pnw_ref[...]).astype(h_ref.dtype)


def _ffn(z3d, norm_w, wu, cw, cb, wd, post_norm_w, tt, final):
    bsz, tp, _ = z3d.shape
    hb = tt // FFN_HALO
    once = lambda a: pl.BlockSpec(a.shape, lambda b, t: (0,) * a.ndim, pipeline_mode=pl.Buffered(1))
    tile = pl.BlockSpec((None, tt, D_MODEL), lambda b, t: (b, t, 0))
    out_shape = [jax.ShapeDtypeStruct((bsz, tp, D_MODEL), F32)]
    if not final:
        out_shape.append(jax.ShapeDtypeStruct((bsz, tp, D_MODEL), BF16))
    res = pl.pallas_call(
        functools.partial(_ffn_kernel, final),
        grid=(bsz, tp // tt),
        in_specs=[tile,
                  pl.BlockSpec((None, FFN_HALO, D_MODEL), lambda b, t: (b, jnp.maximum(t * hb - 1, 0), 0)),
                  once(norm_w), once(wu), once(cw), once(cb), once(wd), once(post_norm_w)],
        out_specs=[tile] * len(out_shape),
        out_shape=out_shape,
        scratch_shapes=[pltpu.VMEM((tt, D_FF), BF16)],
        compiler_params=_seq_params(),
        name="conv_ffn",
    )(z3d, z3d, norm_w, wu, cw, cb, wd, post_norm_w)
    return res[0] if final else (res[0], res[1])


def _largest_tile(n, cands):
    for c in cands:
        if n % c == 0:
            return c
    raise ValueError(f"no tile for {n}")


def _ref_cols(w, name):
    off, width = _REF_OFF[name]
    return w[..., off:off + width]


def _pack_columns(w, vres):
    lead = w.shape[:-1]
    zeros = lambda n: jnp.zeros(lead + (n,), w.dtype)
    vr = vres if vres is not None else zeros(RW_V_RANK)
    parts = []
    for name, width in _K_LAYOUT:
        if name == "rw_wa":
            parts.append(jnp.concatenate([_ref_cols(w, "rw_w"), _ref_cols(w, "rw_a")], axis=-1))
        elif name == "misc":
            parts.append(jnp.concatenate(
                [_ref_cols(w, "gla_gk"), vr, zeros(width - GLA_GATE_RANK - RW_V_RANK)], axis=-1))
        else:
            parts.append(_ref_cols(w, name))
    used = sum(wd for _, wd in _K_LAYOUT)
    parts.append(zeros(WP - used))
    return jnp.concatenate(parts, axis=-1)


def _pad_rows(w, lo, total):
    return jnp.pad(w, ((lo, total - lo - w.shape[0]), (0, 0)))


def kernel(x, meta, mix_norm, w_in, w_in_vres, rw_mu, rw_mu_vres, gla_gk_up, gla_gk_bias, gla_norm, rw_w0,
           rw_w2, rw_a0, rw_a2, rw_v0, rw_v2, rw_g2, rw_kk, rw_ka, rw_rk, rw_ln_w, rw_ln_b, hg_lb_logits,
           hg_norm, w_out_gla, w_out_rw, w_out_hg, w_out, ffn_norm, w_up, conv_w, conv_b, w_down, final_norm):
    bsz, seq, _ = x.shape
    depth = w_in.shape[0]
    t_real = seq + N_META
    tp = -(-t_real // CHUNK) * CHUNK
    n = bsz * tp
    tm = _largest_tile(n, (3072, 2048, 1024, 512, 256, 128, 64))
    tm_merge = _largest_tile(n, (512, 256, 128, 64))
    tt = _largest_tile(tp, (704, 512, 384, 256, 192, 128, 64))
    tt_mix = _largest_tile(tp, (3 * CHUNK, 2 * CHUNK, CHUNK))
    nb_mix = _largest_tile(bsz, (4, 2, 1))

    z = jnp.concatenate([jnp.broadcast_to(meta.astype(x.dtype)[None], (bsz, N_META, D_MODEL)), x,
                         jnp.zeros((bsz, tp - t_real, D_MODEL), x.dtype)], axis=1)

    row = lambda a: a.reshape(1, -1).astype(F32)
    t_idx = jnp.arange(tt_mix)
    tril = ((t_idx[:, None] // CHUNK == t_idx[None, :] // CHUNK)
            & (t_idx[None, :] <= t_idx[:, None])).astype(BF16)
    head_of = jnp.arange(RW_GROUP) // RW_HD
    seg = (head_of[:, None] == head_of[None, :]).astype(BF16)
    lb_p = jax.nn.softmax(hg_lb_logits.astype(F32), axis=0)
    lb_all = jnp.cumsum(lb_p, axis=0) - lb_p[0:1]

    nj = D_FF // FFN_FC
    v_first = None
    h_next = None
    for i in range(depth):
        vres_w = w_in_vres[i - 1] if i > 0 else None
        vres_mu = rw_mu_vres[i - 1] if i > 0 else None
        w_cat = _pack_columns(w_in[i], vres_w).astype(BF16)
        mu_full = jnp.concatenate([jnp.zeros((W_IN - rw_mu.shape[1],), F32), rw_mu[i].astype(F32)])
        mu_cat = _pack_columns(mu_full, vres_mu).reshape(N_CT, 1, IN_TN)
        mus = [mu_cat[_K_OFF[nm][0] // IN_TN] for nm in ("rw_r", "rw_k", "rw_v", "rw_g")]

        z2d = z.reshape(n, D_MODEL)
        p = _inproj(z2d if h_next is None else h_next.reshape(n, D_MODEL), row(mix_norm[i]), w_cat, tm)

        gk_up_pad = _pad_rows(gla_gk_up[i], MISC_GK, LANE).astype(BF16)
        p4 = p.reshape(N_CT, bsz, tp, IN_TN)
        y_gla = _gla_mixer(p4, gk_up_pad, row(gla_gk_bias[i]), row(gla_norm[i]), tril, tt_mix, nb_mix)
        y_hg = _hgrn_mixer(p4, row(lb_all[i]), row(hg_norm[i]), tril, tt_mix, nb_mix)

        prm = {
            "w0": row(rw_w0[i]), "w2": _pad_rows(rw_w2[i], 0, LANE).astype(BF16),
            "a0": row(rw_a0[i]), "a2": _pad_rows(rw_a2[i], RW_W_RANK, LANE).astype(BF16),
            "g2": rw_g2[i].astype(BF16),
            "kk": row(rw_kk[i]), "ka": row(rw_ka[i]), "rk": row(rw_rk[i]),
            "lnw": row(rw_ln_w[i]), "lnb": row(rw_ln_b[i]),
        }
        if i > 0:
            prm["v0"] = row(rw_v0[i - 1])
            prm["v2"] = _pad_rows(rw_v2[i - 1], MISC_VR, LANE).astype(BF16)
        y_rw, v_first = _rwkv_mixer(p4, v_first, mus, prm, seg, tril, tt_mix, nb_mix)

        z2d = _merge(z2d, p, y_gla, y_rw, y_hg, w_out_gla[i].astype(BF16), w_out_rw[i].astype(BF16),
                     w_out_hg[i].astype(BF16), w_out[i].astype(BF16), tm_merge)

        cw = jnp.pad(conv_w[i].astype(F32), ((0, SUBLANE - CONV_W), (0, 0)))
        last = i == depth - 1
        res = _ffn(z2d.reshape(bsz, tp, D_MODEL), row(ffn_norm[i]), w_up[i].astype(BF16), cw, row(conv_b[i]),
                   w_down[i].astype(BF16), row(final_norm if last else mix_norm[i + 1]), tt, final=last)
        z, h_next = (res, None) if last else res
    return z[:, N_META:t_real]
```

```python
import functools

import jax
import jax.numpy as jnp
from jax import lax
from jax.experimental import pallas as pl
from jax.experimental.pallas import tpu as pltpu

F32 = jnp.float32
BF16 = jnp.bfloat16

D_MODEL = 1024
N_META = 16
F_TINY = 1e-30
NORM_EPS = 1e-6

GLA_HEADS, GLA_DK, GLA_DV = 4, 64, 128
GLA_K, GLA_V = GLA_HEADS * GLA_DK, GLA_HEADS * GLA_DV
GLA_GATE_RANK = 16
GLA_GATE_NORM = 16.0

RW_HEADS, RW_HD = 8, 64
RW_DIM = RW_HEADS * RW_HD
RW_W_RANK, RW_A_RANK, RW_V_RANK, RW_G_RANK = 64, 64, 32, 128
RW_GN_EPS = 64e-5
RW_GROUP_HEADS = 4
RW_GROUP = RW_GROUP_HEADS * RW_HD

HG_HEADS, HG_DK, HG_DV = 4, 128, 128
HG_K, HG_V = HG_HEADS * HG_DK, HG_HEADS * HG_DV

D_FF = 2816
CONV_W = 3

_REF_LAYOUT = (
    ("gla_q", GLA_K), ("gla_k", GLA_K), ("gla_v", GLA_V), ("gla_gk", GLA_GATE_RANK), ("gla_g", GLA_V),
    ("hg_q", HG_K), ("hg_f", HG_K), ("hg_i", HG_V), ("hg_g", HG_V),
    ("gate_gla", D_MODEL), ("gate_rw", D_MODEL), ("gate_hg", D_MODEL),
    ("rw_r", RW_DIM), ("rw_w", RW_W_RANK), ("rw_k", RW_DIM), ("rw_v", RW_DIM), ("rw_a", RW_A_RANK),
    ("rw_g", RW_G_RANK),
)
_REF_OFF = {}
_o = 0
for _n, _w in _REF_LAYOUT:
    _REF_OFF[_n] = (_o, _w)
    _o += _w
W_IN = _o

LANE = 128
SUBLANE = 8
_K_LAYOUT = (
    ("gate_gla", 1024), ("gate_rw", 1024), ("gate_hg", 1024),
    ("gla_v", 512), ("gla_g", 512), ("hg_q", 512), ("hg_f", 512), ("hg_i", 512), ("hg_g", 512),
    ("rw_r", 512), ("rw_k", 512), ("rw_v", 512),
    ("gla_q", 256), ("gla_k", 256),
    ("rw_g", 128), ("rw_wa", 128), ("misc", 128),
)
_K_OFF = {}
_o = 0
for _n, _w in _K_LAYOUT:
    assert _o % _w == 0
    _K_OFF[_n] = (_o, _w)
    _o += _w
IN_TN = 512
WP = -(-_o // IN_TN) * IN_TN
N_CT = WP // IN_TN
CT_QK = _K_OFF["gla_q"][0] // IN_TN
CT_SMALL = _K_OFF["rw_g"][0] // IN_TN
assert _K_OFF["gla_k"][0] // IN_TN == CT_QK and _K_OFF["misc"][0] // IN_TN == CT_SMALL
SMALL_G, SMALL_WA, SMALL_MISC = (_K_OFF[_n][0] - CT_SMALL * IN_TN for _n in ("rw_g", "rw_wa", "misc"))
MISC_GK = 0
MISC_VR = GLA_GATE_RANK

CHUNK = 64
SUB = 32
INV_BLOCK = 16
assert CHUNK == 4 * INV_BLOCK
VMEM_LIMIT = 56 * 1024 * 1024


def _mm(a, b):
    return jnp.dot(a.astype(BF16), b.astype(BF16), preferred_element_type=F32)


def _mm_nt(a, b):
    return lax.dot_general(a.astype(BF16), b.astype(BF16), (((1,), (1,)), ((), ())),
                           preferred_element_type=F32)


def _mm_tn(a, b):
    return lax.dot_general(a.astype(BF16), b.astype(BF16), (((0,), (0,)), ((), ())),
                           preferred_element_type=F32)


def _split2(x):
    hi = x.astype(BF16)
    lo = (x - hi.astype(F32)).astype(BF16)
    return hi, lo


def _head_sums(x, seg_group):
    gw = seg_group.shape[0]
    xb = x.astype(BF16)
    return jnp.concatenate([jnp.dot(xb[:, i:i + gw], seg_group, preferred_element_type=F32)
                            for i in range(0, x.shape[1], gw)], axis=1)


def _cumsum_rows(tril_bf16, g):
    w = g.shape[1]
    out = jnp.dot(tril_bf16, jnp.concatenate(_split2(g), axis=1), preferred_element_type=F32)
    return out[:, :w] + out[:, w:]


def _sigmoid(x):
    return 1.0 / (1.0 + jnp.exp(-x))


def _softplus(x):
    return jnp.maximum(x, 0.0) + jnp.log(1.0 + jnp.exp(-jnp.abs(x)))


def _rms(x, w):
    return x * lax.rsqrt(jnp.mean(x * x, axis=-1, keepdims=True) + NORM_EPS) * w


def _stack_heads(x, heads, width):
    lane = lax.broadcasted_iota(jnp.int32, x.shape, 1)
    return jnp.concatenate(
        [jnp.where((lane >= h * width) & (lane < (h + 1) * width), x, 0.0) for h in range(heads)], axis=0)


def _inproj_kernel(z_ref, nw_ref, w_ref, o_ref, *scratch):
    if not scratch:
        h = z_ref[...]
    else:
        h_ref, = scratch

        @pl.when(pl.program_id(1) == 0)
        def _():
            h_ref[...] = _rms(z_ref[...], nw_ref[...]).astype(BF16)

        h = h_ref[...]
    o_ref[...] = jnp.dot(h, w_ref[...], preferred_element_type=F32).astype(o_ref.dtype)


def _inproj(z2d, norm_w, w_bf16, tm):
    n = z2d.shape[0]
    return pl.pallas_call(
        _inproj_kernel,
        grid=(n // tm, N_CT),
        in_specs=[
            pl.BlockSpec((tm, D_MODEL), lambda i, j: (i, 0)),
            pl.BlockSpec((1, D_MODEL), lambda i, j: (0, 0)),
            pl.BlockSpec((D_MODEL, IN_TN), lambda i, j: (0, j)),
        ],
        out_specs=pl.BlockSpec((None, tm, IN_TN), lambda i, j: (j, i, 0)),
        out_shape=jax.ShapeDtypeStruct((N_CT, n, IN_TN), BF16),
        scratch_shapes=[] if z2d.dtype == BF16 else [pltpu.VMEM((tm, D_MODEL), BF16)],
        compiler_params=pltpu.CompilerParams(
            dimension_semantics=("arbitrary", "arbitrary"), vmem_limit_bytes=VMEM_LIMIT),
        name="inproj",
    )(z2d, norm_w, w_bf16)


def _gla_tiles(qkvg, tril_bd, s_ref, norm_w, heads, dk, group_heads, first):
    tt, hk = qkvg[0][0].shape
    assert tt % CHUNK == 0 and CHUNK == 2 * SUB and heads % group_heads == 0
    dv = qkvg[0][2].shape[1] // heads
    nb = len(qkvg)
    gw = group_heads * dk
    n_grp = heads // group_heads

    @pl.when(first)
    def _():
        s_ref[...] = jnp.zeros_like(s_ref)

    c_alls = [_cumsum_rows(tril_bd, g) for _, _, _, g in qkvg]
    top, bot = slice(0, SUB), slice(SUB, CHUNK)
    zeros = jnp.zeros((SUB, gw), F32)
    st = (functools.partial(_stack_heads, heads=group_heads, width=dk) if group_heads > 1 else (lambda a_: a_))
    n_st = group_heads * CHUNK
    causal = (lax.broadcasted_iota(jnp.int32, (n_st, n_st), 1)
              <= lax.broadcasted_iota(jnp.int32, (n_st, n_st), 0))
    chunks = []
    for bi, j, gi in [(bi, j, gi) for j in range(tt // CHUNK) for bi in range(nb) for gi in range(n_grp)]:
        rows = slice(j * CHUNK, (j + 1) * CHUNK)
        lanes = slice(gi * gw, (gi + 1) * gw)
        q, k, v, _ = qkvg[bi]
        c, qj, kj = c_alls[bi][rows, lanes], q[rows, lanes], k[rows, lanes]
        vj = v[rows, gi * group_heads * dv:(gi + 1) * group_heads * dv]
        r0 = c[SUB // 2 - 1:SUB // 2, :]
        r1 = c[SUB + SUB // 2 - 1:SUB + SUB // 2, :]
        c_last = c[CHUNK - 1:CHUNK, :]
        kt0 = jnp.concatenate([kj[top] * jnp.exp(r0 - c[top]), zeros], axis=0)
        kt1 = jnp.concatenate([zeros, kj[bot] * jnp.exp(r1 - c[bot])], axis=0)
        qa = qj * jnp.exp(c - r0)
        qb = jnp.concatenate([zeros, qj[bot] * jnp.exp(c[bot] - r1)], axis=0)
        qs = jnp.concatenate([st(qa), st(qb)], axis=1).astype(BF16)
        ks = jnp.concatenate([st(kt0), st(kt1)], axis=1).astype(BF16)
        vs = jnp.concatenate([vj[:, h * dv:(h + 1) * dv] for h in range(group_heads)], axis=0).astype(BF16)
        chunks.append(dict(bi=bi, gi=gi, qs=qs, ks=ks, vs=vs, qe=st(qj * jnp.exp(c)).astype(BF16),
                           kd=st(kj * jnp.exp(c_last - c)).astype(BF16), dec=jnp.exp(c_last)))
    for ch in chunks:
        ch["a"] = jnp.where(causal, _mm_nt(ch["qs"], ch["ks"]), 0.0)
    for ch in chunks:
        ch["oa"] = _mm(ch["a"], ch["vs"])
        ch["kv"] = _mm_tn(ch["vs"], ch["kd"])
    s_t = {(bi, gi): s_ref[bi, :, gi * gw:(gi + 1) * gw] for bi in range(nb) for gi in range(n_grp)}
    outs = [[[] for _ in range(n_grp)] for _ in range(nb)]
    for ch in chunks:
        key = (ch["bi"], ch["gi"])
        o = ch["oa"] + _mm_nt(ch["qe"], s_t[key])
        s_t[key] = s_t[key] * ch["dec"] + ch["kv"]
        o = o * lax.rsqrt(jnp.mean(o * o, axis=-1, keepdims=True) + NORM_EPS) * norm_w
        outs[ch["bi"]][ch["gi"]].append(
            jnp.concatenate([o[h * CHUNK:(h + 1) * CHUNK] for h in range(group_heads)], axis=1))
    for (bi, gi), s in s_t.items():
        s_ref[bi, :, gi * gw:(gi + 1) * gw] = s
    return [jnp.concatenate([jnp.concatenate(grp, axis=0) for grp in row_outs], axis=1) for row_outs in outs]


def _silu(x):
    return x * _sigmoid(x)


def _gla_kernel(qk_ref, v_ref, small_ref, og_ref, gkup_ref, gkb_ref, nw_ref, tril_ref, y_ref, s_ref):
    qkvg = []
    for bi in range(qk_ref.shape[0]):
        qk = qk_ref[bi].astype(F32)
        x = _mm(small_ref[bi, :, SMALL_MISC:SMALL_MISC + LANE], gkup_ref[...]) + gkb_ref[...]
        g = -_softplus(-x) * (1.0 / GLA_GATE_NORM)
        qkvg.append((qk[:, :GLA_K] * (GLA_DK ** -0.5), qk[:, GLA_K:], v_ref[bi].astype(F32), g))
    outs = _gla_tiles(qkvg, tril_ref[...], s_ref, nw_ref[...], GLA_HEADS, GLA_DK, LANE // GLA_DK,
                      pl.program_id(1) == 0)
    for bi, o in enumerate(outs):
        y_ref[bi] = (o * _silu(og_ref[bi].astype(F32))).astype(y_ref.dtype)


def _hgrn_kernel(q_ref, f_ref, i_ref, og_ref, lb_ref, nw_ref, tril_ref, y_ref, s_ref):
    lb = lb_ref[...]
    qkvg = []
    for bi in range(q_ref.shape[0]):
        z = f_ref[bi].astype(F32)
        forget = lb + (1.0 - lb) * _sigmoid(z)
        g = jnp.log(jnp.maximum(forget, F_TINY))
        qkvg.append((q_ref[bi].astype(F32), 1.0 - forget, i_ref[bi].astype(F32), g))
    outs = _gla_tiles(qkvg, tril_ref[...], s_ref, nw_ref[...], HG_HEADS, HG_DK, 1, pl.program_id(1) == 0)
    for bi, o in enumerate(outs):
        y_ref[bi] = (o * _silu(og_ref[bi].astype(F32))).astype(y_ref.dtype)


def _pcol(name, nb, tt):
    ct = (name if isinstance(name, int) else _K_OFF[name][0] // IN_TN)
    return pl.BlockSpec((None, nb, tt, IN_TN), lambda b, t: (ct, b, t, 0))


def _const_spec(shape):
    nd = len(shape)
    return pl.BlockSpec(shape, lambda b, t: (0,) * nd)


def _seq_params():
    return pltpu.CompilerParams(dimension_semantics=("arbitrary", "arbitrary"), vmem_limit_bytes=VMEM_LIMIT)


def _gla_mixer(p, gk_up_pad, gk_bias, norm_w, tril, tt, nb):
    _, bsz, tp, _ = p.shape
    col = functools.partial(_pcol, nb=nb, tt=tt)
    y = pl.pallas_call(
        _gla_kernel,
        grid=(bsz // nb, tp // tt),
        in_specs=[col(CT_QK), col("gla_v"), col(CT_SMALL), col("gla_g"),
                  _const_spec(gk_up_pad.shape), _const_spec(gk_bias.shape),
                  _const_spec(norm_w.shape), _const_spec(tril.shape)],
        out_specs=pl.BlockSpec((nb, tt, GLA_V), lambda b, t: (b, t, 0)),
        out_shape=jax.ShapeDtypeStruct((bsz, tp, GLA_V), BF16),
        scratch_shapes=[pltpu.VMEM((nb, GLA_DV, GLA_K), F32)],
        compiler_params=_seq_params(),
        name="gla_mixer",
    )(p, p, p, p, gk_up_pad, gk_bias, norm_w, tril)
    return y.reshape(bsz * tp, GLA_V)


def _hgrn_mixer(p, lb, norm_w, tril, tt, nb):
    _, bsz, tp, _ = p.shape
    col = functools.partial(_pcol, nb=nb, tt=tt)
    y = pl.pallas_call(
        _hgrn_kernel,
        grid=(bsz // nb, tp // tt),
        in_specs=[col("hg_q"), col("hg_f"), col("hg_i"), col("hg_g"),
                  _const_spec(lb.shape), _const_spec(norm_w.shape), _const_spec(tril.shape)],
        out_specs=pl.BlockSpec((nb, tt, HG_V), lambda b, t: (b, t, 0)),
        out_shape=jax.ShapeDtypeStruct((bsz, tp, HG_V), BF16),
        scratch_shapes=[pltpu.VMEM((nb, HG_DV, HG_K), F32)],
        compiler_params=_seq_params(),
        name="hgrn_mixer",
    )(p, p, p, p, lb, norm_w, tril)
    return y.reshape(bsz * tp, HG_V)


def _unit_lower_inverses(n_list, blk16, blk32):
    size = n_list[0].shape[0]
    ri = lax.broadcasted_iota(jnp.int32, (size, size), 0)
    ci = lax.broadcasted_iota(jnp.int32, (size, size), 1)
    eye = jnp.where(ri == ci, 1.0, 0.0)
    n16 = [jnp.where(blk16, n, 0.0) for n in n_list]
    n32 = [jnp.where(blk32 & jnp.logical_not(blk16), n, 0.0).astype(BF16) for n in n_list]
    n64 = [jnp.where(blk32, 0.0, n).astype(BF16) for n in n_list]
    t = [eye + n for n in n16]
    m = [n.astype(BF16) for n in n16]
    for it in range(3):
        m = [_mm(x, x).astype(BF16) for x in m]
        t = [ti + _mm(mi, ti) for mi, ti in zip(m, t)]
    for off_diag in (n32, n64):
        tb = [ti.astype(BF16) for ti in t]
        w = [_mm(ti, ni).astype(BF16) for ti, ni in zip(tb, off_diag)]
        t = [ti + _mm(wi, tbi) for ti, wi, tbi in zip(t, w, tb)]
    return t


def _rwkv_kernel(has_vres, *refs):
    if has_vres:
        (r_ref, k_ref, v_ref, small_ref, vf_ref,
         mu_r_ref, mu_k_ref, mu_v_ref, mu_small_ref,
         w0_ref, w2_ref, a0_ref, a2_ref, g2_ref, v0_ref, v2_ref,
         kk_ref, ka_ref, rk_ref, lnw_ref, lnb_ref, seg_ref, tril_ref,
         y_ref, s_ref, carry_ref) = refs
    else:
        (r_ref, k_ref, v_ref, small_ref,
         mu_r_ref, mu_k_ref, mu_v_ref, mu_small_ref,
         w0_ref, w2_ref, a0_ref, a2_ref, g2_ref,
         kk_ref, ka_ref, rk_ref, lnw_ref, lnb_ref, seg_ref, tril_ref,
         y_ref, vf_out_ref, s_ref, carry_ref) = refs

    first = pl.program_id(1) == 0
    nb, tt = r_ref.shape[0], r_ref.shape[1]

    @pl.when(first)
    def _():
        s_ref[...] = jnp.zeros_like(s_ref)
        carry_ref[...] = jnp.zeros_like(carry_ref)

    seg = seg_ref[...]
    tiles = {}

    def prologue(bi):
        srcs = (r_ref, k_ref, v_ref, small_ref)
        mus = (mu_r_ref, mu_k_ref, mu_v_ref, mu_small_ref)
        shifted = []
        off = 0
        crow = bi * SUBLANE
        for src, mu in zip(srcs, mus):
            p = src[bi].astype(F32)
            width = p.shape[1]
            prev_row = carry_ref[crow:crow + 1, off:off + width]
            rolled = pltpu.roll(p, 1, 0)
            row = lax.broadcasted_iota(jnp.int32, p.shape, 0)
            prev = jnp.where(row == 0, prev_row, rolled)
            shifted.append(p + (prev - p) * mu[...])
            carry_ref[crow:crow + 1, off:off + width] = p[tt - 1:tt, :]
            off += width
        r, k, v, small = shifted
        s_g = small[:, SMALL_G:SMALL_G + LANE]
        s_wa = small[:, SMALL_WA:SMALL_WA + LANE]
        s_misc = small[:, SMALL_MISC:SMALL_MISC + LANE]

        w_log = -_softplus(-(w0_ref[...] + _mm(jnp.tanh(s_wa), w2_ref[...]))) - 0.5
        lw = -jnp.exp(w_log)
        a = _sigmoid(a0_ref[...] + _mm(s_wa, a2_ref[...]))
        g = _mm(_sigmoid(s_g), g2_ref[...])
        if has_vres:
            v = v + (vf_ref[bi] - v) * _sigmoid(v0_ref[...] + _mm(s_misc, v2_ref[...]))
        else:
            vf_out_ref[bi] = v
        kk = k * kk_ref[...]
        k = k * (1.0 + (a - 1.0) * ka_ref[...])
        sums = _head_sums(jnp.concatenate([kk * kk, r * k * rk_ref[...]], axis=0), seg)
        kk = kk / jnp.maximum(jnp.sqrt(sums[:tt]), 1e-12)
        c_all = _cumsum_rows(tril_ref[...], lw)
        tiles[bi] = dict(r=r, k=k, v=v, g=g, bonus=sums[tt:] * v, alpha=-kk, beta=kk * a,
                         c=c_all, ce=c_all - lw)

    gs = RW_GROUP
    n_groups = RW_HEADS // RW_GROUP_HEADS
    n_st = RW_GROUP_HEADS * CHUNK
    ri = lax.broadcasted_iota(jnp.int32, (n_st, n_st), 0)
    ci = lax.broadcasted_iota(jnp.int32, (n_st, n_st), 1)
    strict = ci < ri
    sh = INV_BLOCK.bit_length() - 1
    blk16 = (ri >> sh) == (ci >> sh)
    blk32 = (ri >> (sh + 1)) == (ci >> (sh + 1))
    tcol = lax.broadcasted_iota(jnp.int32, (CHUNK, n_st), 1) & (CHUNK - 1)
    trow = lax.broadcasted_iota(jnp.int32, (CHUNK, n_st), 0)
    incl = tcol <= trow
    st = functools.partial(_stack_heads, heads=RW_GROUP_HEADS, width=RW_HD)

    for bi in range(nb):
        prologue(bi)
    blocks = _rwkv_blocks(tiles, range(nb), tt, n_groups, gs, st, strict, incl, blk16, blk32)

    n_state = nb * n_groups
    state = [s_ref[si] for si in range(n_state)]
    y_rows = [[] for _ in range(nb)]
    for j in range(tt // CHUNK):
        cur = sorted([b for b in blocks if b["j"] == j], key=lambda b: b["si"])
        sb = [state[b["si"]].astype(BF16) for b in cur]
        xs = [_mm_nt(b["aem"], s) + b["w1"] for b, s in zip(cur, sb)]
        us = [_mm(b["t_inv"], x).astype(BF16) for b, x in zip(cur, xs)]
        for b, u in zip(cur, us):
            state[b["si"]] = state[b["si"]] * b["dec"] + _mm_tn(u, b["bem"]) + b["vk"]
        ys = [_mm_nt(b["rs"], s) + _mm(b["a_rb"], u) + b["y0"] for b, s, u in zip(cur, sb, us)]
        for bi in range(nb):
            y_rows[bi].append(jnp.concatenate(ys[bi * n_groups:(bi + 1) * n_groups], axis=1))
    for si in range(n_state):
        s_ref[si] = state[si]

    inv_hd = 1.0 / RW_HD
    for bi in range(nb):
        y = jnp.concatenate(y_rows[bi], axis=0)
        mean = _head_sums(y, seg) * inv_hd
        yc = y - mean
        var = _head_sums(yc * yc, seg) * inv_hd
        y = yc * lax.rsqrt(var + RW_GN_EPS) * lnw_ref[...] + lnb_ref[...]
        y_ref[bi] = ((y + tiles[bi]["bonus"]) * tiles[bi]["g"]).astype(y_ref.dtype)


def _rwkv_blocks(tiles, batch_rows, tt, n_groups, gs, st, strict, incl, blk16, blk32):
    blocks = []
    for bi, j in [(bi, j) for j in range(tt // CHUNK) for bi in batch_rows]:
        rows = slice(j * CHUNK, (j + 1) * CHUNK)
        tl = tiles[bi]
        c, ce = tl["c"][rows], tl["ce"][rows]
        rj, kj, vj, al, be_ = tl["r"][rows], tl["k"][rows], tl["v"][rows], tl["alpha"][rows], tl["beta"][rows]
        mid = c[CHUNK // 2 - 1:CHUNK // 2, :]
        c_last = c[CHUNK - 1:CHUNK, :]
        e_out = jnp.exp(mid - c)
        e_end = jnp.exp(c_last - c)
        at = al * jnp.exp(ce - mid)
        rt = rj * jnp.exp(c - mid)
        bh = be_ * e_out
        kh = kj * e_out
        ae = al * jnp.exp(ce)
        rs = rj * jnp.exp(c)
        be = be_ * e_end
        ke = kj * e_end
        dec = jnp.exp(c_last)
        for gi in range(n_groups):
            sl = slice(gi * gs, (gi + 1) * gs)
            stb = lambda a_: st(a_[:, sl].astype(BF16))
            blocks.append(dict(j=j, si=bi * n_groups + gi, bm=stb(bh), km=stb(kh), am=stb(at), v_bd=stb(vj),
                               aem=stb(ae),
                               bem=stb(be), kem=stb(ke), rt=rt[:, sl].astype(BF16),
                               rs=rs[:, sl].astype(BF16), dec=dec[:, sl]))
    for b in blocks:
        b["n_ab"] = jnp.where(strict, _mm_nt(b["am"], b["bm"]), 0.0)
    for b in blocks:
        b["n_ak"] = jnp.where(strict, _mm_nt(b["am"], b["km"]), 0.0).astype(BF16)
    for b in blocks:
        b["a_rb"] = jnp.where(incl, _mm_nt(b["rt"], b["bm"]), 0.0).astype(BF16)
        b["a_rk"] = jnp.where(incl, _mm_nt(b["rt"], b["km"]), 0.0).astype(BF16)
    for b, t_inv in zip(blocks, _unit_lower_inverses([b["n_ab"] for b in blocks], blk16, blk32)):
        b["t_inv"] = t_inv.astype(BF16)
    for b in blocks:
        b["w1"] = _mm(b["n_ak"], b["v_bd"])
        b["y0"] = _mm(b["a_rk"], b["v_bd"])
        b["vk"] = _mm_tn(b["v_bd"], b["kem"])
    return blocks


def _rwkv_mixer(p, v_first, mus, prm, seg, tril, tt, nb):
    _, bsz, tp, _ = p.shape
    nt = tp // tt
    has_vres = v_first is not None
    row_spec = pl.BlockSpec((nb, tt, RW_DIM), lambda b, t: (b, t, 0))
    col = functools.partial(_pcol, nb=nb, tt=tt)
    in_specs = [col("rw_r"), col("rw_k"), col("rw_v"), col(CT_SMALL)]
    args = [p, p, p, p]
    if has_vres:
        in_specs.append(row_spec)
        args.append(v_first)
    consts = list(mus) + [prm["w0"], prm["w2"], prm["a0"], prm["a2"], prm["g2"]]
    if has_vres:
        consts += [prm["v0"], prm["v2"]]
    consts += [prm["kk"], prm["ka"], prm["rk"], prm["lnw"], prm["lnb"], seg, tril]
    in_specs += [_const_spec(c.shape) for c in consts]
    args += consts
    out_shape = [jax.ShapeDtypeStruct((bsz, tp, RW_DIM), BF16)]
    out_specs = [row_spec]
    if not has_vres:
        out_shape.append(jax.ShapeDtypeStruct((bsz, tp, RW_DIM), F32))
        out_specs.append(row_spec)
    carry_w = 3 * RW_DIM + IN_TN
    res = pl.pallas_call(
        functools.partial(_rwkv_kernel, has_vres),
        grid=(bsz // nb, nt),
        in_specs=in_specs,
        out_specs=out_specs,
        out_shape=out_shape,
        scratch_shapes=[pltpu.VMEM((nb * (RW_HEADS // RW_GROUP_HEADS), RW_GROUP, RW_GROUP), F32),
                        pltpu.VMEM((nb * SUBLANE, carry_w), F32)],
        compiler_params=_seq_params(),
        name="rwkv_mixer",
    )(*args)
    y = res[0].reshape(bsz * tp, RW_DIM)
    return (y, v_first) if has_vres else (y, res[1])


def _merge_kernel(z_ref, gg_ref, gr_ref, gh_ref, yg_ref, yr_ref, yh_ref, wa_ref, wb_ref, wc_ref, wo_ref, o_ref):
    d = lambda y, w: jnp.dot(y[...], w[...], preferred_element_type=F32)
    sg = lambda g: _sigmoid(jnp.concatenate([g[t] for t in range(g.shape[0])], axis=1).astype(F32))
    m = sg(gg_ref) * d(yg_ref, wa_ref) + sg(gr_ref) * d(yr_ref, wb_ref) + sg(gh_ref) * d(yh_ref, wc_ref)
    o_ref[...] = z_ref[...] + _mm(m, wo_ref[...])


def _merge(z2d, p, y_gla, y_rw, y_hg, wa, wb, wc, wo, tm):
    n = z2d.shape[0]
    row = lambda w: pl.BlockSpec((tm, w), lambda i: (i, 0))
    gate = lambda name: pl.BlockSpec((D_MODEL // IN_TN, tm, IN_TN),
                                     lambda i, cb=_K_OFF[name][0] // D_MODEL: (cb, i, 0))
    full = lambda a: pl.BlockSpec(a.shape, lambda i: (0, 0))
    return pl.pallas_call(
        _merge_kernel,
        grid=(n // tm,),
        in_specs=[row(D_MODEL), gate("gate_gla"), gate("gate_rw"), gate("gate_hg"),
                  row(GLA_V), row(RW_DIM), row(HG_V), full(wa), full(wb), full(wc), full(wo)],
        out_specs=row(D_MODEL),
        out_shape=jax.ShapeDtypeStruct((n, D_MODEL), F32),
        compiler_params=pltpu.CompilerParams(dimension_semantics=("arbitrary",), vmem_limit_bytes=VMEM_LIMIT),
        name="merge",
    )(z2d, p, p, p, y_gla, y_rw, y_hg, wa, wb, wc, wo)


FFN_FC = 256
FFN_HALO = SUBLANE


def _ffn_kernel(final, z_ref, halo_ref, nw_ref, wu_ref, cw_ref, cb_ref, wd_ref, pnw_ref, o_ref, *rest):
    h_ref, act_ref = (None, rest[0]) if final else rest
    tt = z_ref.shape[0]
    zt = z_ref[...]
    nw = nw_ref[...]
    h = jnp.concatenate([_rms(halo_ref[...], nw), _rms(zt, nw)], axis=0).astype(BF16)
    has_prev = pl.program_id(1) > 0
    fc = FFN_FC
    n_chunks = D_FF // fc
    cols = lambda j, half: slice(half * D_FF + j * fc, half * D_FF + (j + 1) * fc)

    def up(j):
        return (jnp.dot(h, wu_ref[:, cols(j, 0)], preferred_element_type=F32),
                jnp.dot(h, wu_ref[:, cols(j, 1)], preferred_element_type=F32))

    def conv(u, cw, cb):
        u = jnp.concatenate([jnp.where(has_prev, u[:FFN_HALO], 0.0), u[FFN_HALO:]], axis=0)
        inner = u * cw[1:2, :] + pltpu.roll(u * cw[0:1, :], 1, 0)
        c = cb + u * cw[2:3, :] + pltpu.roll(inner, 1, 0)
        return c[FFN_HALO:, :]

    split = (n_chunks + 1) // 2
    out = zt
    u_next = up(0)
    for j in range(n_chunks):
        ug, uv = u_next
        if j + 1 < n_chunks:
            u_next = up(j + 1)
        cg = conv(ug, cw_ref[:, cols(j, 0)], cb_ref[:, cols(j, 0)])
        cv = conv(uv, cw_ref[:, cols(j, 1)], cb_ref[:, cols(j, 1)])
        act_ref[:, j * fc:(j + 1) * fc] = (cg * cv / (1.0 + jnp.exp(-cg))).astype(BF16)
        if j + 1 in (split, n_chunks):
            lo = 0 if j + 1 == split else split
            out = out + jnp.dot(act_ref[:, lo * fc:(j + 1) * fc], wd_ref[lo * fc:(j + 1) * fc, :],
                                preferred_element_type=F32)
    if final:
        o_ref[...] = _rms(out, pnw_ref[...])
    else:
        o_ref[...] = out
        h_ref[...] = _rms(out, pnw_ref[...]).astype(h_ref.dtype)


def _ffn(z3d, norm_w, wu, cw, cb, wd, post_norm_w, tt, final):
    bsz, tp, _ = z3d.shape
    hb = tt // FFN_HALO
    once = lambda a: pl.BlockSpec(a.shape, lambda b, t: (0,) * a.ndim, pipeline_mode=pl.Buffered(1))
    tile = pl.BlockSpec((None, tt, D_MODEL), lambda b, t: (b, t, 0))
    out_shape = [jax.ShapeDtypeStruct((bsz, tp, D_MODEL), F32)]
    if not final:
        out_shape.append(jax.ShapeDtypeStruct((bsz, tp, D_MODEL), BF16))
    res = pl.pallas_call(
        functools.partial(_ffn_kernel, final),
        grid=(bsz, tp // tt),
        in_specs=[tile,
                  pl.BlockSpec((None, FFN_HALO, D_MODEL), lambda b, t: (b, jnp.maximum(t * hb - 1, 0), 0)),
                  once(norm_w), once(wu), once(cw), once(cb), once(wd), once(post_norm_w)],
        out_specs=[tile] * len(out_shape),
        out_shape=out_shape,
        scratch_shapes=[pltpu.VMEM((tt, D_FF), BF16)],
        compiler_params=_seq_params(),
        name="conv_ffn",
    )(z3d, z3d, norm_w, wu, cw, cb, wd, post_norm_w)
    return res[0] if final else (res[0], res[1])


def _largest_tile(n, cands):
    for c in cands:
        if n % c == 0:
            return c
    raise ValueError(f"no tile for {n}")


def _ref_cols(w, name):
    off, width = _REF_OFF[name]
    return w[..., off:off + width]


def _pack_columns(w, vres):
    lead = w.shape[:-1]
    zeros = lambda n: jnp.zeros(lead + (n,), w.dtype)
    vr = vres if vres is not None else zeros(RW_V_RANK)
    parts = []
    for name, width in _K_LAYOUT:
        if name == "rw_wa":
            parts.append(jnp.concatenate([_ref_cols(w, "rw_w"), _ref_cols(w, "rw_a")], axis=-1))
        elif name == "misc":
            parts.append(jnp.concatenate(
                [_ref_cols(w, "gla_gk"), vr, zeros(width - GLA_GATE_RANK - RW_V_RANK)], axis=-1))
        else:
            parts.append(_ref_cols(w, name))
    used = sum(wd for _, wd in _K_LAYOUT)
    parts.append(zeros(WP - used))
    return jnp.concatenate(parts, axis=-1)


def _pad_rows(w, lo, total):
    return jnp.pad(w, ((lo, total - lo - w.shape[0]), (0, 0)))


def kernel(x, meta, mix_norm, w_in, w_in_vres, rw_mu, rw_mu_vres, gla_gk_up, gla_gk_bias, gla_norm, rw_w0,
           rw_w2, rw_a0, rw_a2, rw_v0, rw_v2, rw_g2, rw_kk, rw_ka, rw_rk, rw_ln_w, rw_ln_b, hg_lb_logits,
           hg_norm, w_out_gla, w_out_rw, w_out_hg, w_out, ffn_norm, w_up, conv_w, conv_b, w_down, final_norm):
    bsz, seq, _ = x.shape
    depth = w_in.shape[0]
    t_real = seq + N_META
    tp = -(-t_real // CHUNK) * CHUNK
    n = bsz * tp
    tm = _largest_tile(n, (3072, 2048, 1024, 512, 256, 128, 64))
    tm_merge = _largest_tile(n, (512, 256, 128, 64))
    tt = _largest_tile(tp, (704, 512, 384, 256, 192, 128, 64))
    tt_mix = _largest_tile(tp, (3 * CHUNK, 2 * CHUNK, CHUNK))
    nb_mix = _largest_tile(bsz, (4, 2, 1))

    z = jnp.concatenate([jnp.broadcast_to(meta.astype(x.dtype)[None], (bsz, N_META, D_MODEL)), x,
                         jnp.zeros((bsz, tp - t_real, D_MODEL), x.dtype)], axis=1)

    row = lambda a: a.reshape(1, -1).astype(F32)
    t_idx = jnp.arange(tt_mix)
    tril = ((t_idx[:, None] // CHUNK == t_idx[None, :] // CHUNK)
            & (t_idx[None, :] <= t_idx[:, None])).astype(BF16)
    head_of = jnp.arange(RW_GROUP) // RW_HD
    seg = (head_of[:, None] == head_of[None, :]).astype(BF16)
    lb_p = jax.nn.softmax(hg_lb_logits.astype(F32), axis=0)
    lb_all = jnp.cumsum(lb_p, axis=0) - lb_p[0:1]

    nj = D_FF // FFN_FC
    v_first = None
    h_next = None
    for i in range(depth):
        vres_w = w_in_vres[i - 1] if i > 0 else None
        vres_mu = rw_mu_vres[i - 1] if i > 0 else None
        w_cat = _pack_columns(w_in[i], vres_w).astype(BF16)
        mu_full = jnp.concatenate([jnp.zeros((W_IN - rw_mu.shape[1],), F32), rw_mu[i].astype(F32)])
        mu_cat = _pack_columns(mu_full, vres_mu).reshape(N_CT, 1, IN_TN)
        mus = [mu_cat[_K_OFF[nm][0] // IN_TN] for nm in ("rw_r", "rw_k", "rw_v", "rw_g")]

        z2d = z.reshape(n, D_MODEL)
        p = _inproj(z2d if h_next is None else h_next.reshape(n, D_MODEL), row(mix_norm[i]), w_cat, tm)

        gk_up_pad = _pad_rows(gla_gk_up[i], MISC_GK, LANE).astype(BF16)
        p4 = p.reshape(N_CT, bsz, tp, IN_TN)
        y_gla = _gla_mixer(p4, gk_up_pad, row(gla_gk_bias[i]), row(gla_norm[i]), tril, tt_mix, nb_mix)
        y_hg = _hgrn_mixer(p4, row(lb_all[i]), row(hg_norm[i]), tril, tt_mix, nb_mix)

        prm = {
            "w0": row(rw_w0[i]), "w2": _pad_rows(rw_w2[i], 0, LANE).astype(BF16),
            "a0": row(rw_a0[i]), "a2": _pad_rows(rw_a2[i], RW_W_RANK, LANE).astype(BF16),
            "g2": rw_g2[i].astype(BF16),
            "kk": row(rw_kk[i]), "ka": row(rw_ka[i]), "rk": row(rw_rk[i]),
            "lnw": row(rw_ln_w[i]), "lnb": row(rw_ln_b[i]),
        }
        if i > 0:
            prm["v0"] = row(rw_v0[i - 1])
            prm["v2"] = _pad_rows(rw_v2[i - 1], MISC_VR, LANE).astype(BF16)
        y_rw, v_first = _rwkv_mixer(p4, v_first, mus, prm, seg, tril, tt_mix, nb_mix)

        z2d = _merge(z2d, p, y_gla, y_rw, y_hg, w_out_gla[i].astype(BF16), w_out_rw[i].astype(BF16),
                     w_out_hg[i].astype(BF16), w_out[i].astype(BF16), tm_merge)

        cw = jnp.pad(conv_w[i].astype(F32), ((0, SUBLANE - CONV_W), (0, 0)))
        last = i == depth - 1
        res = _ffn(z2d.reshape(bsz, tp, D_MODEL), row(ffn_norm[i]), w_up[i].astype(BF16), cw, row(conv_b[i]),
                   w_down[i].astype(BF16), row(final_norm if last else mix_norm[i + 1]), tt, final=last)
        z, h_next = (res, None) if last else res
    return z[:, N_META:t_real]
```

```python
import functools

import jax
import jax.numpy as jnp
from jax import lax
from jax.experimental import pallas as pl
from jax.experimental.pallas import tpu as pltpu

F32 = jnp.float32
BF16 = jnp.bfloat16

D_MODEL = 1024
N_META = 16
F_TINY = 1e-30
NORM_EPS = 1e-6

GLA_HEADS, GLA_DK, GLA_DV = 4, 64, 128
GLA_K, GLA_V = GLA_HEADS * GLA_DK, GLA_HEADS * GLA_DV
GLA_GATE_RANK = 16
GLA_GATE_NORM = 16.0

RW_HEADS, RW_HD = 8, 64
RW_DIM = RW_HEADS * RW_HD
RW_W_RANK, RW_A_RANK, RW_V_RANK, RW_G_RANK = 64, 64, 32, 128
RW_GN_EPS = 64e-5
RW_GROUP_HEADS = 4
RW_GROUP = RW_GROUP_HEADS * RW_HD

HG_HEADS, HG_DK, HG_DV = 4, 128, 128
HG_K, HG_V = HG_HEADS * HG_DK, HG_HEADS * HG_DV

D_FF = 2816
CONV_W = 3

_REF_LAYOUT = (
    ("gla_q", GLA_K), ("gla_k", GLA_K), ("gla_v", GLA_V), ("gla_gk", GLA_GATE_RANK), ("gla_g", GLA_V),
    ("hg_q", HG_K), ("hg_f", HG_K), ("hg_i", HG_V), ("hg_g", HG_V),
    ("gate_gla", D_MODEL), ("gate_rw", D_MODEL), ("gate_hg", D_MODEL),
    ("rw_r", RW_DIM), ("rw_w", RW_W_RANK), ("rw_k", RW_DIM), ("rw_v", RW_DIM), ("rw_a", RW_A_RANK),
    ("rw_g", RW_G_RANK),
)
_REF_OFF = {}
_o = 0
for _n, _w in _REF_LAYOUT:
    _REF_OFF[_n] = (_o, _w)
    _o += _w
W_IN = _o

LANE = 128
SUBLANE = 8
_K_LAYOUT = (
    ("gate_gla", 1024), ("gate_rw", 1024), ("gate_hg", 1024),
    ("gla_v", 512), ("gla_g", 512), ("hg_q", 512), ("hg_f", 512), ("hg_i", 512), ("hg_g", 512),
    ("rw_r", 512), ("rw_k", 512), ("rw_v", 512),
    ("gla_q", 256), ("gla_k", 256),
    ("rw_g", 128), ("rw_wa", 128), ("misc", 128),
)
_K_OFF = {}
_o = 0
for _n, _w in _K_LAYOUT:
    assert _o % _w == 0
    _K_OFF[_n] = (_o, _w)
    _o += _w
IN_TN = 512
WP = -(-_o // IN_TN) * IN_TN
N_CT = WP // IN_TN
CT_QK = _K_OFF["gla_q"][0] // IN_TN
CT_SMALL = _K_OFF["rw_g"][0] // IN_TN
assert _K_OFF["gla_k"][0] // IN_TN == CT_QK and _K_OFF["misc"][0] // IN_TN == CT_SMALL
SMALL_G, SMALL_WA, SMALL_MISC = (_K_OFF[_n][0] - CT_SMALL * IN_TN for _n in ("rw_g", "rw_wa", "misc"))
MISC_GK = 0
MISC_VR = GLA_GATE_RANK

CHUNK = 64
SUB = 32
INV_BLOCK = 16
assert CHUNK == 4 * INV_BLOCK
VMEM_LIMIT = 56 * 1024 * 1024


def _mm(a, b):
    return jnp.dot(a.astype(BF16), b.astype(BF16), preferred_element_type=F32)


def _mm_nt(a, b):
    return lax.dot_general(a.astype(BF16), b.astype(BF16), (((1,), (1,)), ((), ())),
                           preferred_element_type=F32)


def _mm_tn(a, b):
    return lax.dot_general(a.astype(BF16), b.astype(BF16), (((0,), (0,)), ((), ())),
                           preferred_element_type=F32)


def _split2(x):
    hi = x.astype(BF16)
    lo = (x - hi.astype(F32)).astype(BF16)
    return hi, lo


def _head_sums(x, seg_group):
    gw = seg_group.shape[0]
    xb = x.astype(BF16)
    return jnp.concatenate([jnp.dot(xb[:, i:i + gw], seg_group, preferred_element_type=F32)
                            for i in range(0, x.shape[1], gw)], axis=1)


def _cumsum_rows(tril_bf16, g):
    w = g.shape[1]
    out = jnp.dot(tril_bf16, jnp.concatenate(_split2(g), axis=1), preferred_element_type=F32)
    return out[:, :w] + out[:, w:]


def _sigmoid(x):
    return 1.0 / (1.0 + jnp.exp(-x))


def _softplus(x):
    return jnp.maximum(x, 0.0) + jnp.log(1.0 + jnp.exp(-jnp.abs(x)))


def _rms(x, w):
    return x * lax.rsqrt(jnp.mean(x * x, axis=-1, keepdims=True) + NORM_EPS) * w


def _stack_heads(x, heads, width):
    lane = lax.broadcasted_iota(jnp.int32, x.shape, 1)
    return jnp.concatenate(
        [jnp.where((lane >= h * width) & (lane < (h + 1) * width), x, 0.0) for h in range(heads)], axis=0)


def _inproj_kernel(z_ref, nw_ref, w_ref, o_ref, *scratch):
    if not scratch:
        h = z_ref[...]
    else:
        h_ref, = scratch

        @pl.when(pl.program_id(1) == 0)
        def _():
            h_ref[...] = _rms(z_ref[...], nw_ref[...]).astype(BF16)

        h = h_ref[...]
    o_ref[...] = jnp.dot(h, w_ref[...], preferred_element_type=F32).astype(o_ref.dtype)


def _inproj(z2d, norm_w, w_bf16, tm):
    n = z2d.shape[0]
    return pl.pallas_call(
        _inproj_kernel,
        grid=(n // tm, N_CT),
        in_specs=[
            pl.BlockSpec((tm, D_MODEL), lambda i, j: (i, 0)),
            pl.BlockSpec((1, D_MODEL), lambda i, j: (0, 0)),
            pl.BlockSpec((D_MODEL, IN_TN), lambda i, j: (0, j)),
        ],
        out_specs=pl.BlockSpec((None, tm, IN_TN), lambda i, j: (j, i, 0)),
        out_shape=jax.ShapeDtypeStruct((N_CT, n, IN_TN), BF16),
        scratch_shapes=[] if z2d.dtype == BF16 else [pltpu.VMEM((tm, D_MODEL), BF16)],
        compiler_params=pltpu.CompilerParams(
            dimension_semantics=("arbitrary", "arbitrary"), vmem_limit_bytes=VMEM_LIMIT),
        name="inproj",
    )(z2d, norm_w, w_bf16)


def _gla_tiles(qkvg, tril_bd, s_ref, norm_w, heads, dk, group_heads, first):
    tt, hk = qkvg[0][0].shape
    assert tt % CHUNK == 0 and CHUNK == 2 * SUB and heads % group_heads == 0
    dv = qkvg[0][2].shape[1] // heads
    nb = len(qkvg)
    gw = group_heads * dk
    n_grp = heads // group_heads

    @pl.when(first)
    def _():
        s_ref[...] = jnp.zeros_like(s_ref)

    c_alls = [_cumsum_rows(tril_bd, g) for _, _, _, g in qkvg]
    top, bot = slice(0, SUB), slice(SUB, CHUNK)
    zeros = jnp.zeros((SUB, gw), F32)
    st = (functools.partial(_stack_heads, heads=group_heads, width=dk) if group_heads > 1 else (lambda a_: a_))
    n_st = group_heads * CHUNK
    causal = (lax.broadcasted_iota(jnp.int32, (n_st, n_st), 1)
              <= lax.broadcasted_iota(jnp.int32, (n_st, n_st), 0))
    chunks = []
    for bi, j, gi in [(bi, j, gi) for j in range(tt // CHUNK) for bi in range(nb) for gi in range(n_grp)]:
        rows = slice(j * CHUNK, (j + 1) * CHUNK)
        lanes = slice(gi * gw, (gi + 1) * gw)
        q, k, v, _ = qkvg[bi]
        c, qj, kj = c_alls[bi][rows, lanes], q[rows, lanes], k[rows, lanes]
        vj = v[rows, gi * group_heads * dv:(gi + 1) * group_heads * dv]
        r0 = c[SUB // 2 - 1:SUB // 2, :]
        r1 = c[SUB + SUB // 2 - 1:SUB + SUB // 2, :]
        c_last = c[CHUNK - 1:CHUNK, :]
        kt0 = jnp.concatenate([kj[top] * jnp.exp(r0 - c[top]), zeros], axis=0)
        kt1 = jnp.concatenate([zeros, kj[bot] * jnp.exp(r1 - c[bot])], axis=0)
        qa = qj * jnp.exp(c - r0)
        qb = jnp.concatenate([zeros, qj[bot] * jnp.exp(c[bot] - r1)], axis=0)
        qs = jnp.concatenate([st(qa), st(qb)], axis=1).astype(BF16)
        ks = jnp.concatenate([st(kt0), st(kt1)], axis=1).astype(BF16)
        vs = jnp.concatenate([vj[:, h * dv:(h + 1) * dv] for h in range(group_heads)], axis=0).astype(BF16)
        chunks.append(dict(bi=bi, gi=gi, qs=qs, ks=ks, vs=vs, qe=st(qj * jnp.exp(c)).astype(BF16),
                           kd=st(kj * jnp.exp(c_last - c)).astype(BF16), dec=jnp.exp(c_last)))
    for ch in chunks:
        ch["a"] = jnp.where(causal, _mm_nt(ch["qs"], ch["ks"]), 0.0)
    for ch in chunks:
        ch["oa"] = _mm(ch["a"], ch["vs"])
        ch["kv"] = _mm_tn(ch["vs"], ch["kd"])
    s_t = {(bi, gi): s_ref[bi, :, gi * gw:(gi + 1) * gw] for bi in range(nb) for gi in range(n_grp)}
    outs = [[[] for _ in range(n_grp)] for _ in range(nb)]
    for ch in chunks:
        key = (ch["bi"], ch["gi"])
        o = ch["oa"] + _mm_nt(ch["qe"], s_t[key])
        s_t[key] = s_t[key] * ch["dec"] + ch["kv"]
        o = o * lax.rsqrt(jnp.mean(o * o, axis=-1, keepdims=True) + NORM_EPS) * norm_w
        outs[ch["bi"]][ch["gi"]].append(
            jnp.concatenate([o[h * CHUNK:(h + 1) * CHUNK] for h in range(group_heads)], axis=1))
    for (bi, gi), s in s_t.items():
        s_ref[bi, :, gi * gw:(gi + 1) * gw] = s
    return [jnp.concatenate([jnp.concatenate(grp, axis=0) for grp in row_outs], axis=1) for row_outs in outs]


def _silu(x):
    return x * _sigmoid(x)


def _gla_kernel(qk_ref, v_ref, small_ref, og_ref, gkup_ref, gkb_ref, nw_ref, tril_ref, y_ref, s_ref):
    qkvg = []
    for bi in range(qk_ref.shape[0]):
        qk = qk_ref[bi].astype(F32)
        x = _mm(small_ref[bi, :, SMALL_MISC:SMALL_MISC + LANE], gkup_ref[...]) + gkb_ref[...]
        g = -_softplus(-x) * (1.0 / GLA_GATE_NORM)
        qkvg.append((qk[:, :GLA_K] * (GLA_DK ** -0.5), qk[:, GLA_K:], v_ref[bi].astype(F32), g))
    outs = _gla_tiles(qkvg, tril_ref[...], s_ref, nw_ref[...], GLA_HEADS, GLA_DK, LANE // GLA_DK,
                      pl.program_id(1) == 0)
    for bi, o in enumerate(outs):
        y_ref[bi] = (o * _silu(og_ref[bi].astype(F32))).astype(y_ref.dtype)


def _hgrn_kernel(q_ref, f_ref, i_ref, og_ref, lb_ref, nw_ref, tril_ref, y_ref, s_ref):
    lb = lb_ref[...]
    qkvg = []
    for bi in range(q_ref.shape[0]):
        z = f_ref[bi].astype(F32)
        forget = lb + (1.0 - lb) * _sigmoid(z)
        g = jnp.log(jnp.maximum(forget, F_TINY))
        qkvg.append((q_ref[bi].astype(F32), 1.0 - forget, i_ref[bi].astype(F32), g))
    outs = _gla_tiles(qkvg, tril_ref[...], s_ref, nw_ref[...], HG_HEADS, HG_DK, 1, pl.program_id(1) == 0)
    for bi, o in enumerate(outs):
        y_ref[bi] = (o * _silu(og_ref[bi].astype(F32))).astype(y_ref.dtype)


def _pcol(name, nb, tt):
    ct = (name if isinstance(name, int) else _K_OFF[name][0] // IN_TN)
    return pl.BlockSpec((None, nb, tt, IN_TN), lambda b, t: (ct, b, t, 0))


def _const_spec(shape):
    nd = len(shape)
    return pl.BlockSpec(shape, lambda b, t: (0,) * nd)


def _seq_params():
    return pltpu.CompilerParams(dimension_semantics=("arbitrary", "arbitrary"), vmem_limit_bytes=VMEM_LIMIT)


def _gla_mixer(p, gk_up_pad, gk_bias, norm_w, tril, tt, nb):
    _, bsz, tp, _ = p.shape
    col = functools.partial(_pcol, nb=nb, tt=tt)
    y = pl.pallas_call(
        _gla_kernel,
        grid=(bsz // nb, tp // tt),
        in_specs=[col(CT_QK), col("gla_v"), col(CT_SMALL), col("gla_g"),
                  _const_spec(gk_up_pad.shape), _const_spec(gk_bias.shape),
                  _const_spec(norm_w.shape), _const_spec(tril.shape)],
        out_specs=pl.BlockSpec((nb, tt, GLA_V), lambda b, t: (b, t, 0)),
        out_shape=jax.ShapeDtypeStruct((bsz, tp, GLA_V), BF16),
        scratch_shapes=[pltpu.VMEM((nb, GLA_DV, GLA_K), F32)],
        compiler_params=_seq_params(),
        name="gla_mixer",
    )(p, p, p, p, gk_up_pad, gk_bias, norm_w, tril)
    return y.reshape(bsz * tp, GLA_V)


def _hgrn_mixer(p, lb, norm_w, tril, tt, nb):
    _, bsz, tp, _ = p.shape
    col = functools.partial(_pcol, nb=nb, tt=tt)
    y = pl.pallas_call(
        _hgrn_kernel,
        grid=(bsz // nb, tp // tt),
        in_specs=[col("hg_q"), col("hg_f"), col("hg_i"), col("hg_g"),
                  _const_spec(lb.shape), _const_spec(norm_w.shape), _const_spec(tril.shape)],
        out_specs=pl.BlockSpec((nb, tt, HG_V), lambda b, t: (b, t, 0)),
        out_shape=jax.ShapeDtypeStruct((bsz, tp, HG_V), BF16),
        scratch_shapes=[pltpu.VMEM((nb, HG_DV, HG_K), F32)],
        compiler_params=_seq_params(),
        name="hgrn_mixer",
    )(p, p, p, p, lb, norm_w, tril)
    return y.reshape(bsz * tp, HG_V)


def _unit_lower_inverses(n_list, blk16, blk32):
    size = n_list[0].shape[0]
    ri = lax.broadcasted_iota(jnp.int32, (size, size), 0)
    ci = lax.broadcasted_iota(jnp.int32, (size, size), 1)
    eye = jnp.where(ri == ci, 1.0, 0.0)
    n16 = [jnp.where(blk16, n, 0.0) for n in n_list]
    n32 = [jnp.where(blk32 & jnp.logical_not(blk16), n, 0.0).astype(BF16) for n in n_list]
    n64 = [jnp.where(blk32, 0.0, n).astype(BF16) for n in n_list]
    t = [eye + n for n in n16]
    m = [n.astype(BF16) for n in n16]
    for it in range(3):
        m = [_mm(x, x).astype(BF16) for x in m]
        t = [ti + _mm(mi, ti) for mi, ti in zip(m, t)]
    for off_diag in (n32, n64):
        tb = [ti.astype(BF16) for ti in t]
        w = [_mm(ti, ni).astype(BF16) for ti, ni in zip(tb, off_diag)]
        t = [ti + _mm(wi, tbi) for ti, wi, tbi in zip(t, w, tb)]
    return t


def _rwkv_kernel(has_vres, *refs):
    if has_vres:
        (r_ref, k_ref, v_ref, small_ref, vf_ref,
         mu_r_ref, mu_k_ref, mu_v_ref, mu_small_ref,
         w0_ref, w2_ref, a0_ref, a2_ref, g2_ref, v0_ref, v2_ref,
         kk_ref, ka_ref, rk_ref, lnw_ref, lnb_ref, seg_ref, tril_ref,
         y_ref, s_ref, carry_ref) = refs
    else:
        (r_ref, k_ref, v_ref, small_ref,
         mu_r_ref, mu_k_ref, mu_v_ref, mu_small_ref,
         w0_ref, w2_ref, a0_ref, a2_ref, g2_ref,
         kk_ref, ka_ref, rk_ref, lnw_ref, lnb_ref, seg_ref, tril_ref,
         y_ref, vf_out_ref, s_ref, carry_ref) = refs

    first = pl.program_id(1) == 0
    nb, tt = r_ref.shape[0], r_ref.shape[1]

    @pl.when(first)
    def _():
        s_ref[...] = jnp.zeros_like(s_ref)
        carry_ref[...] = jnp.zeros_like(carry_ref)

    seg = seg_ref[...]
    tiles = {}

    def prologue(bi):
        srcs = (r_ref, k_ref, v_ref, small_ref)
        mus = (mu_r_ref, mu_k_ref, mu_v_ref, mu_small_ref)
        shifted = []
        off = 0
        crow = bi * SUBLANE
        for src, mu in zip(srcs, mus):
            p = src[bi].astype(F32)
            width = p.shape[1]
            prev_row = carry_ref[crow:crow + 1, off:off + width]
            rolled = pltpu.roll(p, 1, 0)
            row = lax.broadcasted_iota(jnp.int32, p.shape, 0)
            prev = jnp.where(row == 0, prev_row, rolled)
            shifted.append(p + (prev - p) * mu[...])
            carry_ref[crow:crow + 1, off:off + width] = p[tt - 1:tt, :]
            off += width
        r, k, v, small = shifted
        s_g = small[:, SMALL_G:SMALL_G + LANE]
        s_wa = small[:, SMALL_WA:SMALL_WA + LANE]
        s_misc = small[:, SMALL_MISC:SMALL_MISC + LANE]

        w_log = -_softplus(-(w0_ref[...] + _mm(jnp.tanh(s_wa), w2_ref[...]))) - 0.5
        lw = -jnp.exp(w_log)
        a = _sigmoid(a0_ref[...] + _mm(s_wa, a2_ref[...]))
        g = _mm(_sigmoid(s_g), g2_ref[...])
        if has_vres:
            v = v + (vf_ref[bi] - v) * _sigmoid(v0_ref[...] + _mm(s_misc, v2_ref[...]))
        else:
            vf_out_ref[bi] = v
        kk = k * kk_ref[...]
        k = k * (1.0 + (a - 1.0) * ka_ref[...])
        sums = _head_sums(jnp.concatenate([kk * kk, r * k * rk_ref[...]], axis=0), seg)
        kk = kk / jnp.maximum(jnp.sqrt(sums[:tt]), 1e-12)
        c_all = _cumsum_rows(tril_ref[...], lw)
        tiles[bi] = dict(r=r, k=k, v=v, g=g, bonus=sums[tt:] * v, alpha=-kk, beta=kk * a,
                         c=c_all, ce=c_all - lw)

    gs = RW_GROUP
    n_groups = RW_HEADS // RW_GROUP_HEADS
    n_st = RW_GROUP_HEADS * CHUNK
    ri = lax.broadcasted_iota(jnp.int32, (n_st, n_st), 0)
    ci = lax.broadcasted_iota(jnp.int32, (n_st, n_st), 1)
    strict = ci < ri
    sh = INV_BLOCK.bit_length() - 1
    blk16 = (ri >> sh) == (ci >> sh)
    blk32 = (ri >> (sh + 1)) == (ci >> (sh + 1))
    tcol = lax.broadcasted_iota(jnp.int32, (CHUNK, n_st), 1) & (CHUNK - 1)
    trow = lax.broadcasted_iota(jnp.int32, (CHUNK, n_st), 0)
    incl = tcol <= trow
    st = functools.partial(_stack_heads, heads=RW_GROUP_HEADS, width=RW_HD)

    for bi in range(nb):
        prologue(bi)
    blocks = _rwkv_blocks(tiles, range(nb), tt, n_groups, gs, st, strict, incl, blk16, blk32)

    n_state = nb * n_groups
    state = [s_ref[si] for si in range(n_state)]
    y_rows = [[] for _ in range(nb)]
    for j in range(tt // CHUNK):
        cur = sorted([b for b in blocks if b["j"] == j], key=lambda b: b["si"])
        sb = [state[b["si"]].astype(BF16) for b in cur]
        n_st = cur[0]["aem"].shape[0]
        ss = [_mm_nt(b["aem_rs"], s) for b, s in zip(cur, sb)]
        us = [_mm(b["t_inv"], s_[:n_st] + b["w1"]).astype(BF16) for b, s_ in zip(cur, ss)]
        for b, u in zip(cur, us):
            state[b["si"]] = state[b["si"]] * b["dec"] + _mm_tn(u, b["bem"]) + b["vk"]
        ys = [s_[n_st:] + _mm(b["a_rb"], u) + b["y0"] for b, s_, u in zip(cur, ss, us)]
        for bi in range(nb):
            y_rows[bi].append(jnp.concatenate(ys[bi * n_groups:(bi + 1) * n_groups], axis=1))
    for si in range(n_state):
        s_ref[si] = state[si]

    inv_hd = 1.0 / RW_HD
    for bi in range(nb):
        y = jnp.concatenate(y_rows[bi], axis=0)
        mean = _head_sums(y, seg) * inv_hd
        yc = y - mean
        var = _head_sums(yc * yc, seg) * inv_hd
        y = yc * lax.rsqrt(var + RW_GN_EPS) * lnw_ref[...] + lnb_ref[...]
        y_ref[bi] = ((y + tiles[bi]["bonus"]) * tiles[bi]["g"]).astype(y_ref.dtype)


def _rwkv_blocks(tiles, batch_rows, tt, n_groups, gs, st, strict, incl, blk16, blk32):
    blocks = []
    for bi, j in [(bi, j) for j in range(tt // CHUNK) for bi in batch_rows]:
        rows = slice(j * CHUNK, (j + 1) * CHUNK)
        tl = tiles[bi]
        c, ce = tl["c"][rows], tl["ce"][rows]
        rj, kj, vj, al, be_ = tl["r"][rows], tl["k"][rows], tl["v"][rows], tl["alpha"][rows], tl["beta"][rows]
        mid = c[CHUNK // 2 - 1:CHUNK // 2, :]
        c_last = c[CHUNK - 1:CHUNK, :]
        e_out = jnp.exp(mid - c)
        e_end = jnp.exp(c_last - c)
        at = al * jnp.exp(ce - mid)
        rt = rj * jnp.exp(c - mid)
        bh = be_ * e_out
        kh = kj * e_out
        ae = al * jnp.exp(ce)
        rs = rj * jnp.exp(c)
        be = be_ * e_end
        ke = kj * e_end
        dec = jnp.exp(c_last)
        for gi in range(n_groups):
            sl = slice(gi * gs, (gi + 1) * gs)
            stb = lambda a_: st(a_[:, sl].astype(BF16))
            blocks.append(dict(j=j, si=bi * n_groups + gi, bm=stb(bh), km=stb(kh), am=stb(at), v_bd=stb(vj),
                               aem=stb(ae),
                               bem=stb(be), kem=stb(ke), rt=rt[:, sl].astype(BF16),
                               rs=rs[:, sl].astype(BF16), dec=dec[:, sl]))
    n_st = strict.shape[0]
    for b in blocks:
        b["lhs"] = jnp.concatenate([b["am"], b["rt"]], axis=0)
    for b in blocks:
        prod = _mm_nt(b["lhs"], b["bm"])
        b["n_ab"] = jnp.where(strict, prod[:n_st], 0.0)
        b["a_rb"] = jnp.where(incl, prod[n_st:], 0.0).astype(BF16)
    for b in blocks:
        prod = _mm_nt(b["lhs"], b["km"])
        b["n_ak"] = jnp.where(strict, prod[:n_st], 0.0).astype(BF16)
        b["a_rk"] = jnp.where(incl, prod[n_st:], 0.0).astype(BF16)
    for b, t_inv in zip(blocks, _unit_lower_inverses([b["n_ab"] for b in blocks], blk16, blk32)):
        b["t_inv"] = t_inv.astype(BF16)
    for b in blocks:
        prod = _mm(jnp.concatenate([b["n_ak"], b["a_rk"]], axis=0), b["v_bd"])
        b["w1"] = prod[:n_st]
        b["y0"] = prod[n_st:]
        b["vk"] = _mm_tn(b["v_bd"], b["kem"])
        b["aem_rs"] = jnp.concatenate([b["aem"], b["rs"]], axis=0)
    return blocks


def _rwkv_mixer(p, v_first, mus, prm, seg, tril, tt, nb):
    _, bsz, tp, _ = p.shape
    nt = tp // tt
    has_vres = v_first is not None
    row_spec = pl.BlockSpec((nb, tt, RW_DIM), lambda b, t: (b, t, 0))
    col = functools.partial(_pcol, nb=nb, tt=tt)
    in_specs = [col("rw_r"), col("rw_k"), col("rw_v"), col(CT_SMALL)]
    args = [p, p, p, p]
    if has_vres:
        in_specs.append(row_spec)
        args.append(v_first)
    consts = list(mus) + [prm["w0"], prm["w2"], prm["a0"], prm["a2"], prm["g2"]]
    if has_vres:
        consts += [prm["v0"], prm["v2"]]
    consts += [prm["kk"], prm["ka"], prm["rk"], prm["lnw"], prm["lnb"], seg, tril]
    in_specs += [_const_spec(c.shape) for c in consts]
    args += consts
    out_shape = [jax.ShapeDtypeStruct((bsz, tp, RW_DIM), BF16)]
    out_specs = [row_spec]
    if not has_vres:
        out_shape.append(jax.ShapeDtypeStruct((bsz, tp, RW_DIM), F32))
        out_specs.append(row_spec)
    carry_w = 3 * RW_DIM + IN_TN
    res = pl.pallas_call(
        functools.partial(_rwkv_kernel, has_vres),
        grid=(bsz // nb, nt),
        in_specs=in_specs,
        out_specs=out_specs,
        out_shape=out_shape,
        scratch_shapes=[pltpu.VMEM((nb * (RW_HEADS // RW_GROUP_HEADS), RW_GROUP, RW_GROUP), F32),
                        pltpu.VMEM((nb * SUBLANE, carry_w), F32)],
        compiler_params=_seq_params(),
        name="rwkv_mixer",
    )(*args)
    y = res[0].reshape(bsz * tp, RW_DIM)
    return (y, v_first) if has_vres else (y, res[1])


def _merge_kernel(z_ref, gg_ref, gr_ref, gh_ref, yg_ref, yr_ref, yh_ref, wa_ref, wb_ref, wc_ref, wo_ref, o_ref):
    d = lambda y, w: jnp.dot(y[...], w[...], preferred_element_type=F32)
    sg = lambda g: _sigmoid(jnp.concatenate([g[t] for t in range(g.shape[0])], axis=1).astype(F32))
    m = sg(gg_ref) * d(yg_ref, wa_ref) + sg(gr_ref) * d(yr_ref, wb_ref) + sg(gh_ref) * d(yh_ref, wc_ref)
    o_ref[...] = z_ref[...] + _mm(m, wo_ref[...])


def _merge(z2d, p, y_gla, y_rw, y_hg, wa, wb, wc, wo, tm):
    n = z2d.shape[0]
    row = lambda w: pl.BlockSpec((tm, w), lambda i: (i, 0))
    gate = lambda name: pl.BlockSpec((D_MODEL // IN_TN, tm, IN_TN),
                                     lambda i, cb=_K_OFF[name][0] // D_MODEL: (cb, i, 0))
    full = lambda a: pl.BlockSpec(a.shape, lambda i: (0, 0))
    return pl.pallas_call(
        _merge_kernel,
        grid=(n // tm,),
        in_specs=[row(D_MODEL), gate("gate_gla"), gate("gate_rw"), gate("gate_hg"),
                  row(GLA_V), row(RW_DIM), row(HG_V), full(wa), full(wb), full(wc), full(wo)],
        out_specs=row(D_MODEL),
        out_shape=jax.ShapeDtypeStruct((n, D_MODEL), F32),
        compiler_params=pltpu.CompilerParams(dimension_semantics=("arbitrary",), vmem_limit_bytes=VMEM_LIMIT),
        name="merge",
    )(z2d, p, p, p, y_gla, y_rw, y_hg, wa, wb, wc, wo)


FFN_FC = 256
FFN_HALO = SUBLANE


def _ffn_kernel(final, z_ref, halo_ref, nw_ref, wu_ref, cw_ref, cb_ref, wd_ref, pnw_ref, o_ref, *rest):
    h_ref, act_ref = (None, rest[0]) if final else rest
    tt = z_ref.shape[0]
    zt = z_ref[...]
    nw = nw_ref[...]
    h = jnp.concatenate([_rms(halo_ref[...], nw), _rms(zt, nw)], axis=0).astype(BF16)
    has_prev = pl.program_id(1) > 0
    fc = FFN_FC
    n_chunks = D_FF // fc
    cols = lambda j, half: slice(half * D_FF + j * fc, half * D_FF + (j + 1) * fc)

    def up(j):
        return (jnp.dot(h, wu_ref[:, cols(j, 0)], preferred_element_type=F32),
                jnp.dot(h, wu_ref[:, cols(j, 1)], preferred_element_type=F32))

    def conv(u, cw, cb):
        u = jnp.concatenate([jnp.where(has_prev, u[:FFN_HALO], 0.0), u[FFN_HALO:]], axis=0)
        inner = u * cw[1:2, :] + pltpu.roll(u * cw[0:1, :], 1, 0)
        c = cb + u * cw[2:3, :] + pltpu.roll(inner, 1, 0)
        return c[FFN_HALO:, :]

    split = (n_chunks + 1) // 2
    out = zt
    u_next = up(0)
    for j in range(n_chunks):
        ug, uv = u_next
        if j + 1 < n_chunks:
            u_next = up(j + 1)
        cg = conv(ug, cw_ref[:, cols(j, 0)], cb_ref[:, cols(j, 0)])
        cv = conv(uv, cw_ref[:, cols(j, 1)], cb_ref[:, cols(j, 1)])
        act_ref[:, j * fc:(j + 1) * fc] = (cg * cv / (1.0 + jnp.exp(-cg))).astype(BF16)
        if j + 1 in (split, n_chunks):
            lo = 0 if j + 1 == split else split
            out = out + jnp.dot(act_ref[:, lo * fc:(j + 1) * fc], wd_ref[lo * fc:(j + 1) * fc, :],
                                preferred_element_type=F32)
    if final:
        o_ref[...] = _rms(out, pnw_ref[...])
    else:
        o_ref[...] = out
        h_ref[...] = _rms(out, pnw_ref[...]).astype(h_ref.dtype)


def _ffn(z3d, norm_w, wu, cw, cb, wd, post_norm_w, tt, final):
    bsz, tp, _ = z3d.shape
    hb = tt // FFN_HALO
    once = lambda a: pl.BlockSpec(a.shape, lambda b, t: (0,) * a.ndim, pipeline_mode=pl.Buffered(1))
    tile = pl.BlockSpec((None, tt, D_MODEL), lambda b, t: (b, t, 0))
    out_shape = [jax.ShapeDtypeStruct((bsz, tp, D_MODEL), F32)]
    if not final:
        out_shape.append(jax.ShapeDtypeStruct((bsz, tp, D_MODEL), BF16))
    res = pl.pallas_call(
        functools.partial(_ffn_kernel, final),
        grid=(bsz, tp // tt),
        in_specs=[tile,
                  pl.BlockSpec((None, FFN_HALO, D_MODEL), lambda b, t: (b, jnp.maximum(t * hb - 1, 0), 0)),
                  once(norm_w), once(wu), once(cw), once(cb), once(wd), once(post_norm_w)],
        out_specs=[tile] * len(out_shape),
        out_shape=out_shape,
        scratch_shapes=[pltpu.VMEM((tt, D_FF), BF16)],
        compiler_params=_seq_params(),
        name="conv_ffn",
    )(z3d, z3d, norm_w, wu, cw, cb, wd, post_norm_w)
    return res[0] if final else (res[0], res[1])


def _largest_tile(n, cands):
    for c in cands:
        if n % c == 0:
            return c
    raise ValueError(f"no tile for {n}")


def _ref_cols(w, name):
    off, width = _REF_OFF[name]
    return w[..., off:off + width]


def _pack_columns(w, vres):
    lead = w.shape[:-1]
    zeros = lambda n: jnp.zeros(lead + (n,), w.dtype)
    vr = vres if vres is not None else zeros(RW_V_RANK)
    parts = []
    for name, width in _K_LAYOUT:
        if name == "rw_wa":
            parts.append(jnp.concatenate([_ref_cols(w, "rw_w"), _ref_cols(w, "rw_a")], axis=-1))
        elif name == "misc":
            parts.append(jnp.concatenate(
                [_ref_cols(w, "gla_gk"), vr, zeros(width - GLA_GATE_RANK - RW_V_RANK)], axis=-1))
        else:
            parts.append(_ref_cols(w, name))
    used = sum(wd for _, wd in _K_LAYOUT)
    parts.append(zeros(WP - used))
    return jnp.concatenate(parts, axis=-1)


def _pad_rows(w, lo, total):
    return jnp.pad(w, ((lo, total - lo - w.shape[0]), (0, 0)))


def kernel(x, meta, mix_norm, w_in, w_in_vres, rw_mu, rw_mu_vres, gla_gk_up, gla_gk_bias, gla_norm, rw_w0,
           rw_w2, rw_a0, rw_a2, rw_v0, rw_v2, rw_g2, rw_kk, rw_ka, rw_rk, rw_ln_w, rw_ln_b, hg_lb_logits,
           hg_norm, w_out_gla, w_out_rw, w_out_hg, w_out, ffn_norm, w_up, conv_w, conv_b, w_down, final_norm):
    bsz, seq, _ = x.shape
    depth = w_in.shape[0]
    t_real = seq + N_META
    tp = -(-t_real // CHUNK) * CHUNK
    n = bsz * tp
    tm = _largest_tile(n, (3072, 2048, 1024, 512, 256, 128, 64))
    tm_merge = _largest_tile(n, (512, 256, 128, 64))
    tt = _largest_tile(tp, (704, 512, 384, 256, 192, 128, 64))
    tt_mix = _largest_tile(tp, (3 * CHUNK, 2 * CHUNK, CHUNK))
    nb_mix = _largest_tile(bsz, (4, 2, 1))

    z = jnp.concatenate([jnp.broadcast_to(meta.astype(x.dtype)[None], (bsz, N_META, D_MODEL)), x,
                         jnp.zeros((bsz, tp - t_real, D_MODEL), x.dtype)], axis=1)

    row = lambda a: a.reshape(1, -1).astype(F32)
    t_idx = jnp.arange(tt_mix)
    tril = ((t_idx[:, None] // CHUNK == t_idx[None, :] // CHUNK)
            & (t_idx[None, :] <= t_idx[:, None])).astype(BF16)
    head_of = jnp.arange(RW_GROUP) // RW_HD
    seg = (head_of[:, None] == head_of[None, :]).astype(BF16)
    lb_p = jax.nn.softmax(hg_lb_logits.astype(F32), axis=0)
    lb_all = jnp.cumsum(lb_p, axis=0) - lb_p[0:1]

    nj = D_FF // FFN_FC
    v_first = None
    h_next = None
    for i in range(depth):
        vres_w = w_in_vres[i - 1] if i > 0 else None
        vres_mu = rw_mu_vres[i - 1] if i > 0 else None
        w_cat = _pack_columns(w_in[i], vres_w).astype(BF16)
        mu_full = jnp.concatenate([jnp.zeros((W_IN - rw_mu.shape[1],), F32), rw_mu[i].astype(F32)])
        mu_cat = _pack_columns(mu_full, vres_mu).reshape(N_CT, 1, IN_TN)
        mus = [mu_cat[_K_OFF[nm][0] // IN_TN] for nm in ("rw_r", "rw_k", "rw_v", "rw_g")]

        z2d = z.reshape(n, D_MODEL)
        p = _inproj(z2d if h_next is None else h_next.reshape(n, D_MODEL), row(mix_norm[i]), w_cat, tm)

        gk_up_pad = _pad_rows(gla_gk_up[i], MISC_GK, LANE).astype(BF16)
        p4 = p.reshape(N_CT, bsz, tp, IN_TN)
        y_gla = _gla_mixer(p4, gk_up_pad, row(gla_gk_bias[i]), row(gla_norm[i]), tril, tt_mix, nb_mix)
        y_hg = _hgrn_mixer(p4, row(lb_all[i]), row(hg_norm[i]), tril, tt_mix, nb_mix)

        prm = {
            "w0": row(rw_w0[i]), "w2": _pad_rows(rw_w2[i], 0, LANE).astype(BF16),
            "a0": row(rw_a0[i]), "a2": _pad_rows(rw_a2[i], RW_W_RANK, LANE).astype(BF16),
            "g2": rw_g2[i].astype(BF16),
            "kk": row(rw_kk[i]), "ka": row(rw_ka[i]), "rk": row(rw_rk[i]),
            "lnw": row(rw_ln_w[i]), "lnb": row(rw_ln_b[i]),
        }
        if i > 0:
            prm["v0"] = row(rw_v0[i - 1])
            prm["v2"] = _pad_rows(rw_v2[i - 1], MISC_VR, LANE).astype(BF16)
        y_rw, v_first = _rwkv_mixer(p4, v_first, mus, prm, seg, tril, tt_mix, nb_mix)

        z2d = _merge(z2d, p, y_gla, y_rw, y_hg, w_out_gla[i].astype(BF16), w_out_rw[i].astype(BF16),
                     w_out_hg[i].astype(BF16), w_out[i].astype(BF16), tm_merge)

        cw = jnp.pad(conv_w[i].astype(F32), ((0, SUBLANE - CONV_W), (0, 0)))
        last = i == depth - 1
        res = _ffn(z2d.reshape(bsz, tp, D_MODEL), row(ffn_norm[i]), w_up[i].astype(BF16), cw, row(conv_b[i]),
                   w_down[i].astype(BF16), row(final_norm if last else mix_norm[i + 1]), tt, final=last)
        z, h_next = (res, None) if last else res
    return z[:, N_META:t_real]
```

```python
import functools

import jax
import jax.numpy as jnp
from jax import lax
from jax.experimental import pallas as pl
from jax.experimental.pallas import tpu as pltpu

F32 = jnp.float32
BF16 = jnp.bfloat16

D_MODEL = 1024
N_META = 16
F_TINY = 1e-30
NORM_EPS = 1e-6

GLA_HEADS, GLA_DK, GLA_DV = 4, 64, 128
GLA_K, GLA_V = GLA_HEADS * GLA_DK, GLA_HEADS * GLA_DV
GLA_GATE_RANK = 16
GLA_GATE_NORM = 16.0

RW_HEADS, RW_HD = 8, 64
RW_DIM = RW_HEADS * RW_HD
RW_W_RANK, RW_A_RANK, RW_V_RANK, RW_G_RANK = 64, 64, 32, 128
RW_GN_EPS = 64e-5
RW_GROUP_HEADS = 4
RW_GROUP = RW_GROUP_HEADS * RW_HD

HG_HEADS, HG_DK, HG_DV = 4, 128, 128
HG_K, HG_V = HG_HEADS * HG_DK, HG_HEADS * HG_DV

D_FF = 2816
CONV_W = 3

_REF_LAYOUT = (
    ("gla_q", GLA_K), ("gla_k", GLA_K), ("gla_v", GLA_V), ("gla_gk", GLA_GATE_RANK), ("gla_g", GLA_V),
    ("hg_q", HG_K), ("hg_f", HG_K), ("hg_i", HG_V), ("hg_g", HG_V),
    ("gate_gla", D_MODEL), ("gate_rw", D_MODEL), ("gate_hg", D_MODEL),
    ("rw_r", RW_DIM), ("rw_w", RW_W_RANK), ("rw_k", RW_DIM), ("rw_v", RW_DIM), ("rw_a", RW_A_RANK),
    ("rw_g", RW_G_RANK),
)
_REF_OFF = {}
_o = 0
for _n, _w in _REF_LAYOUT:
    _REF_OFF[_n] = (_o, _w)
    _o += _w
W_IN = _o

LANE = 128
SUBLANE = 8
_K_LAYOUT = (
    ("gate_gla", 1024), ("gate_rw", 1024), ("gate_hg", 1024),
    ("gla_v", 512), ("gla_g", 512), ("hg_q", 512), ("hg_f", 512), ("hg_i", 512), ("hg_g", 512),
    ("rw_r", 512), ("rw_k", 512), ("rw_v", 512),
    ("gla_q", 256), ("gla_k", 256),
    ("rw_g", 128), ("rw_wa", 128), ("misc", 128),
)
_K_OFF = {}
_o = 0
for _n, _w in _K_LAYOUT:
    assert _o % _w == 0
    _K_OFF[_n] = (_o, _w)
    _o += _w
IN_TN = 512
WP = -(-_o // IN_TN) * IN_TN
N_CT = WP // IN_TN
CT_QK = _K_OFF["gla_q"][0] // IN_TN
CT_SMALL = _K_OFF["rw_g"][0] // IN_TN
assert _K_OFF["gla_k"][0] // IN_TN == CT_QK and _K_OFF["misc"][0] // IN_TN == CT_SMALL
SMALL_G, SMALL_WA, SMALL_MISC = (_K_OFF[_n][0] - CT_SMALL * IN_TN for _n in ("rw_g", "rw_wa", "misc"))
MISC_GK = 0
MISC_VR = GLA_GATE_RANK

CHUNK = 64
SUB = 32
INV_BLOCK = 16
assert CHUNK == 4 * INV_BLOCK
VMEM_LIMIT = 56 * 1024 * 1024


def _mm(a, b):
    return jnp.dot(a.astype(BF16), b.astype(BF16), preferred_element_type=F32)


def _mm_nt(a, b):
    return lax.dot_general(a.astype(BF16), b.astype(BF16), (((1,), (1,)), ((), ())),
                           preferred_element_type=F32)


def _mm_tn(a, b):
    return lax.dot_general(a.astype(BF16), b.astype(BF16), (((0,), (0,)), ((), ())),
                           preferred_element_type=F32)


def _split2(x):
    hi = x.astype(BF16)
    lo = (x - hi.astype(F32)).astype(BF16)
    return hi, lo


def _head_sums(x, seg_group):
    gw = seg_group.shape[0]
    xb = x.astype(BF16)
    return jnp.concatenate([jnp.dot(xb[:, i:i + gw], seg_group, preferred_element_type=F32)
                            for i in range(0, x.shape[1], gw)], axis=1)


def _cumsum_rows(tril_bf16, g):
    w = g.shape[1]
    out = jnp.dot(tril_bf16, jnp.concatenate(_split2(g), axis=1), preferred_element_type=F32)
    return out[:, :w] + out[:, w:]


def _sigmoid(x):
    return 1.0 / (1.0 + jnp.exp(-x))


def _softplus(x):
    return jnp.maximum(x, 0.0) + jnp.log(1.0 + jnp.exp(-jnp.abs(x)))


def _rms(x, w):
    return x * lax.rsqrt(jnp.mean(x * x, axis=-1, keepdims=True) + NORM_EPS) * w


def _stack_heads(x, heads, width):
    lane = lax.broadcasted_iota(jnp.int32, x.shape, 1)
    return jnp.concatenate(
        [jnp.where((lane >= h * width) & (lane < (h + 1) * width), x, 0.0) for h in range(heads)], axis=0)


def _inproj_kernel(z_ref, nw_ref, w_ref, o_ref, *scratch):
    if not scratch:
        h = z_ref[...]
    else:
        h_ref, = scratch

        @pl.when(pl.program_id(1) == 0)
        def _():
            h_ref[...] = _rms(z_ref[...], nw_ref[...]).astype(BF16)

        h = h_ref[...]
    o_ref[...] = jnp.dot(h, w_ref[...], preferred_element_type=F32).astype(o_ref.dtype)


def _inproj(z2d, norm_w, w_bf16, tm):
    n = z2d.shape[0]
    return pl.pallas_call(
        _inproj_kernel,
        grid=(n // tm, N_CT),
        in_specs=[
            pl.BlockSpec((tm, D_MODEL), lambda i, j: (i, 0)),
            pl.BlockSpec((1, D_MODEL), lambda i, j: (0, 0)),
            pl.BlockSpec((D_MODEL, IN_TN), lambda i, j: (0, j)),
        ],
        out_specs=pl.BlockSpec((None, tm, IN_TN), lambda i, j: (j, i, 0)),
        out_shape=jax.ShapeDtypeStruct((N_CT, n, IN_TN), BF16),
        scratch_shapes=[] if z2d.dtype == BF16 else [pltpu.VMEM((tm, D_MODEL), BF16)],
        compiler_params=pltpu.CompilerParams(
            dimension_semantics=("arbitrary", "arbitrary"), vmem_limit_bytes=VMEM_LIMIT),
        name="inproj",
    )(z2d, norm_w, w_bf16)


def _gla_tiles(qkvg, tril_bd, s_ref, norm_w, heads, dk, group_heads, first):
    tt, hk = qkvg[0][0].shape
    assert tt % CHUNK == 0 and CHUNK == 2 * SUB and heads % group_heads == 0
    dv = qkvg[0][2].shape[1] // heads
    nb = len(qkvg)
    gw = group_heads * dk
    n_grp = heads // group_heads

    @pl.when(first)
    def _():
        s_ref[...] = jnp.zeros_like(s_ref)

    c_alls = [_cumsum_rows(tril_bd, g) for _, _, _, g in qkvg]
    top, bot = slice(0, SUB), slice(SUB, CHUNK)
    zeros = jnp.zeros((SUB, gw), F32)
    st = (functools.partial(_stack_heads, heads=group_heads, width=dk) if group_heads > 1 else (lambda a_: a_))
    n_st = group_heads * CHUNK
    causal = (lax.broadcasted_iota(jnp.int32, (n_st, n_st), 1)
              <= lax.broadcasted_iota(jnp.int32, (n_st, n_st), 0))
    chunks = []
    for bi, j, gi in [(bi, j, gi) for j in range(tt // CHUNK) for bi in range(nb) for gi in range(n_grp)]:
        rows = slice(j * CHUNK, (j + 1) * CHUNK)
        lanes = slice(gi * gw, (gi + 1) * gw)
        q, k, v, _ = qkvg[bi]
        c, qj, kj = c_alls[bi][rows, lanes], q[rows, lanes], k[rows, lanes]
        vj = v[rows, gi * group_heads * dv:(gi + 1) * group_heads * dv]
        r0 = c[SUB // 2 - 1:SUB // 2, :]
        r1 = c[SUB + SUB // 2 - 1:SUB + SUB // 2, :]
        c_last = c[CHUNK - 1:CHUNK, :]
        kt0 = jnp.concatenate([kj[top] * jnp.exp(r0 - c[top]), zeros], axis=0)
        kt1 = jnp.concatenate([zeros, kj[bot] * jnp.exp(r1 - c[bot])], axis=0)
        qa = qj * jnp.exp(c - r0)
        qb = jnp.concatenate([zeros, qj[bot] * jnp.exp(c[bot] - r1)], axis=0)
        qs = jnp.concatenate([st(qa), st(qb)], axis=1).astype(BF16)
        ks = jnp.concatenate([st(kt0), st(kt1)], axis=1).astype(BF16)
        vs = jnp.concatenate([vj[:, h * dv:(h + 1) * dv] for h in range(group_heads)], axis=0).astype(BF16)
        chunks.append(dict(bi=bi, gi=gi, qs=qs, ks=ks, vs=vs, qe=st(qj * jnp.exp(c)).astype(BF16),
                           kd=st(kj * jnp.exp(c_last - c)).astype(BF16), dec=jnp.exp(c_last)))
    for ch in chunks:
        ch["a"] = jnp.where(causal, _mm_nt(ch["qs"], ch["ks"]), 0.0)
    for ch in chunks:
        ch["oa"] = _mm(ch["a"], ch["vs"])
        ch["kv"] = _mm_tn(ch["vs"], ch["kd"])
    s_t = {(bi, gi): s_ref[bi, :, gi * gw:(gi + 1) * gw] for bi in range(nb) for gi in range(n_grp)}
    outs = [[[] for _ in range(n_grp)] for _ in range(nb)]
    for ch in chunks:
        key = (ch["bi"], ch["gi"])
        o = ch["oa"] + _mm_nt(ch["qe"], s_t[key])
        s_t[key] = s_t[key] * ch["dec"] + ch["kv"]
        o = o * lax.rsqrt(jnp.mean(o * o, axis=-1, keepdims=True) + NORM_EPS) * norm_w
        outs[ch["bi"]][ch["gi"]].append(
            jnp.concatenate([o[h * CHUNK:(h + 1) * CHUNK] for h in range(group_heads)], axis=1))
    for (bi, gi), s in s_t.items():
        s_ref[bi, :, gi * gw:(gi + 1) * gw] = s
    return [jnp.concatenate([jnp.concatenate(grp, axis=0) for grp in row_outs], axis=1) for row_outs in outs]


def _silu(x):
    return x * _sigmoid(x)


def _gla_kernel(qk_ref, v_ref, small_ref, og_ref, gkup_ref, gkb_ref, nw_ref, tril_ref, y_ref, s_ref):
    qkvg = []
    for bi in range(qk_ref.shape[0]):
        qk = qk_ref[bi].astype(F32)
        x = _mm(small_ref[bi, :, SMALL_MISC:SMALL_MISC + LANE], gkup_ref[...]) + gkb_ref[...]
        g = -_softplus(-x) * (1.0 / GLA_GATE_NORM)
        qkvg.append((qk[:, :GLA_K] * (GLA_DK ** -0.5), qk[:, GLA_K:], v_ref[bi].astype(F32), g))
    outs = _gla_tiles(qkvg, tril_ref[...], s_ref, nw_ref[...], GLA_HEADS, GLA_DK, LANE // GLA_DK,
                      pl.program_id(1) == 0)
    for bi, o in enumerate(outs):
        y_ref[bi] = (o * _silu(og_ref[bi].astype(F32))).astype(y_ref.dtype)


def _hgrn_kernel(q_ref, f_ref, i_ref, og_ref, lb_ref, nw_ref, tril_ref, y_ref, s_ref):
    lb = lb_ref[...]
    qkvg = []
    for bi in range(q_ref.shape[0]):
        z = f_ref[bi].astype(F32)
        forget = lb + (1.0 - lb) * _sigmoid(z)
        g = jnp.log(jnp.maximum(forget, F_TINY))
        qkvg.append((q_ref[bi].astype(F32), 1.0 - forget, i_ref[bi].astype(F32), g))
    outs = _gla_tiles(qkvg, tril_ref[...], s_ref, nw_ref[...], HG_HEADS, HG_DK, 1, pl.program_id(1) == 0)
    for bi, o in enumerate(outs):
        y_ref[bi] = (o * _silu(og_ref[bi].astype(F32))).astype(y_ref.dtype)


def _pcol(name, nb, tt):
    ct = (name if isinstance(name, int) else _K_OFF[name][0] // IN_TN)
    return pl.BlockSpec((None, nb, tt, IN_TN), lambda b, t: (ct, b, t, 0))


def _const_spec(shape):
    nd = len(shape)
    return pl.BlockSpec(shape, lambda b, t: (0,) * nd)


def _seq_params():
    return pltpu.CompilerParams(dimension_semantics=("arbitrary", "arbitrary"), vmem_limit_bytes=VMEM_LIMIT)


def _gla_mixer(p, gk_up_pad, gk_bias, norm_w, tril, tt, nb):
    _, bsz, tp, _ = p.shape
    col = functools.partial(_pcol, nb=nb, tt=tt)
    y = pl.pallas_call(
        _gla_kernel,
        grid=(bsz // nb, tp // tt),
        in_specs=[col(CT_QK), col("gla_v"), col(CT_SMALL), col("gla_g"),
                  _const_spec(gk_up_pad.shape), _const_spec(gk_bias.shape),
                  _const_spec(norm_w.shape), _const_spec(tril.shape)],
        out_specs=pl.BlockSpec((nb, tt, GLA_V), lambda b, t: (b, t, 0)),
        out_shape=jax.ShapeDtypeStruct((bsz, tp, GLA_V), BF16),
        scratch_shapes=[pltpu.VMEM((nb, GLA_DV, GLA_K), F32)],
        compiler_params=_seq_params(),
        name="gla_mixer",
    )(p, p, p, p, gk_up_pad, gk_bias, norm_w, tril)
    return y.reshape(bsz * tp, GLA_V)


def _hgrn_mixer(p, lb, norm_w, tril, tt, nb):
    _, bsz, tp, _ = p.shape
    col = functools.partial(_pcol, nb=nb, tt=tt)
    y = pl.pallas_call(
        _hgrn_kernel,
        grid=(bsz // nb, tp // tt),
        in_specs=[col("hg_q"), col("hg_f"), col("hg_i"), col("hg_g"),
                  _const_spec(lb.shape), _const_spec(norm_w.shape), _const_spec(tril.shape)],
        out_specs=pl.BlockSpec((nb, tt, HG_V), lambda b, t: (b, t, 0)),
        out_shape=jax.ShapeDtypeStruct((bsz, tp, HG_V), BF16),
        scratch_shapes=[pltpu.VMEM((nb, HG_DV, HG_K), F32)],
        compiler_params=_seq_params(),
        name="hgrn_mixer",
    )(p, p, p, p, lb, norm_w, tril)
    return y.reshape(bsz * tp, HG_V)


def _unit_lower_inverses(n_list, blk16, blk32):
    size = n_list[0].shape[0]
    ri = lax.broadcasted_iota(jnp.int32, (size, size), 0)
    ci = lax.broadcasted_iota(jnp.int32, (size, size), 1)
    eye = jnp.where(ri == ci, 1.0, 0.0)
    n16 = [jnp.where(blk16, n, 0.0) for n in n_list]
    n32 = [jnp.where(blk32 & jnp.logical_not(blk16), n, 0.0).astype(BF16) for n in n_list]
    n64 = [jnp.where(blk32, 0.0, n).astype(BF16) for n in n_list]
    t = [eye + n for n in n16]
    m = [n.astype(BF16) for n in n16]
    for it in range(3):
        m = [_mm(x, x).astype(BF16) for x in m]
        t = [ti + _mm(mi, ti) for mi, ti in zip(m, t)]
    for off_diag in (n32, n64):
        tb = [ti.astype(BF16) for ti in t]
        w = [_mm(ti, ni).astype(BF16) for ti, ni in zip(tb, off_diag)]
        t = [ti + _mm(wi, tbi) for ti, wi, tbi in zip(t, w, tb)]
    return t


def _rwkv_kernel(has_vres, *refs):
    if has_vres:
        (r_ref, k_ref, v_ref, small_ref, vf_ref,
         mu_r_ref, mu_k_ref, mu_v_ref, mu_small_ref,
         w0_ref, w2_ref, a0_ref, a2_ref, g2_ref, v0_ref, v2_ref,
         kk_ref, ka_ref, rk_ref, lnw_ref, lnb_ref, seg_ref, tril_ref,
         y_ref, s_ref, carry_ref) = refs
    else:
        (r_ref, k_ref, v_ref, small_ref,
         mu_r_ref, mu_k_ref, mu_v_ref, mu_small_ref,
         w0_ref, w2_ref, a0_ref, a2_ref, g2_ref,
         kk_ref, ka_ref, rk_ref, lnw_ref, lnb_ref, seg_ref, tril_ref,
         y_ref, vf_out_ref, s_ref, carry_ref) = refs

    first = pl.program_id(1) == 0
    nb, tt = r_ref.shape[0], r_ref.shape[1]

    @pl.when(first)
    def _():
        s_ref[...] = jnp.zeros_like(s_ref)
        carry_ref[...] = jnp.zeros_like(carry_ref)

    seg = seg_ref[...]
    tiles = {}

    def prologue(bi):
        srcs = (r_ref, k_ref, v_ref, small_ref)
        mus = (mu_r_ref, mu_k_ref, mu_v_ref, mu_small_ref)
        shifted = []
        off = 0
        crow = bi * SUBLANE
        for src, mu in zip(srcs, mus):
            p = src[bi].astype(F32)
            width = p.shape[1]
            prev_row = carry_ref[crow:crow + 1, off:off + width]
            rolled = pltpu.roll(p, 1, 0)
            row = lax.broadcasted_iota(jnp.int32, p.shape, 0)
            prev = jnp.where(row == 0, prev_row, rolled)
            shifted.append(p + (prev - p) * mu[...])
            carry_ref[crow:crow + 1, off:off + width] = p[tt - 1:tt, :]
            off += width
        r, k, v, small = shifted
        s_g = small[:, SMALL_G:SMALL_G + LANE]
        s_wa = small[:, SMALL_WA:SMALL_WA + LANE]
        s_misc = small[:, SMALL_MISC:SMALL_MISC + LANE]

        w_log = -_softplus(-(w0_ref[...] + _mm(jnp.tanh(s_wa), w2_ref[...]))) - 0.5
        lw = -jnp.exp(w_log)
        a = _sigmoid(a0_ref[...] + _mm(s_wa, a2_ref[...]))
        g = _mm(_sigmoid(s_g), g2_ref[...])
        if has_vres:
            v = v + (vf_ref[bi] - v) * _sigmoid(v0_ref[...] + _mm(s_misc, v2_ref[...]))
        else:
            vf_out_ref[bi] = v
        kk = k * kk_ref[...]
        k = k * (1.0 + (a - 1.0) * ka_ref[...])
        sums = _head_sums(jnp.concatenate([kk * kk, r * k * rk_ref[...]], axis=0), seg)
        kk = kk / jnp.maximum(jnp.sqrt(sums[:tt]), 1e-12)
        c_all = _cumsum_rows(tril_ref[...], lw)
        tiles[bi] = dict(r=r, k=k, v=v, g=g, bonus=sums[tt:] * v, alpha=-kk, beta=kk * a,
                         c=c_all, ce=c_all - lw)

    gs = RW_GROUP
    n_groups = RW_HEADS // RW_GROUP_HEADS
    n_st = RW_GROUP_HEADS * CHUNK
    ri = lax.broadcasted_iota(jnp.int32, (n_st, n_st), 0)
    ci = lax.broadcasted_iota(jnp.int32, (n_st, n_st), 1)
    strict = ci < ri
    sh = INV_BLOCK.bit_length() - 1
    blk16 = (ri >> sh) == (ci >> sh)
    blk32 = (ri >> (sh + 1)) == (ci >> (sh + 1))
    tcol = lax.broadcasted_iota(jnp.int32, (CHUNK, n_st), 1) & (CHUNK - 1)
    trow = lax.broadcasted_iota(jnp.int32, (CHUNK, n_st), 0)
    incl = tcol <= trow
    st = functools.partial(_stack_heads, heads=RW_GROUP_HEADS, width=RW_HD)

    for bi in range(nb):
        prologue(bi)
    blocks = _rwkv_blocks(tiles, range(nb), tt, n_groups, gs, st, strict, incl, blk16, blk32)

    n_state = nb * n_groups
    state = [s_ref[si] for si in range(n_state)]
    y_rows = [[] for _ in range(nb)]
    for j in range(tt // CHUNK):
        cur = sorted([b for b in blocks if b["j"] == j], key=lambda b: b["si"])
        sb = [state[b["si"]].astype(BF16) for b in cur]
        n_st = cur[0]["aem"].shape[0]
        ss = [_mm_nt(b["aem_rs"], s) for b, s in zip(cur, sb)]
        us = [_mm(b["t_inv"], s_[:n_st] + b["w1"]).astype(BF16) for b, s_ in zip(cur, ss)]
        for b, u in zip(cur, us):
            state[b["si"]] = state[b["si"]] * b["dec"] + _mm_tn(u, b["bem"]) + b["vk"]
        ys = [s_[n_st:] + _mm(b["a_rb"], u) + b["y0"] for b, s_, u in zip(cur, ss, us)]
        for bi in range(nb):
            y_rows[bi].append(jnp.concatenate(ys[bi * n_groups:(bi + 1) * n_groups], axis=1))
    for si in range(n_state):
        s_ref[si] = state[si]

    inv_hd = 1.0 / RW_HD
    for bi in range(nb):
        y = jnp.concatenate(y_rows[bi], axis=0)
        mean = _head_sums(y, seg) * inv_hd
        yc = y - mean
        var = _head_sums(yc * yc, seg) * inv_hd
        y = yc * lax.rsqrt(var + RW_GN_EPS) * lnw_ref[...] + lnb_ref[...]
        y_ref[bi] = ((y + tiles[bi]["bonus"]) * tiles[bi]["g"]).astype(y_ref.dtype)


def _rwkv_blocks(tiles, batch_rows, tt, n_groups, gs, st, strict, incl, blk16, blk32):
    blocks = []
    for bi, j in [(bi, j) for j in range(tt // CHUNK) for bi in batch_rows]:
        rows = slice(j * CHUNK, (j + 1) * CHUNK)
        tl = tiles[bi]
        c, ce = tl["c"][rows], tl["ce"][rows]
        rj, kj, vj, al, be_ = tl["r"][rows], tl["k"][rows], tl["v"][rows], tl["alpha"][rows], tl["beta"][rows]
        mid = c[CHUNK // 2 - 1:CHUNK // 2, :]
        c_last = c[CHUNK - 1:CHUNK, :]
        e_out = jnp.exp(mid - c)
        e_end = jnp.exp(c_last - c)
        at = al * jnp.exp(ce - mid)
        rt = rj * jnp.exp(c - mid)
        bh = be_ * e_out
        kh = kj * e_out
        ae = al * jnp.exp(ce)
        rs = rj * jnp.exp(c)
        be = be_ * e_end
        ke = kj * e_end
        dec = jnp.exp(c_last)
        for gi in range(n_groups):
            sl = slice(gi * gs, (gi + 1) * gs)
            stb = lambda a_: st(a_[:, sl].astype(BF16))
            blocks.append(dict(j=j, si=bi * n_groups + gi, bm=stb(bh), km=stb(kh), am=stb(at), v_bd=stb(vj),
                               aem=stb(ae),
                               bem=stb(be), kem=stb(ke), rt=rt[:, sl].astype(BF16),
                               rs=rs[:, sl].astype(BF16), dec=dec[:, sl]))
    n_st = strict.shape[0]
    for b in blocks:
        b["lhs"] = jnp.concatenate([b["am"], b["rt"]], axis=0)
    for b in blocks:
        prod = _mm_nt(b["lhs"], b["bm"])
        b["n_ab"] = jnp.where(strict, prod[:n_st], 0.0)
        b["a_rb"] = jnp.where(incl, prod[n_st:], 0.0).astype(BF16)
    for b in blocks:
        prod = _mm_nt(b["lhs"], b["km"])
        b["n_ak"] = jnp.where(strict, prod[:n_st], 0.0).astype(BF16)
        b["a_rk"] = jnp.where(incl, prod[n_st:], 0.0).astype(BF16)
    for b, t_inv in zip(blocks, _unit_lower_inverses([b["n_ab"] for b in blocks], blk16, blk32)):
        b["t_inv"] = t_inv.astype(BF16)
    for b in blocks:
        prod = _mm(jnp.concatenate([b["n_ak"], b["a_rk"]], axis=0), b["v_bd"])
        b["w1"] = prod[:n_st]
        b["y0"] = prod[n_st:]
        b["vk"] = _mm_tn(b["v_bd"], b["kem"])
        b["aem_rs"] = jnp.concatenate([b["aem"], b["rs"]], axis=0)
    return blocks


def _rwkv_mixer(p, v_first, mus, prm, seg, tril, tt, nb):
    _, bsz, tp, _ = p.shape
    nt = tp // tt
    has_vres = v_first is not None
    row_spec = pl.BlockSpec((nb, tt, RW_DIM), lambda b, t: (b, t, 0))
    col = functools.partial(_pcol, nb=nb, tt=tt)
    in_specs = [col("rw_r"), col("rw_k"), col("rw_v"), col(CT_SMALL)]
    args = [p, p, p, p]
    if has_vres:
        in_specs.append(row_spec)
        args.append(v_first)
    consts = list(mus) + [prm["w0"], prm["w2"], prm["a0"], prm["a2"], prm["g2"]]
    if has_vres:
        consts += [prm["v0"], prm["v2"]]
    consts += [prm["kk"], prm["ka"], prm["rk"], prm["lnw"], prm["lnb"], seg, tril]
    in_specs += [_const_spec(c.shape) for c in consts]
    args += consts
    out_shape = [jax.ShapeDtypeStruct((bsz, tp, RW_DIM), BF16)]
    out_specs = [row_spec]
    if not has_vres:
        out_shape.append(jax.ShapeDtypeStruct((bsz, tp, RW_DIM), F32))
        out_specs.append(row_spec)
    carry_w = 3 * RW_DIM + IN_TN
    res = pl.pallas_call(
        functools.partial(_rwkv_kernel, has_vres),
        grid=(bsz // nb, nt),
        in_specs=in_specs,
        out_specs=out_specs,
        out_shape=out_shape,
        scratch_shapes=[pltpu.VMEM((nb * (RW_HEADS // RW_GROUP_HEADS), RW_GROUP, RW_GROUP), F32),
                        pltpu.VMEM((nb * SUBLANE, carry_w), F32)],
        compiler_params=_seq_params(),
        name="rwkv_mixer",
    )(*args)
    y = res[0].reshape(bsz * tp, RW_DIM)
    return (y, v_first) if has_vres else (y, res[1])


def _merge_kernel(z_ref, gg_ref, gr_ref, gh_ref, yg_ref, yr_ref, yh_ref, wa_ref, wb_ref, wc_ref, wo_ref, o_ref):
    d = lambda y, w: jnp.dot(y[...], w[...], preferred_element_type=F32)
    sg = lambda g: _sigmoid(jnp.concatenate([g[t] for t in range(g.shape[0])], axis=1).astype(F32))
    m = sg(gg_ref) * d(yg_ref, wa_ref) + sg(gr_ref) * d(yr_ref, wb_ref) + sg(gh_ref) * d(yh_ref, wc_ref)
    o_ref[...] = z_ref[...] + _mm(m, wo_ref[...])


def _merge(z2d, p, y_gla, y_rw, y_hg, wa, wb, wc, wo, tm):
    n = z2d.shape[0]
    row = lambda w: pl.BlockSpec((tm, w), lambda i: (i, 0))
    gate = lambda name: pl.BlockSpec((D_MODEL // IN_TN, tm, IN_TN),
                                     lambda i, cb=_K_OFF[name][0] // D_MODEL: (cb, i, 0))
    full = lambda a: pl.BlockSpec(a.shape, lambda i: (0, 0))
    return pl.pallas_call(
        _merge_kernel,
        grid=(n // tm,),
        in_specs=[row(D_MODEL), gate("gate_gla"), gate("gate_rw"), gate("gate_hg"),
                  row(GLA_V), row(RW_DIM), row(HG_V), full(wa), full(wb), full(wc), full(wo)],
        out_specs=row(D_MODEL),
        out_shape=jax.ShapeDtypeStruct((n, D_MODEL), F32),
        compiler_params=pltpu.CompilerParams(dimension_semantics=("arbitrary",), vmem_limit_bytes=VMEM_LIMIT),
        name="merge",
    )(z2d, p, p, p, y_gla, y_rw, y_hg, wa, wb, wc, wo)


FFN_FC = 256
FFN_HALO = SUBLANE
assert D_FF % FFN_FC == 0 and FFN_FC % LANE == 0 and FFN_HALO >= CONV_W - 1


def _ffn_kernel(final, z_ref, halo_ref, nw_ref, wu_ref, cw_ref, cb_ref, wd_ref, pnw_ref, o_ref, *rest):
    h_ref, act_ref = (None, rest[0]) if final else rest
    tt = z_ref.shape[0]
    zt = z_ref[...]
    nw = nw_ref[...]
    h = jnp.concatenate([_rms(halo_ref[...], nw), _rms(zt, nw)], axis=0).astype(BF16)
    has_prev = pl.program_id(1) > 0
    fc = FFN_FC
    n_chunks = D_FF // fc
    cols = lambda j, half: slice(half * D_FF + j * fc, half * D_FF + (j + 1) * fc)

    def up(j):
        return (jnp.dot(h, wu_ref[:, cols(j, 0)], preferred_element_type=F32),
                jnp.dot(h, wu_ref[:, cols(j, 1)], preferred_element_type=F32))

    def conv(u, cw, cb):
        u = jnp.concatenate([jnp.where(has_prev, u[:FFN_HALO], 0.0), u[FFN_HALO:]], axis=0)
        inner = u * cw[1:2, :] + pltpu.roll(u * cw[0:1, :], 1, 0)
        c = cb + u * cw[2:3, :] + pltpu.roll(inner, 1, 0)
        return c[FFN_HALO:, :]

    split = (n_chunks + 1) // 2
    out = zt
    u_next = up(0)
    for j in range(n_chunks):
        ug, uv = u_next
        if j + 1 < n_chunks:
            u_next = up(j + 1)
        cg = conv(ug, cw_ref[:, cols(j, 0)], cb_ref[:, cols(j, 0)])
        cv = conv(uv, cw_ref[:, cols(j, 1)], cb_ref[:, cols(j, 1)])
        act_ref[:, j * fc:(j + 1) * fc] = (cg * cv / (1.0 + jnp.exp(-cg))).astype(BF16)
        if j + 1 in (split, n_chunks):
            lo = 0 if j + 1 == split else split
            out = out + jnp.dot(act_ref[:, lo * fc:(j + 1) * fc], wd_ref[lo * fc:(j + 1) * fc, :],
                                preferred_element_type=F32)
    if final:
        o_ref[...] = _rms(out, pnw_ref[...])
    else:
        o_ref[...] = out
        h_ref[...] = _rms(out, pnw_ref[...]).astype(h_ref.dtype)


def _ffn(z3d, norm_w, wu, cw, cb, wd, post_norm_w, tt, final):
    bsz, tp, _ = z3d.shape
    hb = tt // FFN_HALO
    once = lambda a: pl.BlockSpec(a.shape, lambda b, t: (0,) * a.ndim, pipeline_mode=pl.Buffered(1))
    tile = pl.BlockSpec((None, tt, D_MODEL), lambda b, t: (b, t, 0))
    out_shape = [jax.ShapeDtypeStruct((bsz, tp, D_MODEL), F32)]
    if not final:
        out_shape.append(jax.ShapeDtypeStruct((bsz, tp, D_MODEL), BF16))
    res = pl.pallas_call(
        functools.partial(_ffn_kernel, final),
        grid=(bsz, tp // tt),
        in_specs=[tile,
                  pl.BlockSpec((None, FFN_HALO, D_MODEL), lambda b, t: (b, jnp.maximum(t * hb - 1, 0), 0)),
                  once(norm_w), once(wu), once(cw), once(cb), once(wd), once(post_norm_w)],
        out_specs=[tile] * len(out_shape),
        out_shape=out_shape,
        scratch_shapes=[pltpu.VMEM((tt, D_FF), BF16)],
        compiler_params=_seq_params(),
        name="conv_ffn",
    )(z3d, z3d, norm_w, wu, cw, cb, wd, post_norm_w)
    return res[0] if final else (res[0], res[1])


def _largest_tile(n, cands):
    for c in cands:
        if n % c == 0:
            return c
    raise ValueError(f"no tile for {n}")


def _ref_cols(w, name):
    off, width = _REF_OFF[name]
    return w[..., off:off + width]


def _pack_columns(w, vres):
    lead = w.shape[:-1]
    zeros = lambda n: jnp.zeros(lead + (n,), w.dtype)
    vr = vres if vres is not None else zeros(RW_V_RANK)
    parts = []
    for name, width in _K_LAYOUT:
        if name == "rw_wa":
            parts.append(jnp.concatenate([_ref_cols(w, "rw_w"), _ref_cols(w, "rw_a")], axis=-1))
        elif name == "misc":
            parts.append(jnp.concatenate(
                [_ref_cols(w, "gla_gk"), vr, zeros(width - GLA_GATE_RANK - RW_V_RANK)], axis=-1))
        else:
            parts.append(_ref_cols(w, name))
    used = sum(wd for _, wd in _K_LAYOUT)
    parts.append(zeros(WP - used))
    return jnp.concatenate(parts, axis=-1)


def _pad_rows(w, lo, total):
    return jnp.pad(w, ((lo, total - lo - w.shape[0]), (0, 0)))


def kernel(x, meta, mix_norm, w_in, w_in_vres, rw_mu, rw_mu_vres, gla_gk_up, gla_gk_bias, gla_norm, rw_w0,
           rw_w2, rw_a0, rw_a2, rw_v0, rw_v2, rw_g2, rw_kk, rw_ka, rw_rk, rw_ln_w, rw_ln_b, hg_lb_logits,
           hg_norm, w_out_gla, w_out_rw, w_out_hg, w_out, ffn_norm, w_up, conv_w, conv_b, w_down, final_norm):
    bsz, seq, _ = x.shape
    depth = w_in.shape[0]
    t_real = seq + N_META
    tp = -(-t_real // CHUNK) * CHUNK
    n = bsz * tp
    tm = _largest_tile(n, (3072, 2048, 1024, 512, 256, 128, 64))
    tm_merge = _largest_tile(n, (512, 256, 128, 64))
    tt = _largest_tile(tp, (704, 512, 384, 256, 192, 128, 64))
    tt_mix = _largest_tile(tp, (3 * CHUNK, 2 * CHUNK, CHUNK))
    nb_mix = _largest_tile(bsz, (4, 2, 1))

    z = jnp.concatenate([jnp.broadcast_to(meta.astype(x.dtype)[None], (bsz, N_META, D_MODEL)), x,
                         jnp.zeros((bsz, tp - t_real, D_MODEL), x.dtype)], axis=1)

    row = lambda a: a.reshape(1, -1).astype(F32)
    t_idx = jnp.arange(tt_mix)
    tril = ((t_idx[:, None] // CHUNK == t_idx[None, :] // CHUNK)
            & (t_idx[None, :] <= t_idx[:, None])).astype(BF16)
    head_of = jnp.arange(RW_GROUP) // RW_HD
    seg = (head_of[:, None] == head_of[None, :]).astype(BF16)
    lb_p = jax.nn.softmax(hg_lb_logits.astype(F32), axis=0)
    lb_all = jnp.cumsum(lb_p, axis=0) - lb_p[0:1]

    v_first = None
    h_next = None
    for i in range(depth):
        vres_w = w_in_vres[i - 1] if i > 0 else None
        vres_mu = rw_mu_vres[i - 1] if i > 0 else None
        w_cat = _pack_columns(w_in[i], vres_w).astype(BF16)
        mu_full = jnp.concatenate([jnp.zeros((W_IN - rw_mu.shape[1],), F32), rw_mu[i].astype(F32)])
        mu_cat = _pack_columns(mu_full, vres_mu).reshape(N_CT, 1, IN_TN)
        mus = [mu_cat[_K_OFF[nm][0] // IN_TN] for nm in ("rw_r", "rw_k", "rw_v", "rw_g")]

        z2d = z.reshape(n, D_MODEL)
        p = _inproj(z2d if h_next is None else h_next.reshape(n, D_MODEL), row(mix_norm[i]), w_cat, tm)

        gk_up_pad = _pad_rows(gla_gk_up[i], MISC_GK, LANE).astype(BF16)
        p4 = p.reshape(N_CT, bsz, tp, IN_TN)
        y_gla = _gla_mixer(p4, gk_up_pad, row(gla_gk_bias[i]), row(gla_norm[i]), tril, tt_mix, nb_mix)
        y_hg = _hgrn_mixer(p4, row(lb_all[i]), row(hg_norm[i]), tril, tt_mix, nb_mix)

        prm = {
            "w0": row(rw_w0[i]), "w2": _pad_rows(rw_w2[i], 0, LANE).astype(BF16),
            "a0": row(rw_a0[i]), "a2": _pad_rows(rw_a2[i], RW_W_RANK, LANE).astype(BF16),
            "g2": rw_g2[i].astype(BF16),
            "kk": row(rw_kk[i]), "ka": row(rw_ka[i]), "rk": row(rw_rk[i]),
            "lnw": row(rw_ln_w[i]), "lnb": row(rw_ln_b[i]),
        }
        if i > 0:
            prm["v0"] = row(rw_v0[i - 1])
            prm["v2"] = _pad_rows(rw_v2[i - 1], MISC_VR, LANE).astype(BF16)
        y_rw, v_first = _rwkv_mixer(p4, v_first, mus, prm, seg, tril, tt_mix, nb_mix)

        z2d = _merge(z2d, p, y_gla, y_rw, y_hg, w_out_gla[i].astype(BF16), w_out_rw[i].astype(BF16),
                     w_out_hg[i].astype(BF16), w_out[i].astype(BF16), tm_merge)

        cw = jnp.pad(conv_w[i].astype(F32), ((0, SUBLANE - CONV_W), (0, 0)))
        last = i == depth - 1
        res = _ffn(z2d.reshape(bsz, tp, D_MODEL), row(ffn_norm[i]), w_up[i].astype(BF16), cw, row(conv_b[i]),
                   w_down[i].astype(BF16), row(final_norm if last else mix_norm[i + 1]), tt, final=last)
        z, h_next = (res, None) if last else res
    return z[:, N_META:t_real]
```

```python
import functools

import jax
import jax.numpy as jnp
from jax import lax
from jax.experimental import pallas as pl
from jax.experimental.pallas import tpu as pltpu

F32 = jnp.float32
BF16 = jnp.bfloat16

D_MODEL = 1024
N_META = 16
F_TINY = 1e-30
NORM_EPS = 1e-6

GLA_HEADS, GLA_DK, GLA_DV = 4, 64, 128
GLA_K, GLA_V = GLA_HEADS * GLA_DK, GLA_HEADS * GLA_DV
GLA_GATE_RANK = 16
GLA_GATE_NORM = 16.0

RW_HEADS, RW_HD = 8, 64
RW_DIM = RW_HEADS * RW_HD
RW_W_RANK, RW_A_RANK, RW_V_RANK, RW_G_RANK = 64, 64, 32, 128
RW_GN_EPS = 64e-5
RW_GROUP_HEADS = 4
RW_GROUP = RW_GROUP_HEADS * RW_HD

HG_HEADS, HG_DK, HG_DV = 4, 128, 128
HG_K, HG_V = HG_HEADS * HG_DK, HG_HEADS * HG_DV

D_FF = 2816
CONV_W = 3

_REF_LAYOUT = (
    ("gla_q", GLA_K), ("gla_k", GLA_K), ("gla_v", GLA_V), ("gla_gk", GLA_GATE_RANK), ("gla_g", GLA_V),
    ("hg_q", HG_K), ("hg_f", HG_K), ("hg_i", HG_V), ("hg_g", HG_V),
    ("gate_gla", D_MODEL), ("gate_rw", D_MODEL), ("gate_hg", D_MODEL),
    ("rw_r", RW_DIM), ("rw_w", RW_W_RANK), ("rw_k", RW_DIM), ("rw_v", RW_DIM), ("rw_a", RW_A_RANK),
    ("rw_g", RW_G_RANK),
)
_REF_OFF = {}
_o = 0
for _n, _w in _REF_LAYOUT:
    _REF_OFF[_n] = (_o, _w)
    _o += _w
W_IN = _o

LANE = 128
SUBLANE = 8
_K_LAYOUT = (
    ("gate_gla", 1024), ("gate_rw", 1024), ("gate_hg", 1024),
    ("gla_v", 512), ("gla_g", 512), ("hg_q", 512), ("hg_f", 512), ("hg_i", 512), ("hg_g", 512),
    ("rw_r", 512), ("rw_k", 512), ("rw_v", 512),
    ("gla_q", 256), ("gla_k", 256),
    ("rw_g", 128), ("rw_wa", 128), ("misc", 128),
)
_K_OFF = {}
_o = 0
for _n, _w in _K_LAYOUT:
    assert _o % _w == 0
    _K_OFF[_n] = (_o, _w)
    _o += _w
IN_TN = 512
WP = -(-_o // IN_TN) * IN_TN
N_CT = WP // IN_TN
CT_QK = _K_OFF["gla_q"][0] // IN_TN
CT_SMALL = _K_OFF["rw_g"][0] // IN_TN
assert _K_OFF["gla_k"][0] // IN_TN == CT_QK and _K_OFF["misc"][0] // IN_TN == CT_SMALL
SMALL_G, SMALL_WA, SMALL_MISC = (_K_OFF[_n][0] - CT_SMALL * IN_TN for _n in ("rw_g", "rw_wa", "misc"))
MISC_GK = 0
MISC_VR = GLA_GATE_RANK

CHUNK = 64
SUB = 32
INV_BLOCK = 16
assert CHUNK == 4 * INV_BLOCK
VMEM_LIMIT = 56 * 1024 * 1024


def _mm(a, b):
    return jnp.dot(a.astype(BF16), b.astype(BF16), preferred_element_type=F32)


def _mm_nt(a, b):
    return lax.dot_general(a.astype(BF16), b.astype(BF16), (((1,), (1,)), ((), ())),
                           preferred_element_type=F32)


def _mm_tn(a, b):
    return lax.dot_general(a.astype(BF16), b.astype(BF16), (((0,), (0,)), ((), ())),
                           preferred_element_type=F32)


def _split2(x):
    hi = x.astype(BF16)
    lo = (x - hi.astype(F32)).astype(BF16)
    return hi, lo


def _head_sums(x, seg_group):
    gw = seg_group.shape[0]
    xb = x.astype(BF16)
    return jnp.concatenate([jnp.dot(xb[:, i:i + gw], seg_group, preferred_element_type=F32)
                            for i in range(0, x.shape[1], gw)], axis=1)


def _cumsum_rows(tril_bf16, g):
    w = g.shape[1]
    out = jnp.dot(tril_bf16, jnp.concatenate(_split2(g), axis=1), preferred_element_type=F32)
    return out[:, :w] + out[:, w:]


def _sigmoid(x):
    return 1.0 / (1.0 + jnp.exp(-x))


def _softplus(x):
    return jnp.maximum(x, 0.0) + jnp.log(1.0 + jnp.exp(-jnp.abs(x)))


def _rms(x, w):
    return x * lax.rsqrt(jnp.mean(x * x, axis=-1, keepdims=True) + NORM_EPS) * w


def _stack_heads(x, heads, width):
    lane = lax.broadcasted_iota(jnp.int32, x.shape, 1)
    return jnp.concatenate(
        [jnp.where((lane >= h * width) & (lane < (h + 1) * width), x, 0.0) for h in range(heads)], axis=0)


def _inproj_kernel(z_ref, nw_ref, w_ref, o_ref, *scratch):
    if not scratch:
        h = z_ref[...]
    else:
        h_ref, = scratch

        @pl.when(pl.program_id(1) == 0)
        def _():
            h_ref[...] = _rms(z_ref[...], nw_ref[...]).astype(BF16)

        h = h_ref[...]
    o_ref[...] = jnp.dot(h, w_ref[...], preferred_element_type=F32).astype(o_ref.dtype)


def _inproj(z2d, norm_w, w_bf16, tm):
    n = z2d.shape[0]
    return pl.pallas_call(
        _inproj_kernel,
        grid=(n // tm, N_CT),
        in_specs=[
            pl.BlockSpec((tm, D_MODEL), lambda i, j: (i, 0)),
            pl.BlockSpec((1, D_MODEL), lambda i, j: (0, 0)),
            pl.BlockSpec((D_MODEL, IN_TN), lambda i, j: (0, j)),
        ],
        out_specs=pl.BlockSpec((None, tm, IN_TN), lambda i, j: (j, i, 0)),
        out_shape=jax.ShapeDtypeStruct((N_CT, n, IN_TN), BF16),
        scratch_shapes=[] if z2d.dtype == BF16 else [pltpu.VMEM((tm, D_MODEL), BF16)],
        compiler_params=pltpu.CompilerParams(
            dimension_semantics=("arbitrary", "arbitrary"), vmem_limit_bytes=VMEM_LIMIT),
        name="inproj",
    )(z2d, norm_w, w_bf16)


def _gla_tiles(qkvg, tril_bd, s_ref, norm_w, heads, dk, group_heads, first):
    tt, hk = qkvg[0][0].shape
    assert tt % CHUNK == 0 and CHUNK == 2 * SUB and heads % group_heads == 0
    dv = qkvg[0][2].shape[1] // heads
    nb = len(qkvg)
    gw = group_heads * dk
    n_grp = heads // group_heads

    @pl.when(first)
    def _():
        s_ref[...] = jnp.zeros_like(s_ref)

    c_alls = [_cumsum_rows(tril_bd, g) for _, _, _, g in qkvg]
    top, bot = slice(0, SUB), slice(SUB, CHUNK)
    zeros = jnp.zeros((SUB, gw), F32)
    st = (functools.partial(_stack_heads, heads=group_heads, width=dk) if group_heads > 1 else (lambda a_: a_))
    n_st = group_heads * CHUNK
    causal = (lax.broadcasted_iota(jnp.int32, (n_st, n_st), 1)
              <= lax.broadcasted_iota(jnp.int32, (n_st, n_st), 0))
    chunks = []
    for bi, j, gi in [(bi, j, gi) for j in range(tt // CHUNK) for bi in range(nb) for gi in range(n_grp)]:
        rows = slice(j * CHUNK, (j + 1) * CHUNK)
        lanes = slice(gi * gw, (gi + 1) * gw)
        q, k, v, _ = qkvg[bi]
        c, qj, kj = c_alls[bi][rows, lanes], q[rows, lanes], k[rows, lanes]
        vj = v[rows, gi * group_heads * dv:(gi + 1) * group_heads * dv]
        r0 = c[SUB // 2 - 1:SUB // 2, :]
        r1 = c[SUB + SUB // 2 - 1:SUB + SUB // 2, :]
        c_last = c[CHUNK - 1:CHUNK, :]
        kt0 = jnp.concatenate([kj[top] * jnp.exp(r0 - c[top]), zeros], axis=0)
        kt1 = jnp.concatenate([zeros, kj[bot] * jnp.exp(r1 - c[bot])], axis=0)
        qa = qj * jnp.exp(c - r0)
        qb = jnp.concatenate([zeros, qj[bot] * jnp.exp(c[bot] - r1)], axis=0)
        qs = jnp.concatenate([st(qa), st(qb)], axis=1).astype(BF16)
        ks = jnp.concatenate([st(kt0), st(kt1)], axis=1).astype(BF16)
        vs = jnp.concatenate([vj[:, h * dv:(h + 1) * dv] for h in range(group_heads)], axis=0).astype(BF16)
        chunks.append(dict(bi=bi, gi=gi, qs=qs, ks=ks, vs=vs, qe=st(qj * jnp.exp(c)).astype(BF16),
                           kd=st(kj * jnp.exp(c_last - c)).astype(BF16), dec=jnp.exp(c_last)))
    for ch in chunks:
        ch["a"] = jnp.where(causal, _mm_nt(ch["qs"], ch["ks"]), 0.0)
    for ch in chunks:
        ch["oa"] = _mm(ch["a"], ch["vs"])
        ch["kv"] = _mm_tn(ch["vs"], ch["kd"])
    s_t = {(bi, gi): s_ref[bi, :, gi * gw:(gi + 1) * gw] for bi in range(nb) for gi in range(n_grp)}
    outs = [[[] for _ in range(n_grp)] for _ in range(nb)]
    for ch in chunks:
        key = (ch["bi"], ch["gi"])
        o = ch["oa"] + _mm_nt(ch["qe"], s_t[key])
        s_t[key] = s_t[key] * ch["dec"] + ch["kv"]
        o = o * lax.rsqrt(jnp.mean(o * o, axis=-1, keepdims=True) + NORM_EPS) * norm_w
        outs[ch["bi"]][ch["gi"]].append(
            jnp.concatenate([o[h * CHUNK:(h + 1) * CHUNK] for h in range(group_heads)], axis=1))
    for (bi, gi), s in s_t.items():
        s_ref[bi, :, gi * gw:(gi + 1) * gw] = s
    return [jnp.concatenate([jnp.concatenate(grp, axis=0) for grp in row_outs], axis=1) for row_outs in outs]


def _silu(x):
    return x * _sigmoid(x)


def _gla_kernel(qk_ref, v_ref, small_ref, og_ref, gkup_ref, gkb_ref, nw_ref, tril_ref, y_ref, s_ref):
    qkvg = []
    for bi in range(qk_ref.shape[0]):
        qk = qk_ref[bi].astype(F32)
        x = _mm(small_ref[bi, :, SMALL_MISC:SMALL_MISC + LANE], gkup_ref[...]) + gkb_ref[...]
        g = -_softplus(-x) * (1.0 / GLA_GATE_NORM)
        qkvg.append((qk[:, :GLA_K] * (GLA_DK ** -0.5), qk[:, GLA_K:], v_ref[bi].astype(F32), g))
    outs = _gla_tiles(qkvg, tril_ref[...], s_ref, nw_ref[...], GLA_HEADS, GLA_DK, LANE // GLA_DK,
                      pl.program_id(1) == 0)
    for bi, o in enumerate(outs):
        y_ref[bi] = (o * _silu(og_ref[bi].astype(F32))).astype(y_ref.dtype)


def _hgrn_kernel(q_ref, f_ref, i_ref, og_ref, lb_ref, nw_ref, tril_ref, y_ref, s_ref):
    lb = lb_ref[...]
    qkvg = []
    for bi in range(q_ref.shape[0]):
        z = f_ref[bi].astype(F32)
        forget = lb + (1.0 - lb) * _sigmoid(z)
        g = jnp.log(jnp.maximum(forget, F_TINY))
        qkvg.append((q_ref[bi].astype(F32), 1.0 - forget, i_ref[bi].astype(F32), g))
    outs = _gla_tiles(qkvg, tril_ref[...], s_ref, nw_ref[...], HG_HEADS, HG_DK, 1, pl.program_id(1) == 0)
    for bi, o in enumerate(outs):
        y_ref[bi] = (o * _silu(og_ref[bi].astype(F32))).astype(y_ref.dtype)


def _gla_hgrn_kernel(qk_ref, v_ref, small_ref, og_ref, gkup_ref, gkb_ref, gnw_ref,
                     q_ref, f_ref, i_ref, hog_ref, lb_ref, hnw_ref, tril_ref, yg_ref, yh_ref, sg_ref, sh_ref):
    _gla_kernel(qk_ref, v_ref, small_ref, og_ref, gkup_ref, gkb_ref, gnw_ref, tril_ref, yg_ref, sg_ref)
    _hgrn_kernel(q_ref, f_ref, i_ref, hog_ref, lb_ref, hnw_ref, tril_ref, yh_ref, sh_ref)


def _gla_hgrn_mixer(p, gk_up_pad, gk_bias, gla_norm_w, lb, hg_norm_w, tril, tt, nb):
    _, bsz, tp, _ = p.shape
    col = functools.partial(_pcol, nb=nb, tt=tt)
    consts_g = [gk_up_pad, gk_bias, gla_norm_w]
    consts_h = [lb, hg_norm_w, tril]
    out_spec = lambda w: pl.BlockSpec((nb, tt, w), lambda b, t: (b, t, 0))
    yg, yh = pl.pallas_call(
        _gla_hgrn_kernel,
        grid=(bsz // nb, tp // tt),
        in_specs=([col(CT_QK), col("gla_v"), col(CT_SMALL), col("gla_g")] + [_const_spec(c.shape) for c in consts_g]
                  + [col("hg_q"), col("hg_f"), col("hg_i"), col("hg_g")] + [_const_spec(c.shape) for c in consts_h]),
        out_specs=[out_spec(GLA_V), out_spec(HG_V)],
        out_shape=[jax.ShapeDtypeStruct((bsz, tp, GLA_V), BF16), jax.ShapeDtypeStruct((bsz, tp, HG_V), BF16)],
        scratch_shapes=[pltpu.VMEM((nb, GLA_DV, GLA_K), F32), pltpu.VMEM((nb, HG_DV, HG_K), F32)],
        compiler_params=_seq_params(),
        name="gla_hgrn_mixer",
    )(p, p, p, p, *consts_g, p, p, p, p, *consts_h)
    return yg.reshape(bsz * tp, GLA_V), yh.reshape(bsz * tp, HG_V)


def _pcol(name, nb, tt):
    ct = (name if isinstance(name, int) else _K_OFF[name][0] // IN_TN)
    return pl.BlockSpec((None, nb, tt, IN_TN), lambda b, t: (ct, b, t, 0))


def _const_spec(shape):
    nd = len(shape)
    return pl.BlockSpec(shape, lambda b, t: (0,) * nd)


def _seq_params():
    return pltpu.CompilerParams(dimension_semantics=("arbitrary", "arbitrary"), vmem_limit_bytes=VMEM_LIMIT)


def _gla_mixer(p, gk_up_pad, gk_bias, norm_w, tril, tt, nb):
    _, bsz, tp, _ = p.shape
    col = functools.partial(_pcol, nb=nb, tt=tt)
    y = pl.pallas_call(
        _gla_kernel,
        grid=(bsz // nb, tp // tt),
        in_specs=[col(CT_QK), col("gla_v"), col(CT_SMALL), col("gla_g"),
                  _const_spec(gk_up_pad.shape), _const_spec(gk_bias.shape),
                  _const_spec(norm_w.shape), _const_spec(tril.shape)],
        out_specs=pl.BlockSpec((nb, tt, GLA_V), lambda b, t: (b, t, 0)),
        out_shape=jax.ShapeDtypeStruct((bsz, tp, GLA_V), BF16),
        scratch_shapes=[pltpu.VMEM((nb, GLA_DV, GLA_K), F32)],
        compiler_params=_seq_params(),
        name="gla_mixer",
    )(p, p, p, p, gk_up_pad, gk_bias, norm_w, tril)
    return y.reshape(bsz * tp, GLA_V)


def _hgrn_mixer(p, lb, norm_w, tril, tt, nb):
    _, bsz, tp, _ = p.shape
    col = functools.partial(_pcol, nb=nb, tt=tt)
    y = pl.pallas_call(
        _hgrn_kernel,
        grid=(bsz // nb, tp // tt),
        in_specs=[col("hg_q"), col("hg_f"), col("hg_i"), col("hg_g"),
                  _const_spec(lb.shape), _const_spec(norm_w.shape), _const_spec(tril.shape)],
        out_specs=pl.BlockSpec((nb, tt, HG_V), lambda b, t: (b, t, 0)),
        out_shape=jax.ShapeDtypeStruct((bsz, tp, HG_V), BF16),
        scratch_shapes=[pltpu.VMEM((nb, HG_DV, HG_K), F32)],
        compiler_params=_seq_params(),
        name="hgrn_mixer",
    )(p, p, p, p, lb, norm_w, tril)
    return y.reshape(bsz * tp, HG_V)


def _unit_lower_inverses(n_list, blk16, blk32):
    size = n_list[0].shape[0]
    ri = lax.broadcasted_iota(jnp.int32, (size, size), 0)
    ci = lax.broadcasted_iota(jnp.int32, (size, size), 1)
    eye = jnp.where(ri == ci, 1.0, 0.0)
    n16 = [jnp.where(blk16, n, 0.0) for n in n_list]
    n32 = [jnp.where(blk32 & jnp.logical_not(blk16), n, 0.0).astype(BF16) for n in n_list]
    n64 = [jnp.where(blk32, 0.0, n).astype(BF16) for n in n_list]
    t = [eye + n for n in n16]
    m = [n.astype(BF16) for n in n16]
    for it in range(3):
        m = [_mm(x, x).astype(BF16) for x in m]
        t = [ti + _mm(mi, ti) for mi, ti in zip(m, t)]
    for off_diag in (n32, n64):
        tb = [ti.astype(BF16) for ti in t]
        w = [_mm(ti, ni).astype(BF16) for ti, ni in zip(tb, off_diag)]
        t = [ti + _mm(wi, tbi) for ti, wi, tbi in zip(t, w, tb)]
    return t


def _rwkv_kernel(has_vres, *refs):
    if has_vres:
        (r_ref, k_ref, v_ref, small_ref, vf_ref,
         mu_r_ref, mu_k_ref, mu_v_ref, mu_small_ref,
         w0_ref, w2_ref, a0_ref, a2_ref, g2_ref, v0_ref, v2_ref,
         kk_ref, ka_ref, rk_ref, lnw_ref, lnb_ref, seg_ref, tril_ref,
         y_ref, s_ref, carry_ref) = refs
    else:
        (r_ref, k_ref, v_ref, small_ref,
         mu_r_ref, mu_k_ref, mu_v_ref, mu_small_ref,
         w0_ref, w2_ref, a0_ref, a2_ref, g2_ref,
         kk_ref, ka_ref, rk_ref, lnw_ref, lnb_ref, seg_ref, tril_ref,
         y_ref, vf_out_ref, s_ref, carry_ref) = refs

    first = pl.program_id(1) == 0
    nb, tt = r_ref.shape[0], r_ref.shape[1]

    @pl.when(first)
    def _():
        s_ref[...] = jnp.zeros_like(s_ref)
        carry_ref[...] = jnp.zeros_like(carry_ref)

    seg = seg_ref[...]
    tiles = {}

    def prologue(bi):
        srcs = (r_ref, k_ref, v_ref, small_ref)
        mus = (mu_r_ref, mu_k_ref, mu_v_ref, mu_small_ref)
        shifted = []
        off = 0
        crow = bi * SUBLANE
        for src, mu in zip(srcs, mus):
            p = src[bi].astype(F32)
            width = p.shape[1]
            prev_row = carry_ref[crow:crow + 1, off:off + width]
            rolled = pltpu.roll(p, 1, 0)
            row = lax.broadcasted_iota(jnp.int32, p.shape, 0)
            prev = jnp.where(row == 0, prev_row, rolled)
            shifted.append(p + (prev - p) * mu[...])
            carry_ref[crow:crow + 1, off:off + width] = p[tt - 1:tt, :]
            off += width
        r, k, v, small = shifted
        s_g = small[:, SMALL_G:SMALL_G + LANE]
        s_wa = small[:, SMALL_WA:SMALL_WA + LANE]
        s_misc = small[:, SMALL_MISC:SMALL_MISC + LANE]

        w_log = -_softplus(-(w0_ref[...] + _mm(jnp.tanh(s_wa), w2_ref[...]))) - 0.5
        lw = -jnp.exp(w_log)
        a = _sigmoid(a0_ref[...] + _mm(s_wa, a2_ref[...]))
        g = _mm(_sigmoid(s_g), g2_ref[...])
        if has_vres:
            v = v + (vf_ref[bi] - v) * _sigmoid(v0_ref[...] + _mm(s_misc, v2_ref[...]))
        else:
            vf_out_ref[bi] = v
        kk = k * kk_ref[...]
        k = k * (1.0 + (a - 1.0) * ka_ref[...])
        sums = _head_sums(jnp.concatenate([kk * kk, r * k * rk_ref[...]], axis=0), seg)
        kk = kk / jnp.maximum(jnp.sqrt(sums[:tt]), 1e-12)
        c_all = _cumsum_rows(tril_ref[...], lw)
        tiles[bi] = dict(r=r, k=k, v=v, g=g, bonus=sums[tt:] * v, alpha=-kk, beta=kk * a,
                         c=c_all, ce=c_all - lw)

    gs = RW_GROUP
    n_groups = RW_HEADS // RW_GROUP_HEADS
    n_st = RW_GROUP_HEADS * CHUNK
    ri = lax.broadcasted_iota(jnp.int32, (n_st, n_st), 0)
    ci = lax.broadcasted_iota(jnp.int32, (n_st, n_st), 1)
    strict = ci < ri
    sh = INV_BLOCK.bit_length() - 1
    blk16 = (ri >> sh) == (ci >> sh)
    blk32 = (ri >> (sh + 1)) == (ci >> (sh + 1))
    tcol = lax.broadcasted_iota(jnp.int32, (CHUNK, n_st), 1) & (CHUNK - 1)
    trow = lax.broadcasted_iota(jnp.int32, (CHUNK, n_st), 0)
    incl = tcol <= trow
    st = functools.partial(_stack_heads, heads=RW_GROUP_HEADS, width=RW_HD)

    for bi in range(nb):
        prologue(bi)
    blocks = _rwkv_blocks(tiles, range(nb), tt, n_groups, gs, st, strict, incl, blk16, blk32)

    n_state = nb * n_groups
    state = [s_ref[si] for si in range(n_state)]
    y_rows = [[] for _ in range(nb)]
    for j in range(tt // CHUNK):
        cur = sorted([b for b in blocks if b["j"] == j], key=lambda b: b["si"])
        sb = [state[b["si"]].astype(BF16) for b in cur]
        n_st = cur[0]["aem"].shape[0]
        ss = [_mm_nt(b["aem_rs"], s) for b, s in zip(cur, sb)]
        us = [_mm(b["t_inv"], s_[:n_st] + b["w1"]).astype(BF16) for b, s_ in zip(cur, ss)]
        for b, u in zip(cur, us):
            state[b["si"]] = state[b["si"]] * b["dec"] + _mm_tn(u, b["bem"]) + b["vk"]
        ys = [s_[n_st:] + _mm(b["a_rb"], u) + b["y0"] for b, s_, u in zip(cur, ss, us)]
        for bi in range(nb):
            y_rows[bi].append(jnp.concatenate(ys[bi * n_groups:(bi + 1) * n_groups], axis=1))
    for si in range(n_state):
        s_ref[si] = state[si]

    inv_hd = 1.0 / RW_HD
    for bi in range(nb):
        y = jnp.concatenate(y_rows[bi], axis=0)
        mean = _head_sums(y, seg) * inv_hd
        yc = y - mean
        var = _head_sums(yc * yc, seg) * inv_hd
        y = yc * lax.rsqrt(var + RW_GN_EPS) * lnw_ref[...] + lnb_ref[...]
        y_ref[bi] = ((y + tiles[bi]["bonus"]) * tiles[bi]["g"]).astype(y_ref.dtype)


def _rwkv_blocks(tiles, batch_rows, tt, n_groups, gs, st, strict, incl, blk16, blk32):
    blocks = []
    for bi, j in [(bi, j) for j in range(tt // CHUNK) for bi in batch_rows]:
        rows = slice(j * CHUNK, (j + 1) * CHUNK)
        tl = tiles[bi]
        c, ce = tl["c"][rows], tl["ce"][rows]
        rj, kj, vj, al, be_ = tl["r"][rows], tl["k"][rows], tl["v"][rows], tl["alpha"][rows], tl["beta"][rows]
        mid = c[CHUNK // 2 - 1:CHUNK // 2, :]
        c_last = c[CHUNK - 1:CHUNK, :]
        e_out = jnp.exp(mid - c)
        e_end = jnp.exp(c_last - c)
        at = al * jnp.exp(ce - mid)
        rt = rj * jnp.exp(c - mid)
        bh = be_ * e_out
        kh = kj * e_out
        ae = al * jnp.exp(ce)
        rs = rj * jnp.exp(c)
        be = be_ * e_end
        ke = kj * e_end
        dec = jnp.exp(c_last)
        for gi in range(n_groups):
            sl = slice(gi * gs, (gi + 1) * gs)
            stb = lambda a_: st(a_[:, sl].astype(BF16))
            blocks.append(dict(j=j, si=bi * n_groups + gi, bm=stb(bh), km=stb(kh), am=stb(at), v_bd=stb(vj),
                               aem=stb(ae),
                               bem=stb(be), kem=stb(ke), rt=rt[:, sl].astype(BF16),
                               rs=rs[:, sl].astype(BF16), dec=dec[:, sl]))
    n_st = strict.shape[0]
    for b in blocks:
        b["lhs"] = jnp.concatenate([b["am"], b["rt"]], axis=0)
    for b in blocks:
        prod = _mm_nt(b["lhs"], b["bm"])
        b["n_ab"] = jnp.where(strict, prod[:n_st], 0.0)
        b["a_rb"] = jnp.where(incl, prod[n_st:], 0.0).astype(BF16)
    for b in blocks:
        prod = _mm_nt(b["lhs"], b["km"])
        b["n_ak"] = jnp.where(strict, prod[:n_st], 0.0).astype(BF16)
        b["a_rk"] = jnp.where(incl, prod[n_st:], 0.0).astype(BF16)
    for b, t_inv in zip(blocks, _unit_lower_inverses([b["n_ab"] for b in blocks], blk16, blk32)):
        b["t_inv"] = t_inv.astype(BF16)
    for b in blocks:
        prod = _mm(jnp.concatenate([b["n_ak"], b["a_rk"]], axis=0), b["v_bd"])
        b["w1"] = prod[:n_st]
        b["y0"] = prod[n_st:]
        b["vk"] = _mm_tn(b["v_bd"], b["kem"])
        b["aem_rs"] = jnp.concatenate([b["aem"], b["rs"]], axis=0)
    return blocks


def _rwkv_mixer(p, v_first, mus, prm, seg, tril, tt, nb):
    _, bsz, tp, _ = p.shape
    nt = tp // tt
    has_vres = v_first is not None
    row_spec = pl.BlockSpec((nb, tt, RW_DIM), lambda b, t: (b, t, 0))
    col = functools.partial(_pcol, nb=nb, tt=tt)
    in_specs = [col("rw_r"), col("rw_k"), col("rw_v"), col(CT_SMALL)]
    args = [p, p, p, p]
    if has_vres:
        in_specs.append(row_spec)
        args.append(v_first)
    consts = list(mus) + [prm["w0"], prm["w2"], prm["a0"], prm["a2"], prm["g2"]]
    if has_vres:
        consts += [prm["v0"], prm["v2"]]
    consts += [prm["kk"], prm["ka"], prm["rk"], prm["lnw"], prm["lnb"], seg, tril]
    in_specs += [_const_spec(c.shape) for c in consts]
    args += consts
    out_shape = [jax.ShapeDtypeStruct((bsz, tp, RW_DIM), BF16)]
    out_specs = [row_spec]
    if not has_vres:
        out_shape.append(jax.ShapeDtypeStruct((bsz, tp, RW_DIM), F32))
        out_specs.append(row_spec)
    carry_w = 3 * RW_DIM + IN_TN
    res = pl.pallas_call(
        functools.partial(_rwkv_kernel, has_vres),
        grid=(bsz // nb, nt),
        in_specs=in_specs,
        out_specs=out_specs,
        out_shape=out_shape,
        scratch_shapes=[pltpu.VMEM((nb * (RW_HEADS // RW_GROUP_HEADS), RW_GROUP, RW_GROUP), F32),
                        pltpu.VMEM((nb * SUBLANE, carry_w), F32)],
        compiler_params=_seq_params(),
        name="rwkv_mixer",
    )(*args)
    y = res[0].reshape(bsz * tp, RW_DIM)
    return (y, v_first) if has_vres else (y, res[1])


def _merge_kernel(z_ref, gg_ref, gr_ref, gh_ref, yg_ref, yr_ref, yh_ref, wa_ref, wb_ref, wc_ref, wo_ref, o_ref):
    d = lambda y, w: jnp.dot(y[...], w[...], preferred_element_type=F32)
    sg = lambda g: _sigmoid(jnp.concatenate([g[t] for t in range(g.shape[0])], axis=1).astype(F32))
    m = sg(gg_ref) * d(yg_ref, wa_ref) + sg(gr_ref) * d(yr_ref, wb_ref) + sg(gh_ref) * d(yh_ref, wc_ref)
    o_ref[...] = z_ref[...] + _mm(m, wo_ref[...])


def _merge(z2d, p, y_gla, y_rw, y_hg, wa, wb, wc, wo, tm):
    n = z2d.shape[0]
    row = lambda w: pl.BlockSpec((tm, w), lambda i: (i, 0))
    gate = lambda name: pl.BlockSpec((D_MODEL // IN_TN, tm, IN_TN),
                                     lambda i, cb=_K_OFF[name][0] // D_MODEL: (cb, i, 0))
    full = lambda a: pl.BlockSpec(a.shape, lambda i: (0, 0))
    return pl.pallas_call(
        _merge_kernel,
        grid=(n // tm,),
        in_specs=[row(D_MODEL), gate("gate_gla"), gate("gate_rw"), gate("gate_hg"),
                  row(GLA_V), row(RW_DIM), row(HG_V), full(wa), full(wb), full(wc), full(wo)],
        out_specs=row(D_MODEL),
        out_shape=jax.ShapeDtypeStruct((n, D_MODEL), F32),
        compiler_params=pltpu.CompilerParams(dimension_semantics=("arbitrary",), vmem_limit_bytes=VMEM_LIMIT),
        name="merge",
    )(z2d, p, p, p, y_gla, y_rw, y_hg, wa, wb, wc, wo)


FFN_FC = 256
FFN_HALO = SUBLANE
assert D_FF % FFN_FC == 0 and FFN_FC % LANE == 0 and FFN_HALO >= CONV_W - 1


def _ffn_kernel(final, z_ref, halo_ref, nw_ref, wu_ref, cw_ref, cb_ref, wd_ref, pnw_ref, o_ref, *rest):
    h_ref, act_ref = (None, rest[0]) if final else rest
    tt = z_ref.shape[0]
    zt = z_ref[...]
    nw = nw_ref[...]
    h = jnp.concatenate([_rms(halo_ref[...], nw), _rms(zt, nw)], axis=0).astype(BF16)
    has_prev = pl.program_id(1) > 0
    fc = FFN_FC
    n_chunks = D_FF // fc
    cols = lambda j, half: slice(half * D_FF + j * fc, half * D_FF + (j + 1) * fc)

    def up(j):
        return (jnp.dot(h, wu_ref[:, cols(j, 0)], preferred_element_type=F32),
                jnp.dot(h, wu_ref[:, cols(j, 1)], preferred_element_type=F32))

    def conv(u, cw, cb):
        u = jnp.concatenate([jnp.where(has_prev, u[:FFN_HALO], 0.0), u[FFN_HALO:]], axis=0)
        inner = u * cw[1:2, :] + pltpu.roll(u * cw[0:1, :], 1, 0)
        c = cb + u * cw[2:3, :] + pltpu.roll(inner, 1, 0)
        return c[FFN_HALO:, :]

    split = (n_chunks + 1) // 2
    out = zt
    u_next = up(0)
    for j in range(n_chunks):
        ug, uv = u_next
        if j + 1 < n_chunks:
            u_next = up(j + 1)
        cg = conv(ug, cw_ref[:, cols(j, 0)], cb_ref[:, cols(j, 0)])
        cv = conv(uv, cw_ref[:, cols(j, 1)], cb_ref[:, cols(j, 1)])
        act_ref[:, j * fc:(j + 1) * fc] = (cg * cv / (1.0 + jnp.exp(-cg))).astype(BF16)
        if j + 1 in (split, n_chunks):
            lo = 0 if j + 1 == split else split
            out = out + jnp.dot(act_ref[:, lo * fc:(j + 1) * fc], wd_ref[lo * fc:(j + 1) * fc, :],
                                preferred_element_type=F32)
    if final:
        o_ref[...] = _rms(out, pnw_ref[...])
    else:
        o_ref[...] = out
        h_ref[...] = _rms(out, pnw_ref[...]).astype(h_ref.dtype)


def _ffn(z3d, norm_w, wu, cw, cb, wd, post_norm_w, tt, final):
    bsz, tp, _ = z3d.shape
    hb = tt // FFN_HALO
    once = lambda a: pl.BlockSpec(a.shape, lambda b, t: (0,) * a.ndim, pipeline_mode=pl.Buffered(1))
    tile = pl.BlockSpec((None, tt, D_MODEL), lambda b, t: (b, t, 0))
    out_shape = [jax.ShapeDtypeStruct((bsz, tp, D_MODEL), F32)]
    if not final:
        out_shape.append(jax.ShapeDtypeStruct((bsz, tp, D_MODEL), BF16))
    res = pl.pallas_call(
        functools.partial(_ffn_kernel, final),
        grid=(bsz, tp // tt),
        in_specs=[tile,
                  pl.BlockSpec((None, FFN_HALO, D_MODEL), lambda b, t: (b, jnp.maximum(t * hb - 1, 0), 0)),
                  once(norm_w), once(wu), once(cw), once(cb), once(wd), once(post_norm_w)],
        out_specs=[tile] * len(out_shape),
        out_shape=out_shape,
        scratch_shapes=[pltpu.VMEM((tt, D_FF), BF16)],
        compiler_params=_seq_params(),
        name="conv_ffn",
    )(z3d, z3d, norm_w, wu, cw, cb, wd, post_norm_w)
    return res[0] if final else (res[0], res[1])


def _largest_tile(n, cands):
    for c in cands:
        if n % c == 0:
            return c
    raise ValueError(f"no tile for {n}")


def _ref_cols(w, name):
    off, width = _REF_OFF[name]
    return w[..., off:off + width]


def _pack_columns(w, vres):
    lead = w.shape[:-1]
    zeros = lambda n: jnp.zeros(lead + (n,), w.dtype)
    vr = vres if vres is not None else zeros(RW_V_RANK)
    parts = []
    for name, width in _K_LAYOUT:
        if name == "rw_wa":
            parts.append(jnp.concatenate([_ref_cols(w, "rw_w"), _ref_cols(w, "rw_a")], axis=-1))
        elif name == "misc":
            parts.append(jnp.concatenate(
                [_ref_cols(w, "gla_gk"), vr, zeros(width - GLA_GATE_RANK - RW_V_RANK)], axis=-1))
        else:
            parts.append(_ref_cols(w, name))
    used = sum(wd for _, wd in _K_LAYOUT)
    parts.append(zeros(WP - used))
    return jnp.concatenate(parts, axis=-1)


def _pad_rows(w, lo, total):
    return jnp.pad(w, ((lo, total - lo - w.shape[0]), (0, 0)))


def kernel(x, meta, mix_norm, w_in, w_in_vres, rw_mu, rw_mu_vres, gla_gk_up, gla_gk_bias, gla_norm, rw_w0,
           rw_w2, rw_a0, rw_a2, rw_v0, rw_v2, rw_g2, rw_kk, rw_ka, rw_rk, rw_ln_w, rw_ln_b, hg_lb_logits,
           hg_norm, w_out_gla, w_out_rw, w_out_hg, w_out, ffn_norm, w_up, conv_w, conv_b, w_down, final_norm):
    bsz, seq, _ = x.shape
    depth = w_in.shape[0]
    t_real = seq + N_META
    tp = -(-t_real // CHUNK) * CHUNK
    n = bsz * tp
    tm = _largest_tile(n, (3072, 2048, 1024, 512, 256, 128, 64))
    tm_merge = _largest_tile(n, (512, 256, 128, 64))
    tt = _largest_tile(tp, (704, 512, 384, 256, 192, 128, 64))
    tt_mix = _largest_tile(tp, (3 * CHUNK, 2 * CHUNK, CHUNK))
    nb_mix = _largest_tile(bsz, (4, 2, 1))

    z = jnp.concatenate([jnp.broadcast_to(meta.astype(x.dtype)[None], (bsz, N_META, D_MODEL)), x,
                         jnp.zeros((bsz, tp - t_real, D_MODEL), x.dtype)], axis=1)

    row = lambda a: a.reshape(1, -1).astype(F32)
    t_idx = jnp.arange(tt_mix)
    tril = ((t_idx[:, None] // CHUNK == t_idx[None, :] // CHUNK)
            & (t_idx[None, :] <= t_idx[:, None])).astype(BF16)
    head_of = jnp.arange(RW_GROUP) // RW_HD
    seg = (head_of[:, None] == head_of[None, :]).astype(BF16)
    lb_p = jax.nn.softmax(hg_lb_logits.astype(F32), axis=0)
    lb_all = jnp.cumsum(lb_p, axis=0) - lb_p[0:1]

    v_first = None
    h_next = None
    for i in range(depth):
        vres_w = w_in_vres[i - 1] if i > 0 else None
        vres_mu = rw_mu_vres[i - 1] if i > 0 else None
        w_cat = _pack_columns(w_in[i], vres_w).astype(BF16)
        mu_full = jnp.concatenate([jnp.zeros((W_IN - rw_mu.shape[1],), F32), rw_mu[i].astype(F32)])
        mu_cat = _pack_columns(mu_full, vres_mu).reshape(N_CT, 1, IN_TN)
        mus = [mu_cat[_K_OFF[nm][0] // IN_TN] for nm in ("rw_r", "rw_k", "rw_v", "rw_g")]

        z2d = z.reshape(n, D_MODEL)
        p = _inproj(z2d if h_next is None else h_next.reshape(n, D_MODEL), row(mix_norm[i]), w_cat, tm)

        gk_up_pad = _pad_rows(gla_gk_up[i], MISC_GK, LANE).astype(BF16)
        p4 = p.reshape(N_CT, bsz, tp, IN_TN)
        y_gla, y_hg = _gla_hgrn_mixer(p4, gk_up_pad, row(gla_gk_bias[i]), row(gla_norm[i]), row(lb_all[i]),
                                      row(hg_norm[i]), tril, tt_mix, nb_mix)

        prm = {
            "w0": row(rw_w0[i]), "w2": _pad_rows(rw_w2[i], 0, LANE).astype(BF16),
            "a0": row(rw_a0[i]), "a2": _pad_rows(rw_a2[i], RW_W_RANK, LANE).astype(BF16),
            "g2": rw_g2[i].astype(BF16),
            "kk": row(rw_kk[i]), "ka": row(rw_ka[i]), "rk": row(rw_rk[i]),
            "lnw": row(rw_ln_w[i]), "lnb": row(rw_ln_b[i]),
        }
        if i > 0:
            prm["v0"] = row(rw_v0[i - 1])
            prm["v2"] = _pad_rows(rw_v2[i - 1], MISC_VR, LANE).astype(BF16)
        y_rw, v_first = _rwkv_mixer(p4, v_first, mus, prm, seg, tril, tt_mix, nb_mix)

        z2d = _merge(z2d, p, y_gla, y_rw, y_hg, w_out_gla[i].astype(BF16), w_out_rw[i].astype(BF16),
                     w_out_hg[i].astype(BF16), w_out[i].astype(BF16), tm_merge)

        cw = jnp.pad(conv_w[i].astype(F32), ((0, SUBLANE - CONV_W), (0, 0)))
        last = i == depth - 1
        res = _ffn(z2d.reshape(bsz, tp, D_MODEL), row(ffn_norm[i]), w_up[i].astype(BF16), cw, row(conv_b[i]),
                   w_down[i].astype(BF16), row(final_norm if last else mix_norm[i + 1]), tt, final=last)
        z, h_next = (res, None) if last else res
    return z[:, N_META:t_real]
```
